```python
import math
import jax, jax.numpy as jnp
from jax import lax
import numpy as np

D_MODEL = 1024
BATCH = 4
SEQ = 4096
DEPTH = 4
DEC_BATCH = 32
DEC_SEQ = 1
PAST_LEN = 8192
PAGE_SIZE = 128

N_HEADS = 16
HEAD_DIM = 64
N_KV_HEADS = 4
GROUP = N_HEADS // N_KV_HEADS
ATT_WIDTH = N_HEADS * HEAD_DIM
KV_WIDTH = N_KV_HEADS * HEAD_DIM
CMP_BLK = 32
CMP_STRIDE = 16
CMP_R = CMP_BLK // CMP_STRIDE
CMP_HID = 2 * HEAD_DIM
SEL_BLK = 64
N_SEL = 16
WINDOW = 512
Q_BLK = 128
FORCE_SCORE = 1.0e4
NSA_IN = ATT_WIDTH + 6 * KV_WIDTH + 3 * N_HEADS + ATT_WIDTH
D_INNER = 2 * D_MODEL
M_HEADS = 4
M_HEAD_DIM = D_INNER // M_HEADS
CONV_W = 4
QKV_BLK = 4
M_CHUNK = 64
N_NSA_LAYERS = (DEPTH + 1) // 2
N_MLSTM_LAYERS = DEPTH // 2
RMS_EPS = 1e-6

kernel_name = 'nsa_mlstm_hybrid_step'


def rmsnorm(x, w):
    xf = x.astype(jnp.float32)
    y = xf * lax.rsqrt(jnp.mean(xf * xf, axis=-1, keepdims=True) + RMS_EPS)
    return (y * w.astype(jnp.float32)).astype(x.dtype)


def alibi_slopes():
    return jnp.asarray(np.exp2(-8.0 * np.arange(1, N_HEADS + 1) / N_HEADS), dtype=jnp.float32)


def masked_softmax(s, mask):
    s = jnp.where(mask, s, -jnp.inf)
    m = jnp.max(s, axis=-1, keepdims=True)
    p = jnp.exp(s - jnp.where(jnp.isfinite(m), m, 0.0))
    return p / jnp.maximum(jnp.sum(p, axis=-1, keepdims=True), 1e-30)


def cmp_to_sel(n_cmp, n_blk):
    start = np.arange(n_cmp) * CMP_STRIDE
    blk = np.arange(n_blk) * SEL_BLK
    ov = (start[:, None] < blk[None, :] + SEL_BLK) & (start[:, None] + CMP_BLK > blk[None, :])
    return jnp.asarray(ov, dtype=jnp.float32)


def compress_branch(rows, pe, w1, b1, w2):
    B, Tp = rows.shape[:2]
    n_seg = Tp // CMP_STRIDE
    n_cmp = n_seg - CMP_R + 1
    seg = rows.reshape(B, n_seg, CMP_STRIDE, 2, N_KV_HEADS, HEAD_DIM).astype(jnp.float32)
    w1f = w1.astype(jnp.float32)
    w1r = w1f.reshape(2, CMP_R, CMP_STRIDE, HEAD_DIM, CMP_HID)
    part = jnp.einsum('bnrchd,cjrde->bnjche', seg, w1r)
    hid = (jnp.einsum('cld,clde->ce', pe.astype(jnp.float32), w1f) + b1.astype(jnp.float32))
    hid = hid[None, None, :, None, :]
    for j in range(CMP_R):
        hid = hid + part[:, j:j + n_cmp, j]
    summ = jnp.einsum('bnche,ced->bnchd', jax.nn.silu(hid), w2.astype(jnp.float32))
    kc = summ[:, :, 0].transpose(0, 2, 1, 3)
    vc = summ[:, :, 1].transpose(0, 2, 1, 3)
    c_end = jnp.arange(n_cmp) * CMP_STRIDE + (CMP_BLK - 1)
    return kc, vc, c_end


def sel_blocks(rows):
    B, T = rows.shape[:2]
    return rows.reshape(B, T // SEL_BLK, SEL_BLK, N_KV_HEADS, HEAD_DIM).transpose(0, 3, 1, 2, 4)


def gather_blocks(blocks, ids):
    return jax.vmap(jax.vmap(lambda b_, i_: b_[i_]))(blocks, ids)


def nsa_attend(q, gates, t_pos, kc, vc, c_end, ks, vs, kw, vw, w_pos):
    f32 = jnp.float32
    B, Q = q.shape[0], q.shape[3]
    slopes = alibi_slopes().reshape(1, N_KV_HEADS, GROUP, 1, 1)
    qf = q.astype(f32) * (HEAD_DIM ** -0.5)
    t = t_pos[:, None]
    d_c = (t - c_end[None, :]).astype(f32)
    s_c = jnp.einsum('bkgqd,bknd->bkgqn', qf, kc.astype(f32)) - slopes * d_c
    p_c = masked_softmax(s_c, d_c >= 0)
    o_c = jnp.einsum('bkgqn,bknd->bkgqd', p_c, vc.astype(f32))
    n_cmp, n_blk = kc.shape[2], ks.shape[2]
    imp = jnp.einsum('bkgqn,nj->bkqj', p_c, cmp_to_sel(n_cmp, n_blk))
    blk = jnp.arange(n_blk)[None, :]
    tb = t // SEL_BLK
    forced = (blk == 0) | (blk == tb) | (blk == tb - 1)
    score = jnp.where(forced, FORCE_SCORE, jnp.where(blk * SEL_BLK <= t, imp, -1.0))
    top_s, idx = lax.top_k(score, min(N_SEL, n_blk))
    ks_g = gather_blocks(ks, idx).reshape(B, N_KV_HEADS, Q, -1, HEAD_DIM)
    vs_g = gather_blocks(vs, idx).reshape(B, N_KV_HEADS, Q, -1, HEAD_DIM)
    s_pos = (idx[..., None] * SEL_BLK + jnp.arange(SEL_BLK)).reshape(B, N_KV_HEADS, Q, -1)
    d_s = (t - s_pos).astype(f32)[:, :, None]
    sel_ok = jnp.repeat(top_s > -0.5, SEL_BLK, axis=-1)[:, :, None]
    s_s = jnp.einsum('bkgqd,bkqnd->bkgqn', qf, ks_g.astype(f32)) - slopes * d_s
    p_s = masked_softmax(s_s, sel_ok & (d_s >= 0))
    o_s = jnp.einsum('bkgqn,bkqnd->bkgqd', p_s, vs_g.astype(f32))
    d_w = (t - w_pos[None, :]).astype(f32)
    s_w = jnp.einsum('bkgqd,bkld->bkgql', qf, kw.astype(f32)) - slopes * d_w
    p_w = masked_softmax(s_w, (d_w >= 0) & (d_w <= WINDOW) & (w_pos[None, :] >= 0))
    o_w = jnp.einsum('bkgql,bkld->bkgqd', p_w, vw.astype(f32))
    return gates[..., 0:1] * o_c + gates[..., 1:2] * o_s + gates[..., 2:3] * o_w


def nsa_project(h, w_in):
    B, T, _ = h.shape
    p = h @ w_in
    o1 = ATT_WIDTH
    o2 = o1 + 4 * KV_WIDTH
    o3 = o2 + 2 * KV_WIDTH
    o4 = o3 + 3 * N_HEADS
    q = p[..., :o1].reshape(B, T, N_KV_HEADS, GROUP, HEAD_DIM).transpose(0, 2, 3, 1, 4)
    kv = p[..., o1:o2].reshape(B, T, 4, N_KV_HEADS, HEAD_DIM)
    win = p[..., o2:o3].reshape(B, T, 2, N_KV_HEADS, HEAD_DIM)
    gates = jax.nn.sigmoid(p[..., o3:o4].astype(jnp.float32))
    gates = gates.reshape(B, T, N_KV_HEADS, GROUP, 3).transpose(0, 2, 3, 1, 4)
    z = p[..., o4:]
    return q, kv, win, gates, z


def nsa_output(o, z, w_out):
    B, T = o.shape[0], o.shape[3]
    o = o.transpose(0, 3, 1, 2, 4).reshape(B, T, ATT_WIDTH).astype(z.dtype)
    return (o * jax.nn.silu(z)) @ w_out


def nsa_prompt(h, w_in, w_out, pe, w1, b1, w2):
    B, T, _ = h.shape
    q, kv, win, gates, z = nsa_project(h, w_in)
    kc, vc, c_end = compress_branch(kv[:, :, 0:2], pe, w1, b1, w2)
    ks, vs = sel_blocks(kv[:, :, 2]), sel_blocks(kv[:, :, 3])
    pad = ((0, 0), (0, 0), (WINDOW, 0), (0, 0))
    kw = jnp.pad(win[:, :, 0].transpose(0, 2, 1, 3), pad)
    vw = jnp.pad(win[:, :, 1].transpose(0, 2, 1, 3), pad)

    def q_block(qb):
        s0 = qb * Q_BLK
        t_pos = s0 + jnp.arange(Q_BLK)
        w_pos = s0 - WINDOW + jnp.arange(WINDOW + Q_BLK)
        return nsa_attend(
            lax.dynamic_slice_in_dim(q, s0, Q_BLK, axis=3),
            lax.dynamic_slice_in_dim(gates, s0, Q_BLK, axis=3),
            t_pos, kc, vc, c_end, ks, vs,
            lax.dynamic_slice_in_dim(kw, s0, WINDOW + Q_BLK, axis=2),
            lax.dynamic_slice_in_dim(vw, s0, WINDOW + Q_BLK, axis=2), w_pos)

    o = lax.map(q_block, jnp.arange(T // Q_BLK))
    o = o.transpose(1, 2, 3, 0, 4, 5).reshape(B, N_KV_HEADS, GROUP, T, HEAD_DIM)
    wr = min(WINDOW, T)
    return nsa_output(o, z, w_out), kv, win[:, T - wr:]


def nsa_sample(h, cache_kv_l, cache_win_l, page_table, w_in, w_out, pe, w1, b1, w2):
    B, Tn, _ = h.shape
    past_len = page_table.shape[1] * PAGE_SIZE
    wr = cache_win_l.shape[1]
    q, kv, win, gates, z = nsa_project(h, w_in)
    past = cache_kv_l[page_table].reshape(B, past_len, 4, N_KV_HEADS, HEAD_DIM).astype(kv.dtype)
    T = past_len + Tn
    Tp = -(-T // SEL_BLK) * SEL_BLK
    rows = jnp.pad(jnp.concatenate([past, kv], axis=1), ((0, 0), (0, Tp - T), (0, 0), (0, 0), (0, 0)))
    kc, vc, c_end = compress_branch(rows[:, :, 0:2], pe, w1, b1, w2)
    ks, vs = sel_blocks(rows[:, :, 2]), sel_blocks(rows[:, :, 3])
    win_all = jnp.concatenate([cache_win_l.astype(win.dtype), win], axis=1)
    kw = win_all[:, :, 0].transpose(0, 2, 1, 3)
    vw = win_all[:, :, 1].transpose(0, 2, 1, 3)
    t_pos = past_len + jnp.arange(Tn)
    w_pos = past_len - wr + jnp.arange(wr + Tn)
    o = nsa_attend(q, gates, t_pos, kc, vc, c_end, ks, vs, kw, vw, w_pos)
    return nsa_output(o, z, w_out), kv, win_all[:, Tn:]


def mlstm_cell(q, k, v, ig, fl, C0, n0, m0, chunk):
    B, NH, T, DH = q.shape
    nc = T // chunk

    def to_chunks(a):
        return jnp.moveaxis(a.reshape(B, NH, nc, chunk, *a.shape[3:]), 2, 0)

    causal = jnp.tril(jnp.ones((chunk, chunk), dtype=bool))

    def step(carry, inp):
        C, n, m = carry
        qc, kc, vc, ic, fc = inp
        b = jnp.cumsum(fc, axis=-1)
        D = jnp.where(causal, b[..., :, None] - b[..., None, :] + ic[..., None, :], -jnp.inf)
        inter = b + m[..., None]
        mt = jnp.maximum(jnp.max(D, axis=-1), inter)
        S = jnp.einsum('bhtd,bhsd->bhts', qc, kc) * jnp.exp(D - mt[..., None])
        decay = jnp.exp(inter - mt)
        num = jnp.einsum('bhts,bhse->bhte', S, vc) + decay[..., None] * jnp.einsum('bhtd,bhde->bhte', qc, C)
        den = jnp.sum(S, axis=-1) + decay * jnp.einsum('bhtd,bhd->bht', qc, n)
        hc = num / jnp.maximum(jnp.abs(den), jnp.exp(-mt))[..., None]
        m_new = mt[..., -1]
        w = jnp.exp(b[..., -1:] - b + ic - m_new[..., None])
        carry_decay = jnp.exp(b[..., -1] + m - m_new)
        C = carry_decay[..., None, None] * C + jnp.einsum('bhsd,bhs,bhse->bhde', kc, w, vc)
        n = carry_decay[..., None] * n + jnp.einsum('bhsd,bhs->bhd', kc, w)
        return (C, n, m_new), hc

    carry0 = (C0.astype(jnp.float32), n0.astype(jnp.float32), m0.astype(jnp.float32))
    (C, n, m), hs = lax.scan(step, carry0, (to_chunks(q), to_chunks(k), to_chunks(v), to_chunks(ig), to_chunks(fl)))
    h = jnp.moveaxis(hs, 0, 2).reshape(B, NH, T, DH)
    return h, C, n, m


def mlstm_mixer(h, conv_buf, C0, n0, m0, chunk, w_in, conv_w, conv_b, w_qkv, w_gate, b_gate, norm_w, skip, w_out):
    f32 = jnp.float32
    B, T, _ = h.shape
    proj = h @ w_in
    xm, z = proj[..., :D_INNER], proj[..., D_INNER:]
    xpad = jnp.concatenate([conv_buf.astype(xm.dtype), xm], axis=1)
    conv = conv_b
    for j in range(CONV_W):
        conv = conv + xpad[:, j:j + T] * conv_w[j]
    c = jax.nn.silu(conv)
    nb = D_INNER // QKV_BLK
    cb = c.reshape(B, T, nb, QKV_BLK)
    xb = xm.reshape(B, T, nb, QKV_BLK)
    q = jnp.einsum('btnj,nji->btni', cb, w_qkv[0]).reshape(B, T, D_INNER)
    k = jnp.einsum('btnj,nji->btni', cb, w_qkv[1]).reshape(B, T, D_INNER)
    v = jnp.einsum('btnj,nji->btni', xb, w_qkv[2]).reshape(B, T, D_INNER)
    gpre = (jnp.concatenate([q, k, v], axis=-1) @ w_gate + b_gate).astype(f32)
    ig = gpre[..., :M_HEADS].transpose(0, 2, 1)
    fl = jax.nn.log_sigmoid(gpre[..., M_HEADS:]).transpose(0, 2, 1)

    def heads(a):
        return a.reshape(B, T, M_HEADS, M_HEAD_DIM).transpose(0, 2, 1, 3).astype(f32)

    hc, C, n, m = mlstm_cell(heads(q), heads(k) * (M_HEAD_DIM ** -0.5), heads(v), ig, fl, C0, n0, m0, chunk)
    hn = hc * lax.rsqrt(jnp.mean(hc * hc, axis=-1, keepdims=True) + RMS_EPS)
    hn = hn.transpose(0, 2, 1, 3).reshape(B, T, D_INNER).astype(h.dtype)
    out = (hn * norm_w + skip * c) * jax.nn.silu(z)
    return out @ w_out, C, n, m, xpad[:, T:]


def setup_inputs(seed: int = 0) -> dict:
    key = jax.random.key(seed)
    ks = jax.random.split(key, 32)
    f32 = jnp.float32

    def nrm(k, shape, s):
        return jax.random.normal(k, shape, f32) * s

    n_pages = PAST_LEN // PAGE_SIZE
    n_used = DEC_BATCH * n_pages
    n_pool = n_used + max(n_used // 4, 1)
    wr = min(WINDOW, PAST_LEN)
    NA, NM = N_NSA_LAYERS, N_MLSTM_LAYERS
    page_table = jax.random.permutation(ks[0], n_pool)[:n_used].reshape(DEC_BATCH, n_pages).astype(jnp.int32)
    b_gate = jnp.concatenate([nrm(ks[25], (NM, M_HEADS), 0.1),
                              jnp.linspace(3.0, 6.0, M_HEADS, dtype=f32)[None, :] + nrm(ks[26], (NM, M_HEADS), 0.1)], axis=-1)
    return {
        'x_prompt': nrm(ks[1], (BATCH, SEQ, D_MODEL), 1.0),
        'x_sample': nrm(ks[2], (DEC_BATCH, DEC_SEQ, D_MODEL), 1.0),
        'cache_kv': nrm(ks[3], (NA, n_pool, PAGE_SIZE, 4, N_KV_HEADS, HEAD_DIM), 1.0),
        'cache_win': nrm(ks[4], (NA, DEC_BATCH, wr, 2, N_KV_HEADS, HEAD_DIM), 1.0),
        'state_C': nrm(ks[5], (NM, DEC_BATCH, M_HEADS, M_HEAD_DIM, M_HEAD_DIM), 0.1),
        'state_n': nrm(ks[6], (NM, DEC_BATCH, M_HEADS, M_HEAD_DIM), 0.1),
        'state_m': jax.random.uniform(ks[7], (NM, DEC_BATCH, M_HEADS), f32, 0.0, 3.0),
        'state_conv': nrm(ks[8], (NM, DEC_BATCH, CONV_W - 1, D_INNER), 1.0),
        'page_table': page_table,
        'norm_w': 1.0 + nrm(ks[9], (DEPTH, D_MODEL), 0.02),
        'final_norm_w': 1.0 + nrm(ks[10], (D_MODEL,), 0.02),
        'nsa_w_in': nrm(ks[11], (NA, D_MODEL, NSA_IN), D_MODEL ** -0.5),
        'nsa_w_out': nrm(ks[12], (NA, ATT_WIDTH, D_MODEL), ATT_WIDTH ** -0.5),
        'nsa_cmp_pe': nrm(ks[13], (NA, 2, CMP_BLK, HEAD_DIM), 0.1),
        'nsa_cmp_w1': nrm(ks[14], (NA, 2, CMP_BLK, HEAD_DIM, CMP_HID), (CMP_BLK * HEAD_DIM) ** -0.5),
        'nsa_cmp_b1': nrm(ks[15], (NA, 2, CMP_HID), 0.02),
        'nsa_cmp_w2': nrm(ks[16], (NA, 2, CMP_HID, HEAD_DIM), CMP_HID ** -0.5),
        'm_w_in': nrm(ks[17], (NM, D_MODEL, 2 * D_INNER), D_MODEL ** -0.5),
        'm_conv_w': nrm(ks[18], (NM, CONV_W, D_INNER), CONV_W ** -0.5),
        'm_conv_b': nrm(ks[19], (NM, D_INNER), 0.02),
        'm_w_qkv': nrm(ks[20], (NM, 3, D_INNER // QKV_BLK, QKV_BLK, QKV_BLK), QKV_BLK ** -0.5),
        'm_w_gate': nrm(ks[21], (NM, 3 * D_INNER, 2 * M_HEADS), (3 * D_INNER) ** -0.5),
        'm_b_gate': b_gate,
        'm_norm_w': 1.0 + nrm(ks[22], (NM, D_INNER), 0.02),
        'm_skip': 1.0 + nrm(ks[23], (NM, D_INNER), 0.1),
        'm_w_out': nrm(ks[24], (NM, D_INNER, D_MODEL), D_INNER ** -0.5),
    }


def reference(x_prompt, x_sample, cache_kv, cache_win, state_C, state_n, state_m, state_conv, page_table,
              norm_w, final_norm_w, nsa_w_in, nsa_w_out, nsa_cmp_pe, nsa_cmp_w1, nsa_cmp_b1, nsa_cmp_w2,
              m_w_in, m_conv_w, m_conv_b, m_w_qkv, m_w_gate, m_b_gate, m_norm_w, m_skip, m_w_out):
    f32 = jnp.float32
    yp, ys = x_prompt, x_sample
    kv_p, kv_s, win_p, win_s = [], [], [], []
    C_p, C_s, n_p, n_s, m_p, m_s, cv_p, cv_s = [], [], [], [], [], [], [], []
    for i in range(DEPTH):
        hp = rmsnorm(yp, norm_w[i])
        hs = rmsnorm(ys, norm_w[i])
        l = i // 2
        if i % 2 == 0:
            prm = (nsa_w_in[l], nsa_w_out[l], nsa_cmp_pe[l], nsa_cmp_w1[l], nsa_cmp_b1[l], nsa_cmp_w2[l])
            dp, kvp, wp = nsa_prompt(hp, *prm)
            ds, kvs, wsm = nsa_sample(hs, cache_kv[l], cache_win[l], page_table, *prm)
            kv_p.append(kvp); kv_s.append(kvs); win_p.append(wp); win_s.append(wsm)
        else:
            prm = (m_w_in[l], m_conv_w[l], m_conv_b[l], m_w_qkv[l], m_w_gate[l], m_b_gate[l],
                   m_norm_w[l], m_skip[l], m_w_out[l])
            Bp, Tp = hp.shape[:2]
            dp, Cp, np_, mp, cp = mlstm_mixer(
                hp, jnp.zeros((Bp, CONV_W - 1, D_INNER), hp.dtype),
                jnp.zeros((Bp, M_HEADS, M_HEAD_DIM, M_HEAD_DIM), f32), jnp.zeros((Bp, M_HEADS, M_HEAD_DIM), f32),
                jnp.full((Bp, M_HEADS), -jnp.inf, f32), min(M_CHUNK, Tp), *prm)
            ds, Cs, ns_, ms, cs = mlstm_mixer(hs, state_conv[l], state_C[l], state_n[l], state_m[l], hs.shape[1], *prm)
            C_p.append(Cp); C_s.append(Cs); n_p.append(np_); n_s.append(ns_)
            m_p.append(mp); m_s.append(ms); cv_p.append(cp); cv_s.append(cs)
        yp = yp + dp
        ys = ys + ds
    y_prompt = rmsnorm(yp, final_norm_w)
    y_sample = rmsnorm(ys, final_norm_w)
    return (y_prompt, y_sample, jnp.stack(kv_p), jnp.stack(kv_s), jnp.stack(win_p), jnp.stack(win_s),
            jnp.stack(C_p), jnp.stack(C_s), jnp.stack(n_p), jnp.stack(n_s), jnp.stack(m_p), jnp.stack(m_s),
            jnp.stack(cv_p), jnp.stack(cv_s))
```

```python
import functools
import math

import jax
import jax.numpy as jnp
import numpy as np
from jax import lax
from jax.experimental import pallas as pl
from jax.experimental.pallas import tpu as pltpu

F32 = jnp.float32
BF16 = jnp.bfloat16
HI = lax.Precision.HIGHEST

D_MODEL = 1024
DEPTH = 4
N_HEADS = 16
HEAD_DIM = 64
N_KV_HEADS = 4
GROUP = N_HEADS // N_KV_HEADS
ATT_WIDTH = N_HEADS * HEAD_DIM
KV_WIDTH = N_KV_HEADS * HEAD_DIM
CMP_BLK = 32
CMP_STRIDE = 16
CMP_R = CMP_BLK // CMP_STRIDE
CMP_HID = 2 * HEAD_DIM
SEL_BLK = 64
N_SEL = 16
WINDOW = 512
FORCE_SCORE = 1.0e4
D_INNER = 2 * D_MODEL
M_HEADS = 4
M_HEAD_DIM = D_INNER // M_HEADS
CONV_W = 4
QKV_BLK = 4
RMS_EPS = 1e-6
PAGE_SIZE = 128

LANES = 128
VMEM_LIMIT = 56 * 1024 * 1024
NEG = -1.0e30
M_INIT = -5.0e29
Q_SCALE = HEAD_DIM ** -0.5
NT = (((1,), (1,)), ((), ()))
TN = (((0,), (0,)), ((), ()))


def _cparams(*sem):
    return pltpu.CompilerParams(dimension_semantics=sem, vmem_limit_bytes=VMEM_LIMIT)


def _silu(x):
    return x * jax.nn.sigmoid(x)


def _alibi_slopes():
    return jnp.asarray(np.exp2(-8.0 * np.arange(1, N_HEADS + 1) / N_HEADS), dtype=F32)


def _rmsnorm(x, w):
    return x * lax.rsqrt(jnp.mean(x * x, axis=-1, keepdims=True) + RMS_EPS) * w


def _rms_proj_kernel(x_ref, nw_ref, *refs, acts, n_chunk, exact):
    n = len(acts)
    w_refs, o_refs = refs[:n], refs[n:]
    h = _rmsnorm(x_ref[...], nw_ref[...])
    if not exact:
        h = h.astype(BF16)
    for w_ref, o_ref, act in zip(w_refs, o_refs, acts):
        width = w_ref.shape[1]
        for n0 in range(0, width, n_chunk):
            n1 = min(width, n0 + n_chunk)
            if exact:
                y = jnp.dot(h, w_ref[:, n0:n1], preferred_element_type=F32, precision=HI)
            else:
                y = jnp.dot(h, w_ref[:, n0:n1], preferred_element_type=F32)
            if act == "sigmoid":
                y = jax.nn.sigmoid(y)
            o_ref[:, n0:n1] = y


def rms_proj(x, nw, weights, acts, *, tm, exact=False, name="rms_proj"):
    m, k = x.shape
    assert m % tm == 0
    in_specs = [pl.BlockSpec((tm, k), lambda i: (i, 0)), pl.BlockSpec((1, k), lambda i: (0, 0))]
    in_specs += [pl.BlockSpec(w.shape, lambda i: (0, 0)) for w in weights]
    out_specs = [pl.BlockSpec((tm, w.shape[1]), lambda i: (i, 0)) for w in weights]
    out_shape = [jax.ShapeDtypeStruct((m, w.shape[1]), F32) for w in weights]
    return pl.pallas_call(
        functools.partial(_rms_proj_kernel, acts=tuple(acts), n_chunk=512, exact=exact),
        grid=(m // tm,), in_specs=in_specs, out_specs=out_specs, out_shape=out_shape,
        compiler_params=_cparams("parallel"), name=name,
    )(x, nw.reshape(1, k), *weights)


def _compress_kernel(x_ref, w1_ref, pe_ref, w1f_ref, b1_ref, w2_ref, o_ref, *, n_seg):
    x = x_ref[0, 0, 0]
    p0 = jnp.dot(x, w1_ref[0, 0], preferred_element_type=F32)
    p1 = jnp.dot(x, w1_ref[0, 1], preferred_element_type=F32)
    hidc = jnp.dot(pe_ref[0], w1f_ref[0], preferred_element_type=F32)[0:1] + b1_ref[0]
    hid = hidc + p0 + pltpu.roll(p1, n_seg - 1, 0)
    o_ref[0, 0, 0] = jnp.dot(_silu(hid).astype(BF16), w2_ref[0], preferred_element_type=F32)


def compress_tokens(x, pe, w1, b1, w2):
    b, _, hkv, n_seg, _ = x.shape
    w1r = w1.reshape(2, CMP_R, CMP_STRIDE * HEAD_DIM, CMP_HID).astype(BF16)
    w1f = w1.reshape(2, CMP_BLK * HEAD_DIM, CMP_HID).astype(BF16)
    pe8 = jnp.broadcast_to(pe.reshape(2, 1, CMP_BLK * HEAD_DIM), (2, 8, CMP_BLK * HEAD_DIM)).astype(BF16)
    return pl.pallas_call(
        functools.partial(_compress_kernel, n_seg=n_seg),
        grid=(b, 2, hkv),
        in_specs=[
            pl.BlockSpec((1, 1, 1, n_seg, CMP_STRIDE * HEAD_DIM), lambda i, c, h: (i, c, h, 0, 0)),
            pl.BlockSpec((1, CMP_R, CMP_STRIDE * HEAD_DIM, CMP_HID), lambda i, c, h: (c, 0, 0, 0)),
            pl.BlockSpec((1, 8, CMP_BLK * HEAD_DIM), lambda i, c, h: (c, 0, 0)),
            pl.BlockSpec((1, CMP_BLK * HEAD_DIM, CMP_HID), lambda i, c, h: (c, 0, 0)),
            pl.BlockSpec((1, 1, CMP_HID), lambda i, c, h: (c, 0, 0)),
            pl.BlockSpec((1, CMP_HID, HEAD_DIM), lambda i, c, h: (c, 0, 0)),
        ],
        out_specs=pl.BlockSpec((1, 1, 1, n_seg, HEAD_DIM), lambda i, c, h: (i, c, h, 0, 0)),
        out_shape=jax.ShapeDtypeStruct((b, 2, hkv, n_seg, HEAD_DIM), F32),
        compiler_params=_cparams("parallel", "parallel", "parallel"), name="nsa_compress",
    )(x, w1r, pe8, w1f, b1.reshape(2, 1, CMP_HID), w2.astype(BF16))


def _topk_mask_t(score_t, n_valid, k):
    r = score_t.shape[0]
    jidx = lax.broadcasted_iota(jnp.int32, (r, 1), 0)
    cnt = jnp.zeros(score_t.shape, F32)
    for i in range(n_valid):
        row = score_t[i:i + 1, :]
        beats = (row > score_t) | ((jidx > i) & (row == score_t))
        cnt = cnt + jnp.where(beats, 1.0, 0.0)
    return cnt < float(k)


def _cmp_sel_kernel(slopes_ref, q_ref, kc_ref, vc_ref, g_ref, ov_ref, oc_ref, sel_ref, *, tq, n_seg, n_blk):
    kh = pl.program_id(1)
    t0 = pl.program_id(2) * tq
    t_col = t0 + lax.broadcasted_iota(jnp.int32, (tq, 1), 0)
    cend = lax.broadcasted_iota(jnp.int32, (1, n_seg), 1) * CMP_STRIDE + (CMP_BLK - 1)
    d_c = (t_col - cend).astype(F32)
    ok = d_c >= 0.0
    kc = kc_ref[0, 0, 0].astype(BF16)
    vc = vc_ref[0, 0, 0].astype(BF16)
    imp = jnp.zeros((tq, LANES), F32)
    outs = []
    for g in range(GROUP):
        qg = (q_ref[0, :, g * HEAD_DIM:(g + 1) * HEAD_DIM] * Q_SCALE).astype(BF16)
        s = lax.dot_general(qg, kc, NT, preferred_element_type=F32) - slopes_ref[kh * GROUP + g] * d_c
        s = jnp.where(ok, s, NEG)
        m = jnp.max(s, axis=-1, keepdims=True)
        p = jnp.where(ok, jnp.exp(s - m), 0.0)
        pn = (p / jnp.maximum(jnp.sum(p, axis=-1, keepdims=True), 1e-30)).astype(BF16)
        o = jnp.dot(pn, vc, preferred_element_type=F32)
        outs.append(o * g_ref[0, 0, :, 3 * g:3 * g + 1])
        imp = imp + jnp.dot(pn, ov_ref[...], preferred_element_type=F32)
    oc_ref[0] = jnp.concatenate(outs, axis=1)
    blk = lax.broadcasted_iota(jnp.int32, (1, LANES), 1)
    tb = lax.shift_right_logical(t_col, int(math.log2(SEL_BLK)))
    forced = (blk == 0) | (blk == tb) | (blk == tb - 1)
    score = jnp.where(forced, FORCE_SCORE, jnp.where(blk * SEL_BLK <= t_col, imp, -1.0))
    score = jnp.where(blk < n_blk, score, -2.0)
    score_t = score.T
    sel_t = _topk_mask_t(score_t, n_blk, min(N_SEL, n_blk)) & (score_t > -0.5)
    sel_ref[0, 0] = jnp.where(sel_t, 1.0, 0.0).T


def cmp_select(q, kvc, gates_r, slopes, ov, *, tq):
    b, t, _ = q.shape
    n_seg = kvc.shape[3]
    n_blk = t // SEL_BLK
    grid_spec = pltpu.PrefetchScalarGridSpec(
        num_scalar_prefetch=1, grid=(b, N_KV_HEADS, t // tq),
        in_specs=[
            pl.BlockSpec((1, tq, GROUP * HEAD_DIM), lambda i, k, j, s: (i, j, k)),
            pl.BlockSpec((1, 1, 1, n_seg, HEAD_DIM), lambda i, k, j, s: (i, 0, k, 0, 0)),
            pl.BlockSpec((1, 1, 1, n_seg, HEAD_DIM), lambda i, k, j, s: (i, 1, k, 0, 0)),
            pl.BlockSpec((1, 1, tq, 3 * GROUP), lambda i, k, j, s: (i, k, j, 0)),
            pl.BlockSpec((n_seg, LANES), lambda i, k, j, s: (0, 0)),
        ],
        out_specs=[
            pl.BlockSpec((1, tq, GROUP * HEAD_DIM), lambda i, k, j, s: (i, j, k)),
            pl.BlockSpec((1, 1, tq, LANES), lambda i, k, j, s: (i, k, j, 0)),
        ],
    )
    return pl.pallas_call(
        functools.partial(_cmp_sel_kernel, tq=tq, n_seg=n_seg, n_blk=n_blk),
        grid_spec=grid_spec,
        out_shape=[jax.ShapeDtypeStruct((b, t, ATT_WIDTH), F32), jax.ShapeDtypeStruct((b, N_KV_HEADS, t, LANES), F32)],
        compiler_params=_cparams("parallel", "parallel", "parallel"), name="nsa_cmp_select",
    )(slopes, q, kvc, kvc, gates_r, ov)


def _cmp_to_sel_matrix(n_rows, n_blk):
    start = np.arange(n_rows) * CMP_STRIDE
    blk = np.arange(LANES) * SEL_BLK
    ov = (start[:, None] < blk[None, :] + SEL_BLK) & (start[:, None] + CMP_BLK > blk[None, :]) & (np.arange(LANES)[None, :] < n_blk)
    return jnp.asarray(ov, dtype=BF16)


def _stack_heads(q_blk, extra=None):
    parts = []
    for g in range(GROUP):
        qg = q_blk[:, g * HEAD_DIM:(g + 1) * HEAD_DIM] * Q_SCALE
        if extra is not None:
            qg = jnp.concatenate([qg, extra], axis=1)
        parts.append(qg)
    return jnp.concatenate(parts, axis=0).astype(BF16)


def _head_cols(fn, tq):
    return jnp.concatenate([jnp.full((tq, 1), fn(g), F32) for g in range(GROUP)], axis=0)


def _sel_attn_kernel(slopes_ref, q_ref, sel_ref, ka_ref, v_ref, g_ref, o_ref, *, tq, tk):
    kh = pl.program_id(1)
    t0 = pl.program_id(2) * tq
    mask_feat = (sel_ref[0, 0][:, 0:HEAD_DIM] - 1.0) * 1.0e30
    qa = _stack_heads(q_ref[0], mask_feat)
    slope_col = _head_cols(lambda g: slopes_ref[kh * GROUP + g], tq)
    t_col = t0 + lax.broadcasted_iota(jnp.int32, (tq, 1), 0)
    t_col4 = jnp.concatenate([t_col] * GROUP, axis=0)

    def body(kt, carry):
        m, l, acc = carry
        ks = pl.multiple_of(kt * tk, tk)
        s = lax.dot_general(qa, ka_ref[0, 0, pl.ds(ks, tk), :], NT, preferred_element_type=F32)
        pos = ks + lax.broadcasted_iota(jnp.int32, (1, tk), 1)
        s = s + slope_col * (pos - t0).astype(F32)
        s = jnp.where(pos <= t_col4, s, NEG)
        m_new = jnp.maximum(m, jnp.max(s, axis=-1, keepdims=True))
        alpha = jnp.exp(m - m_new)
        p = jnp.exp(s - m_new)
        l = alpha * l + jnp.sum(p, axis=-1, keepdims=True)
        acc = alpha * acc + jnp.dot(p.astype(BF16), v_ref[0, 0, pl.ds(ks, tk), :], preferred_element_type=F32)
        return m_new, l, acc

    n_kt = (t0 + tq + tk - 1) // tk
    init = (jnp.full((GROUP * tq, 1), M_INIT, F32), jnp.zeros((GROUP * tq, 1), F32), jnp.zeros((GROUP * tq, HEAD_DIM), F32))
    _, l, acc = lax.fori_loop(0, n_kt, body, init)
    o = acc / jnp.maximum(l, 1e-30)
    o_ref[0] = jnp.concatenate(
        [o[g * tq:(g + 1) * tq] * g_ref[0, 0, :, 3 * g + 1:3 * g + 2] for g in range(GROUP)], axis=1)


def sel_attention(q, sel, k_aug, v_sel, gates_r, slopes, *, tq, tk):
    b, t, _ = q.shape
    grid_spec = pltpu.PrefetchScalarGridSpec(
        num_scalar_prefetch=1, grid=(b, N_KV_HEADS, t // tq),
        in_specs=[
            pl.BlockSpec((1, tq, GROUP * HEAD_DIM), lambda i, k, j, s: (i, j, k)),
            pl.BlockSpec((1, 1, tq, LANES), lambda i, k, j, s: (i, k, j, 0)),
            pl.BlockSpec((1, 1, t, LANES), lambda i, k, j, s: (i, k, 0, 0)),
            pl.BlockSpec((1, 1, t, HEAD_DIM), lambda i, k, j, s: (i, k, 0, 0)),
            pl.BlockSpec((1, 1, tq, 3 * GROUP), lambda i, k, j, s: (i, k, j, 0)),
        ],
        out_specs=pl.BlockSpec((1, tq, GROUP * HEAD_DIM), lambda i, k, j, s: (i, j, k)),
    )
    return pl.pallas_call(
        functools.partial(_sel_attn_kernel, tq=tq, tk=tk), grid_spec=grid_spec,
        out_shape=jax.ShapeDtypeStruct((b, t, ATT_WIDTH), F32),
        compiler_params=_cparams("parallel", "parallel", "arbitrary"), name="nsa_sel_attn",
    )(slopes, q, sel, k_aug, v_sel, gates_r)


def _win_attn_kernel(slopes_ref, q_ref, k_ref, v_ref, g_ref, o_ref, *, tq, span):
    kh = pl.program_id(1)
    t0 = pl.program_id(2) * tq
    qs = _stack_heads(q_ref[0])
    slope_col = _head_cols(lambda g: slopes_ref[kh * GROUP + g], tq)
    t_col = t0 + lax.broadcasted_iota(jnp.int32, (tq, 1), 0)
    t_col4 = jnp.concatenate([t_col] * GROUP, axis=0)
    start = pl.multiple_of(jnp.maximum(t0 + tq - span, 0), tq)
    s = lax.dot_general(qs, k_ref[0, 0, pl.ds(start, span), :], NT, preferred_element_type=F32)
    d = t_col4 - (start + lax.broadcasted_iota(jnp.int32, (1, span), 1))
    ok = (d >= 0) & (d <= WINDOW)
    s = jnp.where(ok, s - slope_col * d.astype(F32), NEG)
    m = jnp.max(s, axis=-1, keepdims=True)
    p = jnp.where(ok, jnp.exp(s - m), 0.0)
    l = jnp.sum(p, axis=-1, keepdims=True)
    o = jnp.dot(p.astype(BF16), v_ref[0, 0, pl.ds(start, span), :], preferred_element_type=F32) / jnp.maximum(l, 1e-30)
    o_ref[0] = jnp.concatenate(
        [o[g * tq:(g + 1) * tq] * g_ref[0, 0, :, 3 * g + 2:3 * g + 3] for g in range(GROUP)], axis=1)


def win_attention(q, k_win, v_win, gates_r, slopes, *, tq):
    b, t, _ = q.shape
    span = WINDOW + tq
    assert t >= span and span % tq == 0
    grid_spec = pltpu.PrefetchScalarGridSpec(
        num_scalar_prefetch=1, grid=(b, N_KV_HEADS, t // tq),
        in_specs=[
            pl.BlockSpec((1, tq, GROUP * HEAD_DIM), lambda i, k, j, s: (i, j, k)),
            pl.BlockSpec((1, 1, t, HEAD_DIM), lambda i, k, j, s: (i, k, 0, 0)),
            pl.BlockSpec((1, 1, t, HEAD_DIM), lambda i, k, j, s: (i, k, 0, 0)),
            pl.BlockSpec((1, 1, tq, 3 * GROUP), lambda i, k, j, s: (i, k, j, 0)),
        ],
        out_specs=pl.BlockSpec((1, tq, GROUP * HEAD_DIM), lambda i, k, j, s: (i, j, k)),
    )
    return pl.pallas_call(
        functools.partial(_win_attn_kernel, tq=tq, span=span), grid_spec=grid_spec,
        out_shape=jax.ShapeDtypeStruct((b, t, ATT_WIDTH), F32),
        compiler_params=_cparams("parallel", "parallel", "arbitrary"), name="nsa_win_attn",
    )(slopes, q, k_win, v_win, gates_r)


def _nsa_out_kernel(oc_ref, os_ref, ow_ref, z_ref, x_ref, w_ref, y_ref, *, exact):
    a = (oc_ref[...] + os_ref[...] + ow_ref[...]) * _silu(z_ref[...])
    if exact:
        y_ref[...] = x_ref[...] + jnp.dot(a, w_ref[...], preferred_element_type=F32, precision=HI)
    else:
        y_ref[...] = x_ref[...] + jnp.dot(a.astype(BF16), w_ref[...], preferred_element_type=F32)


def nsa_out(o_c, o_s, o_w, z, x, w_out, *, tm, exact=False):
    m, d = x.shape
    row = pl.BlockSpec((tm, d), lambda i: (i, 0))
    return pl.pallas_call(
        functools.partial(_nsa_out_kernel, exact=exact), grid=(m // tm,),
        in_specs=[row, row, row, row, row, pl.BlockSpec(w_out.shape, lambda i: (0, 0))],
        out_specs=row, out_shape=jax.ShapeDtypeStruct((m, d), F32),
        compiler_params=_cparams("parallel"), name="nsa_out",
    )(o_c, o_s, o_w, z, x, w_out)


def _split_nsa_w_in(w_in, dtype):
    o1 = ATT_WIDTH
    o2 = o1 + 4 * KV_WIDTH
    o3 = o2 + 2 * KV_WIDTH
    o4 = o3 + 3 * N_HEADS
    wg = jnp.pad(w_in[:, o3:o4], ((0, 0), (0, LANES - 3 * N_HEADS)))
    return [w.astype(dtype) for w in (w_in[:, :o1], w_in[:, o1:o2], w_in[:, o2:o3], wg, w_in[:, o4:])]


NSA_ACTS = (None, None, None, "sigmoid", None)


def nsa_prompt_layer(x, nw, w_in, w_out, pe, w1, b1, w2):
    b, t, d = x.shape
    n_blk = t // SEL_BLK
    assert n_blk <= HEAD_DIM
    q, kv, win, gates, z = rms_proj(x.reshape(b * t, d), nw, _split_nsa_w_in(w_in, BF16), NSA_ACTS, tm=256, name="nsa_in_proj")
    q = q.reshape(b, t, ATT_WIDTH)
    kv5 = kv.reshape(b, t, 4, N_KV_HEADS, HEAD_DIM)
    win5 = win.reshape(b, t, 2, N_KV_HEADS, HEAD_DIM)
    kvt = kv5.transpose(0, 2, 3, 1, 4).astype(BF16)
    wint = win5.transpose(0, 2, 3, 1, 4).astype(BF16)
    gates_r = gates[:, :3 * N_HEADS].reshape(b, t, N_KV_HEADS, 3 * GROUP).transpose(0, 2, 1, 3)
    slopes = _alibi_slopes()
    n_seg = t // CMP_STRIDE
    kvc = compress_tokens(kvt[:, 0:2].reshape(b, 2, N_KV_HEADS, n_seg, CMP_STRIDE * HEAD_DIM), pe, w1, b1, w2)
    o_c, sel = cmp_select(q, kvc, gates_r, slopes, _cmp_to_sel_matrix(n_seg, n_blk), tq=128)
    onehot = (jnp.arange(t)[:, None] // SEL_BLK == jnp.arange(HEAD_DIM)[None, :]).astype(BF16)
    k_aug = jnp.concatenate([kvt[:, 2], jnp.broadcast_to(onehot, (b, N_KV_HEADS, t, HEAD_DIM))], axis=-1)
    o_s = sel_attention(q, sel, k_aug, kvt[:, 3], gates_r, slopes, tq=128, tk=512)
    o_w = win_attention(q, wint[:, 0], wint[:, 1], gates_r, slopes, tq=128)
    r2 = lambda a: a.reshape(b * t, -1)
    y = nsa_out(r2(o_c), r2(o_s), r2(o_w), z, r2(x), w_out.astype(BF16), tm=256)
    wr = min(WINDOW, t)
    return y.reshape(b, t, d), kv5, win5[:, t - wr:]


def _log_sigmoid(x):
    return jnp.minimum(x, 0.0) - jnp.log(1.0 + jnp.exp(-jnp.abs(x)))


def _mconv_body(shifted, xm, cw_ref, cb_ref, wbd_ref, wg_ref, bg_ref, q_ref, k_ref, v_ref, c_ref, g_ref, exact):
    conv = cb_ref[...]
    for j in range(CONV_W):
        conv = conv + shifted[j] * cw_ref[j:j + 1, :]
    c = _silu(conv)
    c_ref[...] = c
    cast = (lambda a: a) if exact else (lambda a: a.astype(BF16))
    kw = dict(preferred_element_type=F32, precision=HI) if exact else dict(preferred_element_type=F32)
    gpre = bg_ref[...]
    for m, (src, dst) in enumerate(((c, q_ref), (c, k_ref), (xm, v_ref))):
        for gi in range(D_INNER // LANES):
            sl = slice(gi * LANES, (gi + 1) * LANES)
            y = jnp.dot(cast(src[:, sl]), wbd_ref[m, gi], **kw)
            gpre = gpre + jnp.dot(cast(y), wg_ref[m * D_INNER + gi * LANES:m * D_INNER + (gi + 1) * LANES, :], **kw)
            dst[:, sl] = y * (M_HEAD_DIM ** -0.5) if m == 1 else y
    lane = lax.broadcasted_iota(jnp.int32, gpre.shape, 1)
    g_ref[...] = jnp.where(lane < M_HEADS, gpre, _log_sigmoid(gpre))


def _mconv_prompt_kernel(xm_ref, halo_ref, cw_ref, cb_ref, wbd_ref, wg_ref, bg_ref, q_ref, k_ref, v_ref, c_ref, g_ref, *, tm):
    xm = xm_ref[0]
    halo = jnp.where(pl.program_id(1) == 0, 0.0, halo_ref[0])
    ext = jnp.concatenate([halo, xm], axis=0)
    shifted = [ext[5 + j:5 + j + tm] for j in range(CONV_W - 1)] + [xm]
    _mconv_body(shifted, xm, cw_ref, cb_ref, wbd_ref, wg_ref, bg_ref, q_ref.at[0], k_ref.at[0], v_ref.at[0], c_ref.at[0],
                g_ref.at[0], False)


def _mlstm_small_weights(w_qkv, w_gate, b_gate, dtype):
    nb = LANES // QKV_BLK
    w = w_qkv.reshape(3, D_INNER // LANES, nb, QKV_BLK, QKV_BLK)
    eye = jnp.eye(nb, dtype=w.dtype)
    wbd = jnp.einsum("mgnji,nk->mgnjki", w, eye).reshape(3, D_INNER // LANES, LANES, LANES)
    wg = jnp.pad(w_gate, ((0, 0), (0, LANES - 2 * M_HEADS)))
    bg = jnp.pad(b_gate, (0, LANES - 2 * M_HEADS)).reshape(1, LANES)
    return wbd.astype(dtype), wg.astype(dtype), bg


def mconv_prompt(xm, conv_w, conv_b, w_qkv, w_gate, b_gate, *, tm):
    b, t, _ = xm.shape
    wbd, wg, bg = _mlstm_small_weights(w_qkv, w_gate, b_gate, BF16)
    row = pl.BlockSpec((1, tm, D_INNER), lambda i, j: (i, j, 0))
    full = lambda a: pl.BlockSpec(a.shape, lambda i, j: (0,) * a.ndim)
    cb = conv_b.reshape(1, D_INNER)
    return pl.pallas_call(
        functools.partial(_mconv_prompt_kernel, tm=tm), grid=(b, t // tm),
        in_specs=[row, pl.BlockSpec((1, 8, D_INNER), lambda i, j: (i, jnp.maximum(j * (tm // 8) - 1, 0), 0)),
                  full(conv_w), full(cb), full(wbd), full(wg), full(bg)],
        out_specs=[row, row, row, row, pl.BlockSpec((1, tm, LANES), lambda i, j: (i, j, 0))],
        out_shape=[jax.ShapeDtypeStruct((b, t, D_INNER), F32)] * 4 + [jax.ShapeDtypeStruct((b, t, LANES), F32)],
        compiler_params=_cparams("parallel", "parallel"), name="mlstm_conv_qkv",
    )(xm, xm, conv_w, cb, wbd, wg, bg)


def _mlstm_cell_kernel(q_ref, k_ref, v_ref, gc_ref, gr_ref, c0_ref, n0_ref, m0_ref, h_ref, cf_ref, nf_ref, mf_ref,
                       c_s, n_s, m_s, *, chunk, n_chunks):
    ci = pl.program_id(2)

    @pl.when(ci == 0)
    def _():
        c_s[...] = c0_ref[0, 0]
        n_s[...] = n0_ref[0, 0]
        m_s[...] = m0_ref[0, 0]

    q, k, v = q_ref[0], k_ref[0], v_ref[0]
    icol, fcol = gc_ref[0, 0, :, 0:1], gc_ref[0, 0, :, 1:2]
    irow, frow = gr_ref[0, 0, 0:1, :], gr_ref[0, 0, 1:2, :]
    ri = lax.broadcasted_iota(jnp.int32, (chunk, chunk), 0)
    cj = lax.broadcasted_iota(jnp.int32, (chunk, chunk), 1)
    causal = cj <= ri
    b_col = jnp.dot(jnp.where(causal, 1.0, 0.0), jnp.broadcast_to(fcol, (chunk, LANES)),
                    preferred_element_type=F32, precision=HI)[:, 0:1]
    b_row = jnp.dot(jnp.broadcast_to(frow, (8, chunk)), jnp.where(ri <= cj, 1.0, 0.0),
                    preferred_element_type=F32, precision=HI)[0:1, :]
    dmat = jnp.where(causal, b_col - b_row + irow, -jnp.inf)
    m_prev = m_s[0:1, 0:1]
    inter = b_col + m_prev
    mt = jnp.maximum(jnp.max(dmat, axis=-1, keepdims=True), inter)
    qb, kb, vb = q.astype(BF16), k.astype(BF16), v.astype(BF16)
    s = lax.dot_general(qb, kb, NT, preferred_element_type=F32) * jnp.exp(dmat - mt)
    decay = jnp.exp(inter - mt)
    num = (jnp.dot(s.astype(BF16), vb, preferred_element_type=F32)
           + decay * jnp.dot(qb, c_s[...].astype(BF16), preferred_element_type=F32))
    den = jnp.sum(s, axis=-1, keepdims=True) + decay * jnp.sum(q * n_s[...], axis=-1, keepdims=True)
    hc = num / jnp.maximum(jnp.abs(den), jnp.exp(-mt))
    h_ref[0] = hc * lax.rsqrt(jnp.mean(hc * hc, axis=-1, keepdims=True) + RMS_EPS)
    m_new = mt[chunk - 1:chunk, :]
    b_last = b_col[chunk - 1:chunk, :]
    kw = k * jnp.exp(b_last - b_col + icol - m_new)
    carry = jnp.exp(b_last + m_prev - m_new)
    c_s[...] = carry * c_s[...] + lax.dot_general(kw.astype(BF16), vb, TN, preferred_element_type=F32)
    n_s[...] = carry * n_s[...] + jnp.sum(kw, axis=0, keepdims=True)
    m_s[...] = jnp.broadcast_to(m_new, m_s.shape)

    @pl.when(ci == n_chunks - 1)
    def _():
        cf_ref[0, 0] = c_s[...]
        nf_ref[0, 0] = n_s[...]
        mf_ref[0, 0] = m_s[...]


def mlstm_cell(q, k, v, gates, c0, n0, m0, *, chunk):
    b, t, _ = q.shape
    n_chunks = t // chunk
    g_col = jnp.stack([gates[..., :M_HEADS], gates[..., M_HEADS:2 * M_HEADS]], axis=-1).transpose(0, 2, 1, 3)
    g_row = g_col.transpose(0, 1, 3, 2)
    m0b = jnp.broadcast_to(m0[:, :, None, None], (b, M_HEADS, 8, LANES))
    n0r = n0.reshape(b, M_HEADS, 1, M_HEAD_DIM)
    head = pl.BlockSpec((1, chunk, M_HEAD_DIM), lambda i, h, c: (i, c, h))
    st = lambda *blk: pl.BlockSpec((1, 1) + blk, lambda i, h, c: (i, h, 0, 0))
    hn, cf, nf, mf = pl.pallas_call(
        functools.partial(_mlstm_cell_kernel, chunk=chunk, n_chunks=n_chunks), grid=(b, M_HEADS, n_chunks),
        in_specs=[head, head, head,
                  pl.BlockSpec((1, 1, chunk, 2), lambda i, h, c: (i, h, c, 0)),
                  pl.BlockSpec((1, 1, 2, chunk), lambda i, h, c: (i, h, 0, c)),
                  st(M_HEAD_DIM, M_HEAD_DIM), st(1, M_HEAD_DIM), st(8, LANES)],
        out_specs=[head, st(M_HEAD_DIM, M_HEAD_DIM), st(1, M_HEAD_DIM), st(8, LANES)],
        out_shape=[jax.ShapeDtypeStruct((b, t, D_INNER), F32), jax.ShapeDtypeStruct((b, M_HEADS, M_HEAD_DIM, M_HEAD_DIM), F32),
                   jax.ShapeDtypeStruct((b, M_HEADS, 1, M_HEAD_DIM), F32), jax.ShapeDtypeStruct((b, M_HEADS, 8, LANES), F32)],
        scratch_shapes=[pltpu.VMEM((M_HEAD_DIM, M_HEAD_DIM), F32), pltpu.VMEM((1, M_HEAD_DIM), F32), pltpu.VMEM((8, LANES), F32)],
        compiler_params=_cparams("parallel", "parallel", "arbitrary"), name="mlstm_cell",
    )(q, k, v, g_col, g_row, c0, n0r, m0b)
    return hn, cf, nf.reshape(b, M_HEADS, M_HEAD_DIM), mf[:, :, 0, 0]


def _mlstm_out_kernel(hn_ref, c_ref, z_ref, x_ref, nw_ref, sk_ref, w_ref, fw_ref, y_ref, *, final, exact):
    a = (hn_ref[...] * nw_ref[...] + sk_ref[...] * c_ref[...]) * _silu(z_ref[...])
    if exact:
        y = x_ref[...] + jnp.dot(a, w_ref[...], preferred_element_type=F32, precision=HI)
    else:
        y = x_ref[...] + jnp.dot(a.astype(BF16), w_ref[...], preferred_element_type=F32)
    y_ref[...] = _rmsnorm(y, fw_ref[...]) if final else y


def mlstm_out(hn, c, z, x, norm_w, skip, w_out, final_w, *, tm, final, exact=False):
    m, d = x.shape
    wide = pl.BlockSpec((tm, D_INNER), lambda i: (i, 0))
    row = pl.BlockSpec((tm, d), lambda i: (i, 0))
    vec = lambda n: pl.BlockSpec((1, n), lambda i: (0, 0))
    return pl.pallas_call(
        functools.partial(_mlstm_out_kernel, final=final, exact=exact), grid=(m // tm,),
        in_specs=[wide, wide, wide, row, vec(D_INNER), vec(D_INNER), pl.BlockSpec(w_out.shape, lambda i: (0, 0)), vec(d)],
        out_specs=row, out_shape=jax.ShapeDtypeStruct((m, d), F32),
        compiler_params=_cparams("parallel"), name="mlstm_out",
    )(hn, c, z, x, norm_w.reshape(1, D_INNER), skip.reshape(1, D_INNER), w_out, final_w.reshape(1, d))


def mlstm_prompt_layer(x, nw, w_in, conv_w, conv_b, w_qkv, w_gate, b_gate, norm_w, skip, w_out, final_w, *, final):
    b, t, d = x.shape
    w_in = w_in.astype(BF16)
    xm, z = rms_proj(x.reshape(b * t, d), nw, [w_in[:, :D_INNER], w_in[:, D_INNER:]], (None, None), tm=256, name="mlstm_in_proj")
    xm3 = xm.reshape(b, t, D_INNER)
    q, k, v, c, gates = mconv_prompt(xm3, conv_w, conv_b, w_qkv, w_gate, b_gate, tm=256)
    c0 = jnp.zeros((b, M_HEADS, M_HEAD_DIM, M_HEAD_DIM), F32)
    n0 = jnp.zeros((b, M_HEADS, M_HEAD_DIM), F32)
    m0 = jnp.full((b, M_HEADS), -jnp.inf, F32)
    hn, cf, nf, mf = mlstm_cell(q, k, v, gates, c0, n0, m0, chunk=min(256, t))
    y = mlstm_out(hn.reshape(b * t, D_INNER), c.reshape(b * t, D_INNER), z, x.reshape(b * t, d), norm_w, skip,
                  w_out.astype(BF16), final_w, tm=256, final=final)
    return y.reshape(b, t, d), cf, nf, mf, xm3[:, t - (CONV_W - 1):]


def _mconv_sample_kernel(xm_ref, hist_ref, cw_ref, cb_ref, wbd_ref, wg_ref, bg_ref, q_ref, k_ref, v_ref, c_ref, g_ref):
    xm = xm_ref[...]
    shifted = [hist_ref[j] for j in range(CONV_W - 1)] + [xm]
    _mconv_body(shifted, xm, cw_ref, cb_ref, wbd_ref, wg_ref, bg_ref, q_ref, k_ref, v_ref, c_ref, g_ref, True)


def mconv_sample(xm, hist, conv_w, conv_b, w_qkv, w_gate, b_gate):
    b = xm.shape[0]
    wbd, wg, bg = _mlstm_small_weights(w_qkv, w_gate, b_gate, F32)
    cb = conv_b.reshape(1, D_INNER)
    full = lambda a: pl.BlockSpec(a.shape, lambda i: (0,) * a.ndim)
    args = (xm, hist, conv_w, cb, wbd, wg, bg)
    row = pl.BlockSpec((b, D_INNER), lambda i: (0, 0))
    return pl.pallas_call(
        _mconv_sample_kernel, grid=(1,), in_specs=[full(a) for a in args],
        out_specs=[row, row, row, row, pl.BlockSpec((b, LANES), lambda i: (0, 0))],
        out_shape=[jax.ShapeDtypeStruct((b, D_INNER), F32)] * 4 + [jax.ShapeDtypeStruct((b, LANES), F32)],
        compiler_params=_cparams("arbitrary"), name="mlstm_conv_qkv_sample",
    )(*args)


def _mlstm_step_kernel(q_ref, k_ref, v_ref, g_ref, c0_ref, n0_ref, h_ref, cf_ref, nf_ref, mf_ref):
    q, k, v = q_ref[0], k_ref[0], v_ref[0]
    g = g_ref[0, 0]
    ig, fl, m0 = g[:, 0:1], g[:, 1:2], g[:, 2:3]
    c0, n0 = c0_ref[0, 0, 0], n0_ref[0, 0]
    m_new = jnp.maximum(fl + m0, ig)
    decay = jnp.exp(fl + m0 - m_new)
    sw = jnp.exp(ig - m_new)
    s = jnp.sum(q * k, axis=-1, keepdims=True) * sw
    qc = jnp.dot(jnp.broadcast_to(q, (8, M_HEAD_DIM)), c0, preferred_element_type=F32, precision=HI)[0:1]
    num = s * v + decay * qc
    den = s + decay * jnp.sum(q * n0, axis=-1, keepdims=True)
    hc = num / jnp.maximum(jnp.abs(den), jnp.exp(-m_new))
    h_ref[0] = hc * lax.rsqrt(jnp.mean(hc * hc, axis=-1, keepdims=True) + RMS_EPS)
    row0 = lax.broadcasted_iota(jnp.int32, (8, M_HEAD_DIM), 0) == 0
    k8 = jnp.where(row0, k * sw, 0.0)
    cf_ref[0, 0] = decay * c0 + lax.dot_general(k8, jnp.broadcast_to(v, (8, M_HEAD_DIM)), TN,
                                                preferred_element_type=F32, precision=HI)
    nf_ref[0, 0] = decay * n0 + sw * k
    mf_ref[0, 0] = jnp.broadcast_to(m_new, (1, LANES))


def mlstm_step(layer, q, k, v, gates, c_all, n0, m0):
    b = q.shape[0]
    gsm = jnp.stack([gates[:, :M_HEADS], gates[:, M_HEADS:2 * M_HEADS], m0], axis=-1)
    gsm = jnp.pad(gsm, ((0, 0), (0, 0), (0, LANES - 3))).reshape(b, M_HEADS, 1, LANES)
    r3 = lambda a: a.reshape(b, 1, D_INNER)
    head = pl.BlockSpec((1, 1, M_HEAD_DIM), lambda i, h: (i, 0, h))
    st = lambda *blk: pl.BlockSpec((1, 1) + blk, lambda i, h: (i, h, 0, 0))
    hn, cf, nf, mf = pl.pallas_call(
        _mlstm_step_kernel, grid=(b, M_HEADS),
        in_specs=[head, head, head, st(1, LANES),
                  pl.BlockSpec((1, 1, 1, M_HEAD_DIM, M_HEAD_DIM), lambda i, h: (layer, i, h, 0, 0)), st(1, M_HEAD_DIM)],
        out_specs=[head, st(M_HEAD_DIM, M_HEAD_DIM), st(1, M_HEAD_DIM), st(1, LANES)],
        out_shape=[jax.ShapeDtypeStruct((b, 1, D_INNER), F32), jax.ShapeDtypeStruct((b, M_HEADS, M_HEAD_DIM, M_HEAD_DIM), F32),
                   jax.ShapeDtypeStruct((b, M_HEADS, 1, M_HEAD_DIM), F32), jax.ShapeDtypeStruct((b, M_HEADS, 1, LANES), F32)],
        compiler_params=_cparams("parallel", "parallel"), name="mlstm_step",
    )(r3(q), r3(k), r3(v), gsm, c_all, n0.reshape(b, M_HEADS, 1, M_HEAD_DIM))
    return hn.reshape(b, D_INNER), cf, nf.reshape(b, M_HEADS, M_HEAD_DIM), mf[:, :, 0, 0]


def mlstm_sample_layer(layer, x, conv_state, c_all, n0, m0, nw, w_in, conv_w, conv_b, w_qkv, w_gate, b_gate, norm_w, skip,
                       w_out, final_w, *, final):
    b = x.shape[0]
    xm, z = rms_proj(x, nw, [w_in[:, :D_INNER], w_in[:, D_INNER:]], (None, None), tm=b, exact=True, name="mlstm_in_proj_sample")
    hist = conv_state.transpose(1, 0, 2)
    q, k, v, c, gates = mconv_sample(xm, hist, conv_w, conv_b, w_qkv, w_gate, b_gate)
    hn, cf, nf, mf = mlstm_step(layer, q, k, v, gates, c_all, n0, m0)
    y = mlstm_out(hn, c, z, x, norm_w, skip, w_out, final_w, tm=b, final=final, exact=True)
    conv_new = jnp.concatenate([hist[1:], xm[None]], axis=0).transpose(1, 0, 2)
    return y, cf, nf, mf, conv_new


SAMPLE_ROWS = 8
PAGES_PER_STEP = 8
N_BLK_PAD = 256


def _row_scalars(vals):
    row = lax.broadcasted_iota(jnp.int32, (SAMPLE_ROWS, 1), 0)
    col = jnp.zeros((SAMPLE_ROWS, 1), F32)
    for g, v in enumerate(vals):
        col = jnp.where(row == g, v, col)
    return col


def _nsa_sample_kernel(pt_ref, slopes_ref, *refs, gp, n_groups, past_len):
    cmp_pages, sel_pages = refs[:gp], refs[gp:2 * gp]
    (q_ref, kvn_ref, cwin_ref, wn_ref, g_ref, wp_ref, pe_ref, w1f_ref, b1_ref, w2_ref, ov_ref, o_ref,
     cmp_s, kc_s, vc_s, sel_s, oc_s, m_s, l_s, acc_s) = refs[2 * gp:]
    step = pl.program_id(1)
    t = past_len
    n_seg = past_len // CMP_STRIDE
    n_blk = past_len // SEL_BLK + 1
    span = gp * PAGE_SIZE
    slope_cols = [_row_scalars([slopes_ref[kh * GROUP + g] for g in range(GROUP)]) for kh in range(N_KV_HEADS)]

    @pl.when(step < n_groups)
    def _stash():
        for i in range(gp):
            row0 = pl.multiple_of((step * gp + i) * PAGE_SIZE, PAGE_SIZE)
            for c in range(2):
                for hp in range(N_KV_HEADS // 2):
                    a = cmp_pages[i][0, 0, c, 2 * hp:2 * hp + 2].reshape(2 * HEAD_DIM, PAGE_SIZE)
                    cmp_s[c * 2 + hp, pl.ds(row0, PAGE_SIZE), :] = a.T

    @pl.when(step == n_groups - 1)
    def _compress_and_select():
        for slab in range(4):
            c, hp = divmod(slab, 2)
            x = jnp.concatenate([cmp_s[slab, pl.ds(r, n_seg, stride=CMP_STRIDE), :] for r in range(CMP_STRIDE)], axis=1)
            part = jnp.dot(x.astype(BF16), wp_ref[c], preferred_element_type=F32)
            hidc = jnp.dot(pe_ref[c], w1f_ref[c], preferred_element_type=F32)[0:1] + b1_ref[c]
            for hh in range(2):
                p0 = part[:, hh * 2 * CMP_HID:hh * 2 * CMP_HID + CMP_HID]
                p1 = part[:, hh * 2 * CMP_HID + CMP_HID:(hh + 1) * 2 * CMP_HID]
                hid = hidc + p0 + pltpu.roll(p1, n_seg - 1, 0)
                tok = jnp.dot(_silu(hid).astype(BF16), w2_ref[c], preferred_element_type=F32)
                if c == 0:
                    kc_s[2 * hp + hh] = tok
                else:
                    vc_s[2 * hp + hh] = tok
        cend = lax.broadcasted_iota(jnp.int32, (1, n_seg), 1) * CMP_STRIDE + (CMP_BLK - 1)
        d_c = (t - cend).astype(F32)
        ok = d_c >= 0.0
        row = lax.broadcasted_iota(jnp.int32, (SAMPLE_ROWS, 1), 0)
        blk = lax.broadcasted_iota(jnp.int32, (1, N_BLK_PAD), 1)
        tb = t // SEL_BLK
        forced = (blk == 0) | (blk == tb) | (blk == tb - 1)
        ii = lax.broadcasted_iota(jnp.int32, (N_BLK_PAD, N_BLK_PAD), 0)
        jj = lax.broadcasted_iota(jnp.int32, (N_BLK_PAD, N_BLK_PAD), 1)
        for kh in range(N_KV_HEADS):
            q8 = (q_ref[0, kh] * Q_SCALE).astype(BF16)
            s = lax.dot_general(q8, kc_s[kh].astype(BF16), NT, preferred_element_type=F32) - slope_cols[kh] * d_c
            s = jnp.where(ok, s, NEG)
            m = jnp.max(s, axis=-1, keepdims=True)
            p = jnp.where(ok, jnp.exp(s - m), 0.0)
            pn = (p / jnp.maximum(jnp.sum(p, axis=-1, keepdims=True), 1e-30)).astype(BF16)
            oc_s[kh] = jnp.dot(pn, vc_s[kh].astype(BF16), preferred_element_type=F32)
            pn_heads = jnp.where(row < GROUP, pn, jnp.zeros_like(pn))
            imp = jnp.sum(jnp.dot(pn_heads, ov_ref[...], preferred_element_type=F32), axis=0, keepdims=True)
            score = jnp.where(forced, FORCE_SCORE, jnp.where(blk * SEL_BLK <= t, imp, -1.0))
            score = jnp.where(blk < n_blk, score, -2.0)
            col = jnp.broadcast_to(score, (SAMPLE_ROWS, N_BLK_PAD)).T[:, 0:1]
            beats = (col > score) | ((ii < jj) & (col == score))
            cnt = jnp.sum(jnp.where(beats, 1.0, 0.0), axis=0, keepdims=True)
            sel = jnp.where((cnt < float(min(N_SEL, n_blk))) & (score > -0.5), 1.0, 0.0)
            sel_s[kh] = jnp.broadcast_to(sel, (SAMPLE_ROWS, N_BLK_PAD))
        m_s[...] = jnp.full(m_s.shape, M_INIT, F32)
        l_s[...] = jnp.zeros(l_s.shape, F32)
        acc_s[...] = jnp.zeros(acc_s.shape, F32)

    @pl.when(step >= n_groups)
    def _selected():
        gb = step - n_groups
        pos = gb * span + lax.broadcasted_iota(jnp.int32, (1, span), 1)
        key_blk = gb * (span // SEL_BLK) + lax.broadcasted_iota(jnp.int32, (N_BLK_PAD, span), 1) // SEL_BLK
        expand = jnp.where(lax.broadcasted_iota(jnp.int32, (N_BLK_PAD, span), 0) == key_blk, 1.0, 0.0).astype(BF16)
        for kh in range(N_KV_HEADS):
            kk = jnp.concatenate([sel_pages[i][0, 0, 0, kh] for i in range(gp)], axis=1).astype(BF16)
            vv = jnp.concatenate([sel_pages[i][0, 0, 1, kh] for i in range(gp)], axis=1).astype(BF16)
            q8 = (q_ref[0, kh] * Q_SCALE).astype(BF16)
            s = jnp.dot(q8, kk, preferred_element_type=F32) + slope_cols[kh] * (pos - t).astype(F32)
            chosen = jnp.dot(sel_s[kh].astype(BF16), expand, preferred_element_type=F32)
            s = jnp.where(chosen > 0.5, s, NEG)
            m_old = m_s[kh][:, 0:1]
            m_new = jnp.maximum(m_old, jnp.max(s, axis=-1, keepdims=True))
            alpha = jnp.exp(m_old - m_new)
            p = jnp.exp(s - m_new)
            l_s[kh] = jnp.broadcast_to(alpha * l_s[kh][:, 0:1] + jnp.sum(p, axis=-1, keepdims=True), (SAMPLE_ROWS, LANES))
            acc_s[kh] = alpha * acc_s[kh] + lax.dot_general(p.astype(BF16), vv, NT, preferred_element_type=F32)
            m_s[kh] = jnp.broadcast_to(m_new, (SAMPLE_ROWS, LANES))

    @pl.when(step == 2 * n_groups - 1)
    def _finish():
        wr = cwin_ref.shape[-1]
        d_w = wr - lax.broadcasted_iota(jnp.int32, (1, wr), 1)
        ok_w = (d_w <= WINDOW) & (t - d_w >= 0)
        for kh in range(N_KV_HEADS):
            q8 = q_ref[0, kh] * Q_SCALE
            s_n = jnp.sum(q8 * kvn_ref[0, 2, kh], axis=-1, keepdims=True)
            m_old = m_s[kh][:, 0:1]
            m_new = jnp.maximum(m_old, s_n)
            alpha = jnp.exp(m_old - m_new)
            p_n = jnp.exp(s_n - m_new)
            l = alpha * l_s[kh][:, 0:1] + p_n
            o_sel = (alpha * acc_s[kh] + p_n * kvn_ref[0, 3, kh]) / jnp.maximum(l, 1e-30)
            s_w = jnp.dot(q8.astype(BF16), cwin_ref[0, 0, 0, kh].astype(BF16), preferred_element_type=F32)
            s_w = jnp.where(ok_w, s_w - slope_cols[kh] * d_w.astype(F32), NEG)
            s_wn = jnp.sum(q8 * wn_ref[0, 0, kh], axis=-1, keepdims=True)
            m_w = jnp.maximum(jnp.max(s_w, axis=-1, keepdims=True), s_wn)
            p_w = jnp.where(ok_w, jnp.exp(s_w - m_w), 0.0)
            p_wn = jnp.exp(s_wn - m_w)
            l_w = jnp.sum(p_w, axis=-1, keepdims=True) + p_wn
            o_win = (lax.dot_general(p_w.astype(BF16), cwin_ref[0, 0, 1, kh].astype(BF16), NT, preferred_element_type=F32)
                     + p_wn * wn_ref[0, 1, kh]) / jnp.maximum(l_w, 1e-30)
            gts = g_ref[0, kh]
            o_ref[0, kh] = gts[:, 0:1] * oc_s[kh] + gts[:, 1:2] * o_sel + gts[:, 2:3] * o_win


def _pair_w1(w1):
    wr = w1.reshape(2, CMP_R, CMP_STRIDE, HEAD_DIM, CMP_HID).transpose(0, 2, 3, 1, 4)
    wp = jnp.einsum("crdje,hk->crhdkje", wr, jnp.eye(2, dtype=w1.dtype))
    return wp.reshape(2, CMP_STRIDE * 2 * HEAD_DIM, 2 * CMP_R * CMP_HID).astype(BF16)


def nsa_sample_attention(layer, q, kv, win, gates, cache_kv, cache_win, page_table, pe, w1, b1, w2):
    b = q.shape[0]
    n_pages = page_table.shape[1]
    past_len = n_pages * PAGE_SIZE
    gp = PAGES_PER_STEP
    assert n_pages % gp == 0 and cache_win.shape[2] == WINDOW and past_len // SEL_BLK + 1 <= N_BLK_PAD
    n_groups = n_pages // gp
    n_seg = past_len // CMP_STRIDE
    pad_rows = lambda a: jnp.pad(a, ((0, 0), (0, 0), (0, SAMPLE_ROWS - GROUP), (0, 0)))
    q4 = pad_rows(q.reshape(b, N_KV_HEADS, GROUP, HEAD_DIM))
    g4 = pad_rows(gates[:, :3 * N_HEADS].reshape(b, N_KV_HEADS, GROUP, 3))
    kvn = kv.reshape(b, 4, N_KV_HEADS, 1, HEAD_DIM)
    wn = win.reshape(b, 2, N_KV_HEADS, 1, HEAD_DIM)
    cache_t = cache_kv.transpose(0, 1, 3, 4, 5, 2)
    cwin_t = cache_win.transpose(0, 1, 3, 4, 5, 2)
    start = np.arange(n_seg) * CMP_STRIDE
    blk = np.arange(N_BLK_PAD) * SEL_BLK
    ov = jnp.asarray((start[:, None] < blk[None, :] + SEL_BLK) & (start[:, None] + CMP_BLK > blk[None, :]), dtype=BF16)
    pe8 = jnp.broadcast_to(pe.reshape(2, 1, CMP_BLK * HEAD_DIM), (2, 8, CMP_BLK * HEAD_DIM)).astype(BF16)
    w1f = w1.reshape(2, CMP_BLK * HEAD_DIM, CMP_HID).astype(BF16)
    consts = (_pair_w1(w1), pe8, w1f, b1.reshape(2, 1, CMP_HID), w2.astype(BF16), ov)
    page_blk = (1, 1, 2, N_KV_HEADS, HEAD_DIM, PAGE_SIZE)

    def cmp_map(i):
        return lambda bi, s, pt, sl: (layer, pt[bi * n_pages + jnp.minimum(s, n_groups - 1) * gp + i], 0, 0, 0, 0)

    def sel_map(i):
        return lambda bi, s, pt, sl: (layer, pt[bi * n_pages + jnp.maximum(s - n_groups, 0) * gp + i], 1, 0, 0, 0)

    per_b = lambda a: pl.BlockSpec((1,) + a.shape[1:], lambda bi, s, pt, sl: (bi,) + (0,) * (a.ndim - 1))
    const = lambda a: pl.BlockSpec(a.shape, lambda bi, s, pt, sl: (0,) * a.ndim)
    in_specs = [pl.BlockSpec(page_blk, cmp_map(i)) for i in range(gp)] + [pl.BlockSpec(page_blk, sel_map(i)) for i in range(gp)]
    in_specs += [per_b(q4), per_b(kvn),
                 pl.BlockSpec((1, 1) + cwin_t.shape[2:], lambda bi, s, pt, sl: (layer, bi, 0, 0, 0, 0)),
                 per_b(wn), per_b(g4)] + [const(a) for a in consts]
    small = pltpu.VMEM((N_KV_HEADS, SAMPLE_ROWS, LANES), F32)
    grid_spec = pltpu.PrefetchScalarGridSpec(
        num_scalar_prefetch=2, grid=(b, 2 * n_groups), in_specs=in_specs, out_specs=per_b(q4),
        scratch_shapes=[pltpu.VMEM((4, past_len, LANES), F32),
                        pltpu.VMEM((N_KV_HEADS, n_seg, HEAD_DIM), F32), pltpu.VMEM((N_KV_HEADS, n_seg, HEAD_DIM), F32),
                        pltpu.VMEM((N_KV_HEADS, SAMPLE_ROWS, N_BLK_PAD), F32),
                        pltpu.VMEM((N_KV_HEADS, SAMPLE_ROWS, HEAD_DIM), F32), small, small,
                        pltpu.VMEM((N_KV_HEADS, SAMPLE_ROWS, HEAD_DIM), F32)])
    o = pl.pallas_call(
        functools.partial(_nsa_sample_kernel, gp=gp, n_groups=n_groups, past_len=past_len), grid_spec=grid_spec,
        out_shape=jax.ShapeDtypeStruct(q4.shape, F32),
        compiler_params=_cparams("parallel", "arbitrary"), name="nsa_sample_attn",
    )(page_table.reshape(-1), _alibi_slopes(), *([cache_t] * (2 * gp)), q4, kvn, cwin_t, wn, g4, *consts)
    return o[:, :, :GROUP].reshape(b, ATT_WIDTH)


def nsa_sample_layer(layer, x, cache_kv, cache_win, page_table, nw, w_in, w_out, pe, w1, b1, w2):
    b = x.shape[0]
    q, kv, win, gates, z = rms_proj(x, nw, _split_nsa_w_in(w_in, F32), NSA_ACTS, tm=b, exact=True, name="nsa_in_proj_sample")
    o = nsa_sample_attention(layer, q, kv, win, gates, cache_kv, cache_win, page_table, pe, w1, b1, w2)
    zero = jnp.zeros_like(o)
    y = nsa_out(o, zero, zero, z, x, w_out, tm=b, exact=True)
    win_new = jnp.concatenate([cache_win[layer][:, 1:], win.reshape(b, 1, 2, N_KV_HEADS, HEAD_DIM)], axis=1)
    return y, kv.reshape(b, 1, 4, N_KV_HEADS, HEAD_DIM), win_new


def kernel(x_prompt, x_sample, cache_kv, cache_win, state_C, state_n, state_m, state_conv, page_table, norm_w, final_norm_w,
           nsa_w_in, nsa_w_out, nsa_cmp_pe, nsa_cmp_w1, nsa_cmp_b1, nsa_cmp_w2, m_w_in, m_conv_w, m_conv_b, m_w_qkv, m_w_gate,
           m_b_gate, m_norm_w, m_skip, m_w_out):
    assert DEPTH % 2 == 0 and x_sample.shape[1] == 1
    yp, ys = x_prompt, x_sample[:, 0]
    outs = {name: [] for name in ("kv_p", "kv_s", "win_p", "win_s", "C_p", "C_s", "n_p", "n_s", "m_p", "m_s", "cv_p", "cv_s")}
    for i in range(DEPTH):
        l = i // 2
        if i % 2 == 0:
            prm = (nsa_w_in[l], nsa_w_out[l], nsa_cmp_pe[l], nsa_cmp_w1[l], nsa_cmp_b1[l], nsa_cmp_w2[l])
            yp, kvp, wp = nsa_prompt_layer(yp, norm_w[i], *prm)
            ys, kvs, wsm = nsa_sample_layer(l, ys, cache_kv, cache_win, page_table, norm_w[i], *prm)
            for name, val in (("kv_p", kvp), ("kv_s", kvs), ("win_p", wp), ("win_s", wsm)):
                outs[name].append(val)
        else:
            prm = (m_w_in[l], m_conv_w[l], m_conv_b[l], m_w_qkv[l], m_w_gate[l], m_b_gate[l], m_norm_w[l], m_skip[l], m_w_out[l])
            final = i == DEPTH - 1
            yp, cp, np_, mp, cvp = mlstm_prompt_layer(yp, norm_w[i], *prm, final_norm_w, final=final)
            ys, cs, ns, ms, cvs = mlstm_sample_layer(l, ys, state_conv[l], state_C, state_n[l], state_m[l], norm_w[i], *prm,
                                                     final_norm_w, final=final)
            for name, val in (("C_p", cp), ("C_s", cs), ("n_p", np_), ("n_s", ns), ("m_p", mp), ("m_s", ms),
                              ("cv_p", cvp), ("cv_s", cvs)):
                outs[name].append(val)
    st = {name: jnp.stack(vals) for name, vals in outs.items()}
    return (yp, ys[:, None], st["kv_p"], st["kv_s"], st["win_p"], st["win_s"], st["C_p"], st["C_s"], st["n_p"], st["n_s"],
            st["m_p"], st["m_s"], st["cv_p"], st["cv_s"])
```

```python
import functools
import math

import jax
import jax.numpy as jnp
import numpy as np
from jax import lax
from jax.experimental import pallas as pl
from jax.experimental.pallas import tpu as pltpu

F32 = jnp.float32
BF16 = jnp.bfloat16
HI = lax.Precision.HIGHEST

D_MODEL = 1024
DEPTH = 4
N_HEADS = 16
HEAD_DIM = 64
N_KV_HEADS = 4
GROUP = N_HEADS // N_KV_HEADS
ATT_WIDTH = N_HEADS * HEAD_DIM
KV_WIDTH = N_KV_HEADS * HEAD_DIM
CMP_BLK = 32
CMP_STRIDE = 16
CMP_R = CMP_BLK // CMP_STRIDE
CMP_HID = 2 * HEAD_DIM
SEL_BLK = 64
N_SEL = 16
WINDOW = 512
FORCE_SCORE = 1.0e4
D_INNER = 2 * D_MODEL
M_HEADS = 4
M_HEAD_DIM = D_INNER // M_HEADS
CONV_W = 4
QKV_BLK = 4
RMS_EPS = 1e-6
PAGE_SIZE = 128

LANES = 128
VMEM_LIMIT = 56 * 1024 * 1024
NEG = -1.0e30
M_INIT = -5.0e29
Q_SCALE = HEAD_DIM ** -0.5
NT = (((1,), (1,)), ((), ()))
TN = (((0,), (0,)), ((), ()))


def _cparams(*sem):
    return pltpu.CompilerParams(dimension_semantics=sem, vmem_limit_bytes=VMEM_LIMIT)


def _silu(x):
    return x * jax.nn.sigmoid(x)


def _alibi_slopes():
    return jnp.asarray(np.exp2(-8.0 * np.arange(1, N_HEADS + 1) / N_HEADS), dtype=F32)


def _rmsnorm(x, w):
    return x * lax.rsqrt(jnp.mean(x * x, axis=-1, keepdims=True) + RMS_EPS) * w


def _rms_proj_kernel(x_ref, nw_ref, *refs, acts, n_chunk, exact):
    n = len(acts)
    w_refs, o_refs = refs[:n], refs[n:]
    h = _rmsnorm(x_ref[...], nw_ref[...])
    if not exact:
        h = h.astype(BF16)
    for w_ref, o_ref, act in zip(w_refs, o_refs, acts):
        width = w_ref.shape[1]
        for n0 in range(0, width, n_chunk):
            n1 = min(width, n0 + n_chunk)
            if exact:
                y = jnp.dot(h, w_ref[:, n0:n1], preferred_element_type=F32, precision=HI)
            else:
                y = jnp.dot(h, w_ref[:, n0:n1], preferred_element_type=F32)
            if act == "sigmoid":
                y = jax.nn.sigmoid(y)
            o_ref[:, n0:n1] = y


def rms_proj(x, nw, weights, acts, *, tm, exact=False, name="rms_proj"):
    m, k = x.shape
    assert m % tm == 0
    in_specs = [pl.BlockSpec((tm, k), lambda i: (i, 0)), pl.BlockSpec((1, k), lambda i: (0, 0))]
    in_specs += [pl.BlockSpec(w.shape, lambda i: (0, 0)) for w in weights]
    out_specs = [pl.BlockSpec((tm, w.shape[1]), lambda i: (i, 0)) for w in weights]
    out_shape = [jax.ShapeDtypeStruct((m, w.shape[1]), F32) for w in weights]
    return pl.pallas_call(
        functools.partial(_rms_proj_kernel, acts=tuple(acts), n_chunk=512, exact=exact),
        grid=(m // tm,), in_specs=in_specs, out_specs=out_specs, out_shape=out_shape,
        compiler_params=_cparams("parallel"), name=name,
    )(x, nw.reshape(1, k), *weights)


def _compress_kernel(x_ref, w1_ref, pe_ref, w1f_ref, b1_ref, w2_ref, o_ref, *, n_seg):
    x = x_ref[0, 0, 0]
    p0 = jnp.dot(x, w1_ref[0, 0], preferred_element_type=F32)
    p1 = jnp.dot(x, w1_ref[0, 1], preferred_element_type=F32)
    hidc = jnp.dot(pe_ref[0], w1f_ref[0], preferred_element_type=F32)[0:1] + b1_ref[0]
    hid = hidc + p0 + pltpu.roll(p1, n_seg - 1, 0)
    o_ref[0, 0, 0] = jnp.dot(_silu(hid).astype(BF16), w2_ref[0], preferred_element_type=F32)


def compress_tokens(x, pe, w1, b1, w2):
    b, _, hkv, n_seg, _ = x.shape
    w1r = w1.reshape(2, CMP_R, CMP_STRIDE * HEAD_DIM, CMP_HID).astype(BF16)
    w1f = w1.reshape(2, CMP_BLK * HEAD_DIM, CMP_HID).astype(BF16)
    pe8 = jnp.broadcast_to(pe.reshape(2, 1, CMP_BLK * HEAD_DIM), (2, 8, CMP_BLK * HEAD_DIM)).astype(BF16)
    return pl.pallas_call(
        functools.partial(_compress_kernel, n_seg=n_seg),
        grid=(b, 2, hkv),
        in_specs=[
            pl.BlockSpec((1, 1, 1, n_seg, CMP_STRIDE * HEAD_DIM), lambda i, c, h: (i, c, h, 0, 0)),
            pl.BlockSpec((1, CMP_R, CMP_STRIDE * HEAD_DIM, CMP_HID), lambda i, c, h: (c, 0, 0, 0)),
            pl.BlockSpec((1, 8, CMP_BLK * HEAD_DIM), lambda i, c, h: (c, 0, 0)),
            pl.BlockSpec((1, CMP_BLK * HEAD_DIM, CMP_HID), lambda i, c, h: (c, 0, 0)),
            pl.BlockSpec((1, 1, CMP_HID), lambda i, c, h: (c, 0, 0)),
            pl.BlockSpec((1, CMP_HID, HEAD_DIM), lambda i, c, h: (c, 0, 0)),
        ],
        out_specs=pl.BlockSpec((1, 1, 1, n_seg, HEAD_DIM), lambda i, c, h: (i, c, h, 0, 0)),
        out_shape=jax.ShapeDtypeStruct((b, 2, hkv, n_seg, HEAD_DIM), F32),
        compiler_params=_cparams("parallel", "parallel", "parallel"), name="nsa_compress",
    )(x, w1r, pe8, w1f, b1.reshape(2, 1, CMP_HID), w2.astype(BF16))


def _topk_mask_t(score_t, n_valid, k):
    r = score_t.shape[0]
    jidx = lax.broadcasted_iota(jnp.int32, (r, 1), 0)
    cnt = jnp.zeros(score_t.shape, F32)
    for i in range(n_valid):
        row = score_t[i:i + 1, :]
        beats = (row > score_t) | ((jidx > i) & (row == score_t))
        cnt = cnt + jnp.where(beats, 1.0, 0.0)
    return cnt < float(k)


def _cmp_sel_kernel(slopes_ref, q_ref, kc_ref, vc_ref, g_ref, ov_ref, oc_ref, sel_ref, *, tq, n_seg, n_blk):
    kh = pl.program_id(1)
    t0 = pl.program_id(2) * tq
    t_col = t0 + lax.broadcasted_iota(jnp.int32, (tq, 1), 0)
    cend = lax.broadcasted_iota(jnp.int32, (1, n_seg), 1) * CMP_STRIDE + (CMP_BLK - 1)
    d_c = (t_col - cend).astype(F32)
    ok = d_c >= 0.0
    kc = kc_ref[0, 0, 0].astype(BF16)
    vc = vc_ref[0, 0, 0].astype(BF16)
    imp = jnp.zeros((tq, LANES), F32)
    outs = []
    for g in range(GROUP):
        qg = (q_ref[0, :, g * HEAD_DIM:(g + 1) * HEAD_DIM] * Q_SCALE).astype(BF16)
        s = lax.dot_general(qg, kc, NT, preferred_element_type=F32) - slopes_ref[kh * GROUP + g] * d_c
        s = jnp.where(ok, s, NEG)
        m = jnp.max(s, axis=-1, keepdims=True)
        p = jnp.where(ok, jnp.exp(s - m), 0.0)
        pn = (p / jnp.maximum(jnp.sum(p, axis=-1, keepdims=True), 1e-30)).astype(BF16)
        o = jnp.dot(pn, vc, preferred_element_type=F32)
        outs.append(o * g_ref[0, 0, :, 3 * g:3 * g + 1])
        imp = imp + jnp.dot(pn, ov_ref[...], preferred_element_type=F32)
    oc_ref[0] = jnp.concatenate(outs, axis=1)
    blk = lax.broadcasted_iota(jnp.int32, (1, LANES), 1)
    tb = lax.shift_right_logical(t_col, int(math.log2(SEL_BLK)))
    forced = (blk == 0) | (blk == tb) | (blk == tb - 1)
    score = jnp.where(forced, FORCE_SCORE, jnp.where(blk * SEL_BLK <= t_col, imp, -1.0))
    score = jnp.where(blk < n_blk, score, -2.0)
    score_t = score.T
    sel_t = _topk_mask_t(score_t, n_blk, min(N_SEL, n_blk)) & (score_t > -0.5)
    sel_ref[0, 0] = jnp.where(sel_t, 1.0, 0.0).T


def cmp_select(q, kvc, gates_r, slopes, ov, *, tq):
    b, t, _ = q.shape
    n_seg = kvc.shape[3]
    n_blk = t // SEL_BLK
    grid_spec = pltpu.PrefetchScalarGridSpec(
        num_scalar_prefetch=1, grid=(b, N_KV_HEADS, t // tq),
        in_specs=[
            pl.BlockSpec((1, tq, GROUP * HEAD_DIM), lambda i, k, j, s: (i, j, k)),
            pl.BlockSpec((1, 1, 1, n_seg, HEAD_DIM), lambda i, k, j, s: (i, 0, k, 0, 0)),
            pl.BlockSpec((1, 1, 1, n_seg, HEAD_DIM), lambda i, k, j, s: (i, 1, k, 0, 0)),
            pl.BlockSpec((1, 1, tq, 3 * GROUP), lambda i, k, j, s: (i, k, j, 0)),
            pl.BlockSpec((n_seg, LANES), lambda i, k, j, s: (0, 0)),
        ],
        out_specs=[
            pl.BlockSpec((1, tq, GROUP * HEAD_DIM), lambda i, k, j, s: (i, j, k)),
            pl.BlockSpec((1, 1, tq, LANES), lambda i, k, j, s: (i, k, j, 0)),
        ],
    )
    return pl.pallas_call(
        functools.partial(_cmp_sel_kernel, tq=tq, n_seg=n_seg, n_blk=n_blk),
        grid_spec=grid_spec,
        out_shape=[jax.ShapeDtypeStruct((b, t, ATT_WIDTH), F32), jax.ShapeDtypeStruct((b, N_KV_HEADS, t, LANES), F32)],
        compiler_params=_cparams("parallel", "parallel", "parallel"), name="nsa_cmp_select",
    )(slopes, q, kvc, kvc, gates_r, ov)


def _cmp_to_sel_matrix(n_rows, n_blk):
    start = np.arange(n_rows) * CMP_STRIDE
    blk = np.arange(LANES) * SEL_BLK
    ov = (start[:, None] < blk[None, :] + SEL_BLK) & (start[:, None] + CMP_BLK > blk[None, :]) & (np.arange(LANES)[None, :] < n_blk)
    return jnp.asarray(ov, dtype=BF16)


def _stack_heads(q_blk, extra=None):
    parts = []
    for g in range(GROUP):
        qg = q_blk[:, g * HEAD_DIM:(g + 1) * HEAD_DIM] * Q_SCALE
        if extra is not None:
            qg = jnp.concatenate([qg, extra], axis=1)
        parts.append(qg)
    return jnp.concatenate(parts, axis=0).astype(BF16)


def _head_cols(fn, tq):
    return jnp.concatenate([jnp.full((tq, 1), fn(g), F32) for g in range(GROUP)], axis=0)


def _sel_attn_kernel(slopes_ref, q_ref, sel_ref, ka_ref, v_ref, g_ref, o_ref, *, tq, tk):
    kh = pl.program_id(1)
    t0 = pl.program_id(2) * tq
    mask_feat = (sel_ref[0, 0][:, 0:HEAD_DIM] - 1.0) * 1.0e30
    qa = _stack_heads(q_ref[0], mask_feat)
    slope_col = _head_cols(lambda g: slopes_ref[kh * GROUP + g], tq)
    t_col = t0 + lax.broadcasted_iota(jnp.int32, (tq, 1), 0)
    t_col4 = jnp.concatenate([t_col] * GROUP, axis=0)

    def body(kt, carry):
        m, l, acc = carry
        ks = pl.multiple_of(kt * tk, tk)
        s = lax.dot_general(qa, ka_ref[0, 0, pl.ds(ks, tk), :], NT, preferred_element_type=F32)
        pos = ks + lax.broadcasted_iota(jnp.int32, (1, tk), 1)
        s = s + slope_col * (pos - t0).astype(F32)
        s = jnp.where(pos <= t_col4, s, NEG)
        m_new = jnp.maximum(m, jnp.max(s, axis=-1, keepdims=True))
        alpha = jnp.exp(m - m_new)
        p = jnp.exp(s - m_new)
        l = alpha * l + jnp.sum(p, axis=-1, keepdims=True)
        acc = alpha * acc + jnp.dot(p.astype(BF16), v_ref[0, 0, pl.ds(ks, tk), :], preferred_element_type=F32)
        return m_new, l, acc

    n_kt = (t0 + tq + tk - 1) // tk
    init = (jnp.full((GROUP * tq, 1), M_INIT, F32), jnp.zeros((GROUP * tq, 1), F32), jnp.zeros((GROUP * tq, HEAD_DIM), F32))
    _, l, acc = lax.fori_loop(0, n_kt, body, init)
    o = acc / jnp.maximum(l, 1e-30)
    o_ref[0] = jnp.concatenate(
        [o[g * tq:(g + 1) * tq] * g_ref[0, 0, :, 3 * g + 1:3 * g + 2] for g in range(GROUP)], axis=1)


def sel_attention(q, sel, k_aug, v_sel, gates_r, slopes, *, tq, tk):
    b, t, _ = q.shape
    grid_spec = pltpu.PrefetchScalarGridSpec(
        num_scalar_prefetch=1, grid=(b, N_KV_HEADS, t // tq),
        in_specs=[
            pl.BlockSpec((1, tq, GROUP * HEAD_DIM), lambda i, k, j, s: (i, j, k)),
            pl.BlockSpec((1, 1, tq, LANES), lambda i, k, j, s: (i, k, j, 0)),
            pl.BlockSpec((1, 1, t, LANES), lambda i, k, j, s: (i, k, 0, 0)),
            pl.BlockSpec((1, 1, t, HEAD_DIM), lambda i, k, j, s: (i, k, 0, 0)),
            pl.BlockSpec((1, 1, tq, 3 * GROUP), lambda i, k, j, s: (i, k, j, 0)),
        ],
        out_specs=pl.BlockSpec((1, tq, GROUP * HEAD_DIM), lambda i, k, j, s: (i, j, k)),
    )
    return pl.pallas_call(
        functools.partial(_sel_attn_kernel, tq=tq, tk=tk), grid_spec=grid_spec,
        out_shape=jax.ShapeDtypeStruct((b, t, ATT_WIDTH), F32),
        compiler_params=_cparams("parallel", "parallel", "arbitrary"), name="nsa_sel_attn",
    )(slopes, q, sel, k_aug, v_sel, gates_r)


def _win_attn_kernel(slopes_ref, q_ref, k_ref, v_ref, g_ref, o_ref, *, tq, span):
    kh = pl.program_id(1)
    t0 = pl.program_id(2) * tq
    qs = _stack_heads(q_ref[0])
    slope_col = _head_cols(lambda g: slopes_ref[kh * GROUP + g], tq)
    t_col = t0 + lax.broadcasted_iota(jnp.int32, (tq, 1), 0)
    t_col4 = jnp.concatenate([t_col] * GROUP, axis=0)
    start = pl.multiple_of(jnp.maximum(t0 + tq - span, 0), tq)
    s = lax.dot_general(qs, k_ref[0, 0, pl.ds(start, span), :], NT, preferred_element_type=F32)
    d = t_col4 - (start + lax.broadcasted_iota(jnp.int32, (1, span), 1))
    ok = (d >= 0) & (d <= WINDOW)
    s = jnp.where(ok, s - slope_col * d.astype(F32), NEG)
    m = jnp.max(s, axis=-1, keepdims=True)
    p = jnp.where(ok, jnp.exp(s - m), 0.0)
    l = jnp.sum(p, axis=-1, keepdims=True)
    o = jnp.dot(p.astype(BF16), v_ref[0, 0, pl.ds(start, span), :], preferred_element_type=F32) / jnp.maximum(l, 1e-30)
    o_ref[0] = jnp.concatenate(
        [o[g * tq:(g + 1) * tq] * g_ref[0, 0, :, 3 * g + 2:3 * g + 3] for g in range(GROUP)], axis=1)


def win_attention(q, k_win, v_win, gates_r, slopes, *, tq):
    b, t, _ = q.shape
    span = WINDOW + tq
    assert t >= span and span % tq == 0
    grid_spec = pltpu.PrefetchScalarGridSpec(
        num_scalar_prefetch=1, grid=(b, N_KV_HEADS, t // tq),
        in_specs=[
            pl.BlockSpec((1, tq, GROUP * HEAD_DIM), lambda i, k, j, s: (i, j, k)),
            pl.BlockSpec((1, 1, t, HEAD_DIM), lambda i, k, j, s: (i, k, 0, 0)),
            pl.BlockSpec((1, 1, t, HEAD_DIM), lambda i, k, j, s: (i, k, 0, 0)),
            pl.BlockSpec((1, 1, tq, 3 * GROUP), lambda i, k, j, s: (i, k, j, 0)),
        ],
        out_specs=pl.BlockSpec((1, tq, GROUP * HEAD_DIM), lambda i, k, j, s: (i, j, k)),
    )
    return pl.pallas_call(
        functools.partial(_win_attn_kernel, tq=tq, span=span), grid_spec=grid_spec,
        out_shape=jax.ShapeDtypeStruct((b, t, ATT_WIDTH), F32),
        compiler_params=_cparams("parallel", "parallel", "arbitrary"), name="nsa_win_attn",
    )(slopes, q, k_win, v_win, gates_r)


def _nsa_out_kernel(oc_ref, os_ref, ow_ref, z_ref, x_ref, w_ref, y_ref, *, exact):
    a = (oc_ref[...] + os_ref[...] + ow_ref[...]) * _silu(z_ref[...])
    if exact:
        y_ref[...] = x_ref[...] + jnp.dot(a, w_ref[...], preferred_element_type=F32, precision=HI)
    else:
        y_ref[...] = x_ref[...] + jnp.dot(a.astype(BF16), w_ref[...], preferred_element_type=F32)


def nsa_out(o_c, o_s, o_w, z, x, w_out, *, tm, exact=False):
    m, d = x.shape
    row = pl.BlockSpec((tm, d), lambda i: (i, 0))
    return pl.pallas_call(
        functools.partial(_nsa_out_kernel, exact=exact), grid=(m // tm,),
        in_specs=[row, row, row, row, row, pl.BlockSpec(w_out.shape, lambda i: (0, 0))],
        out_specs=row, out_shape=jax.ShapeDtypeStruct((m, d), F32),
        compiler_params=_cparams("parallel"), name="nsa_out",
    )(o_c, o_s, o_w, z, x, w_out)


LOG2E = math.log2(math.e)
Q_SCALE2 = Q_SCALE * LOG2E
ALIBI_ROWS = 16
BIG = 1.0e30
ONES_ROWS = 16
TQ = 128
TK = 512
WCH = 128
N_CHAINS = 1


def _alibi_table():
    s = _alibi_slopes() * LOG2E
    s1 = s.astype(BF16).astype(F32)
    s2 = (s - s1).astype(BF16).astype(F32)
    s3 = (s - s1 - s2).astype(BF16).astype(F32)
    tab = jnp.stack([SEL_BLK * s1, SEL_BLK * s2, SEL_BLK * s3, s1, s2, s3, jnp.full_like(s, -BIG), jnp.zeros_like(s)], axis=1)
    return tab.reshape(-1)


def _pos_features(pos, valid, width):
    lane = lax.broadcasted_iota(jnp.int32, (pos.shape[0], width), 1)
    blk = lax.shift_right_logical(pos, int(math.log2(SEL_BLK))).astype(F32)
    rem = (pos & (SEL_BLK - 1)).astype(F32)
    f = jnp.where(lane < 3, blk, jnp.where(lane < 6, rem, 0.0))
    return jnp.where(lane == 6, jnp.where(valid, 0.0, 1.0), f)


def _nsa_in_proj_kernel(x_ref, nw_ref, wt_ref, wn_ref, qt_ref, kvt_ref, wint_ref, gt_ref, zt_ref, kn_ref, kwn_ref):
    h = _rmsnorm(x_ref[0], nw_ref[...]).astype(BF16)

    def nt(r0, r1):
        return lax.dot_general(wt_ref[r0:r1, :], h, NT, preferred_element_type=F32)

    o1, o2, o3 = ATT_WIDTH, ATT_WIDTH + 4 * KV_WIDTH, ATT_WIDTH + 6 * KV_WIDTH
    o4 = o3 + LANES
    for r0 in range(0, o1, 512):
        qt_ref[0, r0:r0 + 512, :] = (nt(r0, r0 + 512) * Q_SCALE2).astype(BF16)
    for r0 in range(o1, o2, 512):
        kvt_ref[0, r0 - o1:r0 - o1 + 512, :] = nt(r0, r0 + 512)
    wint_ref[0] = nt(o2, o3)
    gt_ref[0] = jax.nn.sigmoid(nt(o3, o4))
    for r0 in range(o4, o4 + ATT_WIDTH, 512):
        zt_ref[0, r0 - o4:r0 - o4 + 512, :] = nt(r0, r0 + 512)
    yn = jnp.dot(h, wn_ref[...], preferred_element_type=F32)
    for j in range(3):
        for hh in range(N_KV_HEADS):
            c0 = (j * N_KV_HEADS + hh) * HEAD_DIM
            kn_ref[0, j, hh] = yn[:, c0:c0 + HEAD_DIM].astype(BF16)
    for hh in range(N_KV_HEADS):
        c0 = (3 * N_KV_HEADS + hh) * HEAD_DIM
        kwn_ref[0, hh] = yn[:, c0:c0 + HEAD_DIM].astype(BF16)


def nsa_in_proj(x, nw, w_in, *, tm):
    b, t, d = x.shape
    o1, o2, o3 = ATT_WIDTH, ATT_WIDTH + 4 * KV_WIDTH, ATT_WIDTH + 6 * KV_WIDTH
    o4 = o3 + 3 * N_HEADS
    w_t = w_in.T
    wt = jnp.concatenate([w_t[:o4], jnp.zeros((LANES - 3 * N_HEADS, d), w_in.dtype), w_t[o4:]], axis=0).astype(BF16)
    wn = jnp.concatenate([w_in[:, o1:o1 + 3 * KV_WIDTH], w_in[:, o2:o2 + KV_WIDTH]], axis=1).astype(BF16)
    tok = lambda rows: pl.BlockSpec((1, rows, tm), lambda i, j: (i, 0, j))
    full = lambda a: pl.BlockSpec(a.shape, lambda i, j: (0,) * a.ndim)
    tshape = lambda rows, dt: jax.ShapeDtypeStruct((b, rows, t), dt)
    return pl.pallas_call(
        _nsa_in_proj_kernel, grid=(b, t // tm),
        in_specs=[pl.BlockSpec((1, tm, d), lambda i, j: (i, j, 0)), pl.BlockSpec((1, d), lambda i, j: (0, 0)), full(wt), full(wn)],
        out_specs=[tok(ATT_WIDTH), tok(4 * KV_WIDTH), tok(2 * KV_WIDTH), tok(LANES), tok(ATT_WIDTH),
                   pl.BlockSpec((1, 3, N_KV_HEADS, tm, HEAD_DIM), lambda i, j: (i, 0, 0, j, 0)),
                   pl.BlockSpec((1, N_KV_HEADS, tm, HEAD_DIM), lambda i, j: (i, 0, j, 0))],
        out_shape=[tshape(ATT_WIDTH, BF16), tshape(4 * KV_WIDTH, F32), tshape(2 * KV_WIDTH, F32), tshape(LANES, F32),
                   tshape(ATT_WIDTH, F32), jax.ShapeDtypeStruct((b, 3, N_KV_HEADS, t, HEAD_DIM), BF16),
                   jax.ShapeDtypeStruct((b, N_KV_HEADS, t, HEAD_DIM), BF16)],
        compiler_params=_cparams("parallel", "parallel"), name="nsa_in_proj",
    )(x, nw.reshape(1, d), wt, wn)


def _topk_mask_rows(score, k):
    r = score.shape[0]
    groups = [score[8 * v:8 * v + 8] for v in range(r // 8)]
    sub = lax.broadcasted_iota(jnp.int32, (8, 1), 0)
    cnt = [jnp.zeros(g.shape, F32) for g in groups]
    for i in range(r):
        row = score[i:i + 1, :]
        for v, g in enumerate(groups):
            if 8 * v > i:
                beats = row >= g
            elif 8 * v + 7 < i:
                beats = row > g
            else:
                beats = (row > g) | ((sub > i - 8 * v) & (row == g))
            cnt[v] = cnt[v] + jnp.where(beats, 1.0, 0.0)
    return jnp.concatenate(cnt, axis=0) < float(k)


def _nsa_attn_kernel(tab_ref, qt_ref, gt_ref, kc_ref, vc_ref, ksel_ref, kwin_ref, vselt_ref, vwint_ref, ovt_ref, ot_ref,
                     ksa_s, kwa_s, kca_s, vst_s, vwt_s, vct_s, *, t_len, n_seg, n_blk):
    kh = pl.program_id(1)
    qi = pl.program_id(2)
    t0 = qi * TQ
    cols = GROUP * TQ
    ones = lambda n: jnp.ones((ONES_ROWS, n), BF16)

    @pl.when(qi == 0)
    def _build_keys():
        pos = lax.broadcasted_iota(jnp.int32, (t_len, 1), 0)
        onehot = jnp.where(lax.shift_right_logical(pos, int(math.log2(SEL_BLK)))
                           == lax.broadcasted_iota(jnp.int32, (1, HEAD_DIM), 1), 1.0, 0.0)
        ksa_s[:, 0:LANES] = jnp.concatenate([ksel_ref[0, 0, 0].astype(F32), onehot], axis=1).astype(BF16)
        ksa_s[:, LANES:2 * LANES] = _pos_features(pos, pos >= 0, LANES).astype(BF16)
        no_key = _pos_features(jnp.zeros((WINDOW, 1), jnp.int32), jnp.zeros((WINDOW, 1), jnp.bool_), HEAD_DIM)
        kwa_s[0:WINDOW, :] = jnp.concatenate([jnp.zeros((WINDOW, HEAD_DIM), F32), no_key], axis=1).astype(BF16)
        kwa_s[WINDOW:WINDOW + t_len, :] = jnp.concatenate(
            [kwin_ref[0, 0].astype(F32), _pos_features(pos, pos >= 0, HEAD_DIM)], axis=1).astype(BF16)
        cend = lax.broadcasted_iota(jnp.int32, (n_seg, 1), 0) * CMP_STRIDE + (CMP_BLK - 1)
        kca_s[...] = jnp.concatenate([kc_ref[0, 0, 0], _pos_features(cend, cend >= 0, HEAD_DIM)], axis=1).astype(BF16)
        vc_pad = jnp.concatenate([vc_ref[0, 0, 0], jnp.zeros((n_seg, LANES - HEAD_DIM), F32)], axis=1)
        vct_s[...] = jnp.concatenate([vc_pad.T[0:HEAD_DIM].astype(BF16), ones(n_seg)], axis=0)
        for c in range(t_len // TK):
            vst_s[c] = jnp.concatenate([vselt_ref[0, :, c * TK:(c + 1) * TK].astype(BF16), ones(TK)], axis=0)
        for c in range(WINDOW // WCH):
            vwt_s[c] = jnp.zeros((HEAD_DIM + ONES_ROWS, WCH), BF16)
        for c in range(t_len // WCH):
            vwt_s[WINDOW // WCH + c] = jnp.concatenate([vwint_ref[0, :, c * WCH:(c + 1) * WCH].astype(BF16), ones(WCH)], axis=0)

    qb = qt_ref[0]
    qw = jnp.concatenate([qb[g * HEAD_DIM:(g + 1) * HEAD_DIM, :] for g in range(GROUP)], axis=1)
    frow = lax.broadcasted_iota(jnp.int32, (ALIBI_ROWS, cols), 0)
    fhead = lax.broadcasted_iota(jnp.int32, (ALIBI_ROWS, cols), 1) // TQ
    feat = jnp.zeros((ALIBI_ROWS, cols), F32)
    for g in range(GROUP):
        for r in range(7):
            feat = jnp.where((frow == r) & (fhead == g), tab_ref[(kh * GROUP + g) * 8 + r], feat)
    feat = feat.astype(BF16)
    q_base = jnp.concatenate([qw, feat, jnp.zeros((LANES - HEAD_DIM - ALIBI_ROWS, cols), BF16)], axis=0)
    t_row = t0 + lax.broadcasted_iota(jnp.int32, (1, cols), 1) % TQ

    def finish(acc):
        return acc[0:HEAD_DIM] * (1.0 / jnp.maximum(acc[HEAD_DIM:HEAD_DIM + 1], 1e-30))

    hcols = cols // N_CHAINS
    chains = [slice(c * hcols, (c + 1) * hcols) for c in range(N_CHAINS)]
    reps = hcols // TQ

    cend = lax.broadcasted_iota(jnp.int32, (n_seg, 1), 0) * CMP_STRIDE + (CMP_BLK - 1)
    o_cmp, imp = [], jnp.zeros((HEAD_DIM, TQ), F32)
    for ch in chains:
        ok_c = cend <= t_row[:, ch]
        s_c = jnp.where(ok_c, jnp.dot(kca_s[...], q_base[:, ch], preferred_element_type=F32), NEG)
        m_c = jnp.max(s_c, axis=0, keepdims=True)
        p_c = jnp.where(ok_c, jnp.exp2(s_c - m_c), 0.0).astype(BF16)
        acc_c = jnp.dot(vct_s[...], p_c, preferred_element_type=F32)
        inv_c = 1.0 / jnp.maximum(acc_c[HEAD_DIM:HEAD_DIM + 1], 1e-30)
        o_cmp.append(acc_c[0:HEAD_DIM] * inv_c)
        imp_c = jnp.dot(ovt_ref[...], p_c, preferred_element_type=F32) * inv_c
        for g in range(reps):
            imp = imp + imp_c[:, g * TQ:(g + 1) * TQ]
    blk = lax.broadcasted_iota(jnp.int32, (HEAD_DIM, 1), 0)
    tq_row = t_row[:, 0:TQ]
    tb = lax.shift_right_logical(tq_row, int(math.log2(SEL_BLK)))
    forced = (blk == 0) | (blk == tb) | (blk == tb - 1)
    score = jnp.where(forced, FORCE_SCORE, jnp.where(blk * SEL_BLK <= tq_row, imp, -1.0))
    score = jnp.where(blk < n_blk, score, -2.0)
    chosen = _topk_mask_rows(score, min(N_SEL, n_blk)) & (score > -0.5)

    span = WINDOW + TQ
    k_w = kwa_s[pl.ds(pl.multiple_of(t0, TQ), span), :]
    v_w = jnp.concatenate([vwt_s[qi * (TQ // WCH) + c] for c in range(span // WCH)], axis=1)
    jj = lax.broadcasted_iota(jnp.int32, (TQ, 1), 0)
    ii = lax.broadcasted_iota(jnp.int32, (1, hcols), 1) % TQ
    o_win = []
    for ch in chains:
        s_w = jnp.dot(k_w, q_base[:, ch], preferred_element_type=F32)
        s_w = jnp.concatenate([jnp.where(jj >= ii, s_w[0:TQ], NEG), s_w[TQ:WINDOW],
                               jnp.where(jj <= ii, s_w[WINDOW:span], NEG)], axis=0)
        m_w = jnp.maximum(jnp.max(s_w, axis=0, keepdims=True), M_INIT)
        p_w = jnp.exp2((s_w - m_w).astype(BF16))
        o_win.append(finish(jnp.dot(v_w, p_w, preferred_element_type=F32)))

    mrow = jnp.concatenate([jnp.where(chosen, 0.0, -BIG).astype(BF16)] * reps, axis=1)
    pad = jnp.zeros((LANES - ALIBI_ROWS, hcols), BF16)
    q_sel = [jnp.concatenate([qw[:, ch], mrow, feat[:, ch], pad], axis=0) for ch in chains]
    n_pairs = (t0 + TQ - 1) // (2 * TK) + 1
    key_off = lax.broadcasted_iota(jnp.int32, (TK, 1), 0)

    def sel_step(j, carry):
        out = []
        for c, ch in enumerate(chains):
            m, acc = carry[2 * c], carry[2 * c + 1]
            tiles = []
            for u in range(2):
                ks = pl.multiple_of((2 * j + u) * TK, TK)
                s = jnp.dot(ksa_s[pl.ds(ks, TK), :], q_sel[c], preferred_element_type=F32)
                tiles.append(jnp.where(ks + key_off <= t_row[:, ch], s, NEG))
            m_new = jnp.maximum(m, jnp.maximum(jnp.max(tiles[0], axis=0, keepdims=True), jnp.max(tiles[1], axis=0, keepdims=True)))
            acc = jnp.exp2(m - m_new) * acc
            for u in range(2):
                p = jnp.exp2((tiles[u] - m_new).astype(BF16))
                acc = acc + jnp.dot(vst_s[2 * j + u], p, preferred_element_type=F32)
            out += [m_new, acc]
        return tuple(out)

    carry = []
    for c in range(N_CHAINS):
        carry += [jnp.full((1, hcols), M_INIT, F32), jnp.zeros((HEAD_DIM + ONES_ROWS, hcols), F32)]
    carry = lax.fori_loop(0, n_pairs, sel_step, tuple(carry))
    o_sel = [finish(carry[2 * c + 1]) for c in range(N_CHAINS)]

    for g in range(GROUP):
        c, sl = g // reps, slice((g % reps) * TQ, (g % reps + 1) * TQ)
        gate = lambda j: gt_ref[0, pl.ds(kh * 3 * GROUP + 3 * g + j, 1), :]
        ot_ref[0, g * HEAD_DIM:(g + 1) * HEAD_DIM, :] = (gate(0) * o_cmp[c][:, sl] + gate(1) * o_sel[c][:, sl]
                                                         + gate(2) * o_win[c][:, sl])


def nsa_attention(qt, gt, kvc, kn, kwn, kvt, wint):
    b, _, t = qt.shape
    n_seg = kvc.shape[3]
    n_blk = t // SEL_BLK
    assert t % (2 * TK) == 0 and n_blk <= HEAD_DIM and t >= WINDOW + TQ
    start = np.arange(n_seg) * CMP_STRIDE
    blk = np.arange(HEAD_DIM) * SEL_BLK
    ovt = jnp.asarray(((start[None, :] < blk[:, None] + SEL_BLK) & (start[None, :] + CMP_BLK > blk[:, None])
                       & (np.arange(HEAD_DIM)[:, None] < n_blk)), dtype=BF16)
    vrows = HEAD_DIM + ONES_ROWS
    grid_spec = pltpu.PrefetchScalarGridSpec(
        num_scalar_prefetch=1, grid=(b, N_KV_HEADS, t // TQ),
        in_specs=[
            pl.BlockSpec((1, GROUP * HEAD_DIM, TQ), lambda i, k, j, s: (i, k, j)),
            pl.BlockSpec((1, LANES, TQ), lambda i, k, j, s: (i, 0, j)),
            pl.BlockSpec((1, 1, 1, n_seg, HEAD_DIM), lambda i, k, j, s: (i, 0, k, 0, 0)),
            pl.BlockSpec((1, 1, 1, n_seg, HEAD_DIM), lambda i, k, j, s: (i, 1, k, 0, 0)),
            pl.BlockSpec((1, 1, 1, t, HEAD_DIM), lambda i, k, j, s: (i, 2, k, 0, 0)),
            pl.BlockSpec((1, 1, t, HEAD_DIM), lambda i, k, j, s: (i, k, 0, 0)),
            pl.BlockSpec((1, HEAD_DIM, t), lambda i, k, j, s: (i, 3 * N_KV_HEADS + k, 0)),
            pl.BlockSpec((1, HEAD_DIM, t), lambda i, k, j, s: (i, N_KV_HEADS + k, 0)),
            pl.BlockSpec((HEAD_DIM, n_seg), lambda i, k, j, s: (0, 0)),
        ],
        out_specs=pl.BlockSpec((1, GROUP * HEAD_DIM, TQ), lambda i, k, j, s: (i, k, j)),
        scratch_shapes=[pltpu.VMEM((t, 2 * LANES), BF16), pltpu.VMEM((WINDOW + t, LANES), BF16), pltpu.VMEM((n_seg, LANES), BF16),
                        pltpu.VMEM((t // TK, vrows, TK), BF16), pltpu.VMEM(((WINDOW + t) // WCH, vrows, WCH), BF16),
                        pltpu.VMEM((vrows, n_seg), BF16)])
    return pl.pallas_call(
        functools.partial(_nsa_attn_kernel, t_len=t, n_seg=n_seg, n_blk=n_blk), grid_spec=grid_spec,
        out_shape=jax.ShapeDtypeStruct((b, ATT_WIDTH, t), F32),
        compiler_params=_cparams("parallel", "parallel", "arbitrary"), name="nsa_attn",
    )(_alibi_table(), qt, gt, kvc, kvc, kn, kwn, kvt, wint, ovt)


def _nsa_out_t_kernel(ot_ref, zt_ref, x_ref, w_ref, y_ref):
    a_t = (ot_ref[0] * _silu(zt_ref[0])).astype(BF16)
    y_ref[0] = x_ref[0] + lax.dot_general(a_t, w_ref[...], TN, preferred_element_type=F32)


def nsa_out_t(ot, zt, x, w_out, *, tm):
    b, t, d = x.shape
    tok = pl.BlockSpec((1, ATT_WIDTH, tm), lambda i, j: (i, 0, j))
    row = pl.BlockSpec((1, tm, d), lambda i, j: (i, j, 0))
    return pl.pallas_call(
        _nsa_out_t_kernel, grid=(b, t // tm),
        in_specs=[tok, tok, row, pl.BlockSpec(w_out.shape, lambda i, j: (0, 0))],
        out_specs=row, out_shape=jax.ShapeDtypeStruct((b, t, d), F32),
        compiler_params=_cparams("parallel", "parallel"), name="nsa_out",
    )(ot, zt, x, w_out)


def nsa_prompt_layer_t(x, nw, w_in, w_out, pe, w1, b1, w2):
    b, t, d = x.shape
    qt, kvt, wint, gt, zt, kn, kwn = nsa_in_proj(x, nw, w_in, tm=256)
    n_seg = t // CMP_STRIDE
    kvc = compress_tokens(kn[:, 0:2].reshape(b, 2, N_KV_HEADS, n_seg, CMP_STRIDE * HEAD_DIM), pe, w1, b1, w2)
    ot = nsa_attention(qt, gt, kvc, kn, kwn, kvt, wint)
    y = nsa_out_t(ot, zt, x, w_out.astype(BF16), tm=256)
    wr = min(WINDOW, t)
    kv5 = kvt.reshape(b, 4, N_KV_HEADS, HEAD_DIM, t).transpose(0, 4, 1, 2, 3)
    win5 = wint[:, :, t - wr:].reshape(b, 2, N_KV_HEADS, HEAD_DIM, wr).transpose(0, 4, 1, 2, 3)
    return y, kv5, win5


def _split_nsa_w_in(w_in, dtype):
    o1 = ATT_WIDTH
    o2 = o1 + 4 * KV_WIDTH
    o3 = o2 + 2 * KV_WIDTH
    o4 = o3 + 3 * N_HEADS
    wg = jnp.pad(w_in[:, o3:o4], ((0, 0), (0, LANES - 3 * N_HEADS)))
    return [w.astype(dtype) for w in (w_in[:, :o1], w_in[:, o1:o2], w_in[:, o2:o3], wg, w_in[:, o4:])]


NSA_ACTS = (None, None, None, "sigmoid", None)


def nsa_prompt_layer(x, nw, w_in, w_out, pe, w1, b1, w2):
    b, t, d = x.shape
    n_blk = t // SEL_BLK
    assert n_blk <= HEAD_DIM
    q, kv, win, gates, z = rms_proj(x.reshape(b * t, d), nw, _split_nsa_w_in(w_in, BF16), NSA_ACTS, tm=256, name="nsa_in_proj")
    q = q.reshape(b, t, ATT_WIDTH)
    kv5 = kv.reshape(b, t, 4, N_KV_HEADS, HEAD_DIM)
    win5 = win.reshape(b, t, 2, N_KV_HEADS, HEAD_DIM)
    kvt = kv5.transpose(0, 2, 3, 1, 4).astype(BF16)
    wint = win5.transpose(0, 2, 3, 1, 4).astype(BF16)
    gates_r = gates[:, :3 * N_HEADS].reshape(b, t, N_KV_HEADS, 3 * GROUP).transpose(0, 2, 1, 3)
    slopes = _alibi_slopes()
    n_seg = t // CMP_STRIDE
    kvc = compress_tokens(kvt[:, 0:2].reshape(b, 2, N_KV_HEADS, n_seg, CMP_STRIDE * HEAD_DIM), pe, w1, b1, w2)
    o_c, sel = cmp_select(q, kvc, gates_r, slopes, _cmp_to_sel_matrix(n_seg, n_blk), tq=128)
    onehot = (jnp.arange(t)[:, None] // SEL_BLK == jnp.arange(HEAD_DIM)[None, :]).astype(BF16)
    k_aug = jnp.concatenate([kvt[:, 2], jnp.broadcast_to(onehot, (b, N_KV_HEADS, t, HEAD_DIM))], axis=-1)
    o_s = sel_attention(q, sel, k_aug, kvt[:, 3], gates_r, slopes, tq=128, tk=512)
    o_w = win_attention(q, wint[:, 0], wint[:, 1], gates_r, slopes, tq=128)
    r2 = lambda a: a.reshape(b * t, -1)
    y = nsa_out(r2(o_c), r2(o_s), r2(o_w), z, r2(x), w_out.astype(BF16), tm=256)
    wr = min(WINDOW, t)
    return y.reshape(b, t, d), kv5, win5[:, t - wr:]


def _log_sigmoid(x):
    return jnp.minimum(x, 0.0) - jnp.log(1.0 + jnp.exp(-jnp.abs(x)))


def _mconv_body(shifted, xm, cw_ref, cb_ref, wbd_ref, wg_ref, bg_ref, q_ref, k_ref, v_ref, c_ref, g_ref, exact):
    conv = cb_ref[...]
    for j in range(CONV_W):
        conv = conv + shifted[j] * cw_ref[j:j + 1, :]
    c = _silu(conv)
    c_ref[...] = c
    cast = (lambda a: a) if exact else (lambda a: a.astype(BF16))
    kw = dict(preferred_element_type=F32, precision=HI) if exact else dict(preferred_element_type=F32)
    gpre = bg_ref[...]
    for m, (src, dst) in enumerate(((c, q_ref), (c, k_ref), (xm, v_ref))):
        for gi in range(D_INNER // LANES):
            sl = slice(gi * LANES, (gi + 1) * LANES)
            y = jnp.dot(cast(src[:, sl]), wbd_ref[m, gi], **kw)
            gpre = gpre + jnp.dot(cast(y), wg_ref[m * D_INNER + gi * LANES:m * D_INNER + (gi + 1) * LANES, :], **kw)
            dst[:, sl] = y * (M_HEAD_DIM ** -0.5) if m == 1 else y
    lane = lax.broadcasted_iota(jnp.int32, gpre.shape, 1)
    g_ref[...] = jnp.where(lane < M_HEADS, gpre, _log_sigmoid(gpre))


def _mconv_prompt_kernel(xm_ref, halo_ref, cw_ref, cb_ref, wbd_ref, wg_ref, bg_ref, q_ref, k_ref, v_ref, c_ref, g_ref, *, tm):
    xm = xm_ref[0]
    halo = jnp.where(pl.program_id(1) == 0, 0.0, halo_ref[0])
    ext = jnp.concatenate([halo, xm], axis=0)
    shifted = [ext[5 + j:5 + j + tm] for j in range(CONV_W - 1)] + [xm]
    _mconv_body(shifted, xm, cw_ref, cb_ref, wbd_ref, wg_ref, bg_ref, q_ref.at[0], k_ref.at[0], v_ref.at[0], c_ref.at[0],
                g_ref.at[0], False)


def _mlstm_small_weights(w_qkv, w_gate, b_gate, dtype):
    nb = LANES // QKV_BLK
    w = w_qkv.reshape(3, D_INNER // LANES, nb, QKV_BLK, QKV_BLK)
    eye = jnp.eye(nb, dtype=w.dtype)
    wbd = jnp.einsum("mgnji,nk->mgnjki", w, eye).reshape(3, D_INNER // LANES, LANES, LANES)
    wg = jnp.pad(w_gate, ((0, 0), (0, LANES - 2 * M_HEADS)))
    bg = jnp.pad(b_gate, (0, LANES - 2 * M_HEADS)).reshape(1, LANES)
    return wbd.astype(dtype), wg.astype(dtype), bg


def mconv_prompt(xm, conv_w, conv_b, w_qkv, w_gate, b_gate, *, tm):
    b, t, _ = xm.shape
    wbd, wg, bg = _mlstm_small_weights(w_qkv, w_gate, b_gate, BF16)
    row = pl.BlockSpec((1, tm, D_INNER), lambda i, j: (i, j, 0))
    full = lambda a: pl.BlockSpec(a.shape, lambda i, j: (0,) * a.ndim)
    cb = conv_b.reshape(1, D_INNER)
    return pl.pallas_call(
        functools.partial(_mconv_prompt_kernel, tm=tm), grid=(b, t // tm),
        in_specs=[row, pl.BlockSpec((1, 8, D_INNER), lambda i, j: (i, jnp.maximum(j * (tm // 8) - 1, 0), 0)),
                  full(conv_w), full(cb), full(wbd), full(wg), full(bg)],
        out_specs=[row, row, row, row, pl.BlockSpec((1, tm, LANES), lambda i, j: (i, j, 0))],
        out_shape=[jax.ShapeDtypeStruct((b, t, D_INNER), F32)] * 4 + [jax.ShapeDtypeStruct((b, t, LANES), F32)],
        compiler_params=_cparams("parallel", "parallel"), name="mlstm_conv_qkv",
    )(xm, xm, conv_w, cb, wbd, wg, bg)


def _mlstm_cell_kernel(q_ref, k_ref, v_ref, gc_ref, gr_ref, c0_ref, n0_ref, m0_ref, h_ref, cf_ref, nf_ref, mf_ref,
                       c_s, n_s, m_s, *, chunk, n_chunks):
    ci = pl.program_id(2)

    @pl.when(ci == 0)
    def _():
        c_s[...] = c0_ref[0, 0]
        n_s[...] = n0_ref[0, 0]
        m_s[...] = m0_ref[0, 0]

    q, k, v = q_ref[0], k_ref[0], v_ref[0]
    icol, fcol = gc_ref[0, 0, :, 0:1], gc_ref[0, 0, :, 1:2]
    irow, frow = gr_ref[0, 0, 0:1, :], gr_ref[0, 0, 1:2, :]
    ri = lax.broadcasted_iota(jnp.int32, (chunk, chunk), 0)
    cj = lax.broadcasted_iota(jnp.int32, (chunk, chunk), 1)
    causal = cj <= ri
    b_col = jnp.dot(jnp.where(causal, 1.0, 0.0), jnp.broadcast_to(fcol, (chunk, LANES)),
                    preferred_element_type=F32, precision=HI)[:, 0:1]
    b_row = jnp.dot(jnp.broadcast_to(frow, (8, chunk)), jnp.where(ri <= cj, 1.0, 0.0),
                    preferred_element_type=F32, precision=HI)[0:1, :]
    dmat = jnp.where(causal, b_col - b_row + irow, -jnp.inf)
    m_prev = m_s[0:1, 0:1]
    inter = b_col + m_prev
    mt = jnp.maximum(jnp.max(dmat, axis=-1, keepdims=True), inter)
    qb, kb, vb = q.astype(BF16), k.astype(BF16), v.astype(BF16)
    s = lax.dot_general(qb, kb, NT, preferred_element_type=F32) * jnp.exp(dmat - mt)
    decay = jnp.exp(inter - mt)
    num = (jnp.dot(s.astype(BF16), vb, preferred_element_type=F32)
           + decay * jnp.dot(qb, c_s[...].astype(BF16), preferred_element_type=F32))
    den = jnp.sum(s, axis=-1, keepdims=True) + decay * jnp.sum(q * n_s[...], axis=-1, keepdims=True)
    hc = num / jnp.maximum(jnp.abs(den), jnp.exp(-mt))
    h_ref[0] = hc * lax.rsqrt(jnp.mean(hc * hc, axis=-1, keepdims=True) + RMS_EPS)
    m_new = mt[chunk - 1:chunk, :]
    b_last = b_col[chunk - 1:chunk, :]
    kw = k * jnp.exp(b_last - b_col + icol - m_new)
    carry = jnp.exp(b_last + m_prev - m_new)
    c_s[...] = carry * c_s[...] + lax.dot_general(kw.astype(BF16), vb, TN, preferred_element_type=F32)
    n_s[...] = carry * n_s[...] + jnp.sum(kw, axis=0, keepdims=True)
    m_s[...] = jnp.broadcast_to(m_new, m_s.shape)

    @pl.when(ci == n_chunks - 1)
    def _():
        cf_ref[0, 0] = c_s[...]
        nf_ref[0, 0] = n_s[...]
        mf_ref[0, 0] = m_s[...]


def mlstm_cell(q, k, v, gates, c0, n0, m0, *, chunk):
    b, t, _ = q.shape
    n_chunks = t // chunk
    g_col = jnp.stack([gates[..., :M_HEADS], gates[..., M_HEADS:2 * M_HEADS]], axis=-1).transpose(0, 2, 1, 3)
    g_row = g_col.transpose(0, 1, 3, 2)
    m0b = jnp.broadcast_to(m0[:, :, None, None], (b, M_HEADS, 8, LANES))
    n0r = n0.reshape(b, M_HEADS, 1, M_HEAD_DIM)
    head = pl.BlockSpec((1, chunk, M_HEAD_DIM), lambda i, h, c: (i, c, h))
    st = lambda *blk: pl.BlockSpec((1, 1) + blk, lambda i, h, c: (i, h, 0, 0))
    hn, cf, nf, mf = pl.pallas_call(
        functools.partial(_mlstm_cell_kernel, chunk=chunk, n_chunks=n_chunks), grid=(b, M_HEADS, n_chunks),
        in_specs=[head, head, head,
                  pl.BlockSpec((1, 1, chunk, 2), lambda i, h, c: (i, h, c, 0)),
                  pl.BlockSpec((1, 1, 2, chunk), lambda i, h, c: (i, h, 0, c)),
                  st(M_HEAD_DIM, M_HEAD_DIM), st(1, M_HEAD_DIM), st(8, LANES)],
        out_specs=[head, st(M_HEAD_DIM, M_HEAD_DIM), st(1, M_HEAD_DIM), st(8, LANES)],
        out_shape=[jax.ShapeDtypeStruct((b, t, D_INNER), F32), jax.ShapeDtypeStruct((b, M_HEADS, M_HEAD_DIM, M_HEAD_DIM), F32),
                   jax.ShapeDtypeStruct((b, M_HEADS, 1, M_HEAD_DIM), F32), jax.ShapeDtypeStruct((b, M_HEADS, 8, LANES), F32)],
        scratch_shapes=[pltpu.VMEM((M_HEAD_DIM, M_HEAD_DIM), F32), pltpu.VMEM((1, M_HEAD_DIM), F32), pltpu.VMEM((8, LANES), F32)],
        compiler_params=_cparams("parallel", "parallel", "arbitrary"), name="mlstm_cell",
    )(q, k, v, g_col, g_row, c0, n0r, m0b)
    return hn, cf, nf.reshape(b, M_HEADS, M_HEAD_DIM), mf[:, :, 0, 0]


def _mlstm_out_kernel(hn_ref, c_ref, z_ref, x_ref, nw_ref, sk_ref, w_ref, fw_ref, y_ref, *, final, exact):
    a = (hn_ref[...] * nw_ref[...] + sk_ref[...] * c_ref[...]) * _silu(z_ref[...])
    if exact:
        y = x_ref[...] + jnp.dot(a, w_ref[...], preferred_element_type=F32, precision=HI)
    else:
        y = x_ref[...] + jnp.dot(a.astype(BF16), w_ref[...], preferred_element_type=F32)
    y_ref[...] = _rmsnorm(y, fw_ref[...]) if final else y


def mlstm_out(hn, c, z, x, norm_w, skip, w_out, final_w, *, tm, final, exact=False):
    m, d = x.shape
    wide = pl.BlockSpec((tm, D_INNER), lambda i: (i, 0))
    row = pl.BlockSpec((tm, d), lambda i: (i, 0))
    vec = lambda n: pl.BlockSpec((1, n), lambda i: (0, 0))
    return pl.pallas_call(
        functools.partial(_mlstm_out_kernel, final=final, exact=exact), grid=(m // tm,),
        in_specs=[wide, wide, wide, row, vec(D_INNER), vec(D_INNER), pl.BlockSpec(w_out.shape, lambda i: (0, 0)), vec(d)],
        out_specs=row, out_shape=jax.ShapeDtypeStruct((m, d), F32),
        compiler_params=_cparams("parallel"), name="mlstm_out",
    )(hn, c, z, x, norm_w.reshape(1, D_INNER), skip.reshape(1, D_INNER), w_out, final_w.reshape(1, d))


def mlstm_prompt_layer(x, nw, w_in, conv_w, conv_b, w_qkv, w_gate, b_gate, norm_w, skip, w_out, final_w, *, final):
    b, t, d = x.shape
    w_in = w_in.astype(BF16)
    xm, z = rms_proj(x.reshape(b * t, d), nw, [w_in[:, :D_INNER], w_in[:, D_INNER:]], (None, None), tm=256, name="mlstm_in_proj")
    xm3 = xm.reshape(b, t, D_INNER)
    q, k, v, c, gates = mconv_prompt(xm3, conv_w, conv_b, w_qkv, w_gate, b_gate, tm=256)
    c0 = jnp.zeros((b, M_HEADS, M_HEAD_DIM, M_HEAD_DIM), F32)
    n0 = jnp.zeros((b, M_HEADS, M_HEAD_DIM), F32)
    m0 = jnp.full((b, M_HEADS), -jnp.inf, F32)
    hn, cf, nf, mf = mlstm_cell(q, k, v, gates, c0, n0, m0, chunk=min(256, t))
    y = mlstm_out(hn.reshape(b * t, D_INNER), c.reshape(b * t, D_INNER), z, x.reshape(b * t, d), norm_w, skip,
                  w_out.astype(BF16), final_w, tm=256, final=final)
    return y.reshape(b, t, d), cf, nf, mf, xm3[:, t - (CONV_W - 1):]


def _mconv_sample_kernel(xm_ref, hist_ref, cw_ref, cb_ref, wbd_ref, wg_ref, bg_ref, q_ref, k_ref, v_ref, c_ref, g_ref):
    xm = xm_ref[...]
    shifted = [hist_ref[j] for j in range(CONV_W - 1)] + [xm]
    _mconv_body(shifted, xm, cw_ref, cb_ref, wbd_ref, wg_ref, bg_ref, q_ref, k_ref, v_ref, c_ref, g_ref, True)


def mconv_sample(xm, hist, conv_w, conv_b, w_qkv, w_gate, b_gate):
    b = xm.shape[0]
    wbd, wg, bg = _mlstm_small_weights(w_qkv, w_gate, b_gate, F32)
    cb = conv_b.reshape(1, D_INNER)
    full = lambda a: pl.BlockSpec(a.shape, lambda i: (0,) * a.ndim)
    args = (xm, hist, conv_w, cb, wbd, wg, bg)
    row = pl.BlockSpec((b, D_INNER), lambda i: (0, 0))
    return pl.pallas_call(
        _mconv_sample_kernel, grid=(1,), in_specs=[full(a) for a in args],
        out_specs=[row, row, row, row, pl.BlockSpec((b, LANES), lambda i: (0, 0))],
        out_shape=[jax.ShapeDtypeStruct((b, D_INNER), F32)] * 4 + [jax.ShapeDtypeStruct((b, LANES), F32)],
        compiler_params=_cparams("arbitrary"), name="mlstm_conv_qkv_sample",
    )(*args)


def _mlstm_step_kernel(q_ref, k_ref, v_ref, g_ref, c0_ref, n0_ref, h_ref, cf_ref, nf_ref, mf_ref):
    q, k, v = q_ref[0], k_ref[0], v_ref[0]
    g = g_ref[0, 0]
    ig, fl, m0 = g[:, 0:1], g[:, 1:2], g[:, 2:3]
    c0, n0 = c0_ref[0, 0, 0], n0_ref[0, 0]
    m_new = jnp.maximum(fl + m0, ig)
    decay = jnp.exp(fl + m0 - m_new)
    sw = jnp.exp(ig - m_new)
    s = jnp.sum(q * k, axis=-1, keepdims=True) * sw
    qc = jnp.dot(jnp.broadcast_to(q, (8, M_HEAD_DIM)), c0, preferred_element_type=F32, precision=HI)[0:1]
    num = s * v + decay * qc
    den = s + decay * jnp.sum(q * n0, axis=-1, keepdims=True)
    hc = num / jnp.maximum(jnp.abs(den), jnp.exp(-m_new))
    h_ref[0] = hc * lax.rsqrt(jnp.mean(hc * hc, axis=-1, keepdims=True) + RMS_EPS)
    row0 = lax.broadcasted_iota(jnp.int32, (8, M_HEAD_DIM), 0) == 0
    k8 = jnp.where(row0, k * sw, 0.0)
    cf_ref[0, 0] = decay * c0 + lax.dot_general(k8, jnp.broadcast_to(v, (8, M_HEAD_DIM)), TN,
                                                preferred_element_type=F32, precision=HI)
    nf_ref[0, 0] = decay * n0 + sw * k
    mf_ref[0, 0] = jnp.broadcast_to(m_new, (1, LANES))


def mlstm_step(layer, q, k, v, gates, c_all, n0, m0):
    b = q.shape[0]
    gsm = jnp.stack([gates[:, :M_HEADS], gates[:, M_HEADS:2 * M_HEADS], m0], axis=-1)
    gsm = jnp.pad(gsm, ((0, 0), (0, 0), (0, LANES - 3))).reshape(b, M_HEADS, 1, LANES)
    r3 = lambda a: a.reshape(b, 1, D_INNER)
    head = pl.BlockSpec((1, 1, M_HEAD_DIM), lambda i, h: (i, 0, h))
    st = lambda *blk: pl.BlockSpec((1, 1) + blk, lambda i, h: (i, h, 0, 0))
    hn, cf, nf, mf = pl.pallas_call(
        _mlstm_step_kernel, grid=(b, M_HEADS),
        in_specs=[head, head, head, st(1, LANES),
                  pl.BlockSpec((1, 1, 1, M_HEAD_DIM, M_HEAD_DIM), lambda i, h: (layer, i, h, 0, 0)), st(1, M_HEAD_DIM)],
        out_specs=[head, st(M_HEAD_DIM, M_HEAD_DIM), st(1, M_HEAD_DIM), st(1, LANES)],
        out_shape=[jax.ShapeDtypeStruct((b, 1, D_INNER), F32), jax.ShapeDtypeStruct((b, M_HEADS, M_HEAD_DIM, M_HEAD_DIM), F32),
                   jax.ShapeDtypeStruct((b, M_HEADS, 1, M_HEAD_DIM), F32), jax.ShapeDtypeStruct((b, M_HEADS, 1, LANES), F32)],
        compiler_params=_cparams("parallel", "parallel"), name="mlstm_step",
    )(r3(q), r3(k), r3(v), gsm, c_all, n0.reshape(b, M_HEADS, 1, M_HEAD_DIM))
    return hn.reshape(b, D_INNER), cf, nf.reshape(b, M_HEADS, M_HEAD_DIM), mf[:, :, 0, 0]


def mlstm_sample_layer(layer, x, conv_state, c_all, n0, m0, nw, w_in, conv_w, conv_b, w_qkv, w_gate, b_gate, norm_w, skip,
                       w_out, final_w, *, final):
    b = x.shape[0]
    xm, z = rms_proj(x, nw, [w_in[:, :D_INNER], w_in[:, D_INNER:]], (None, None), tm=b, exact=True, name="mlstm_in_proj_sample")
    hist = conv_state.transpose(1, 0, 2)
    q, k, v, c, gates = mconv_sample(xm, hist, conv_w, conv_b, w_qkv, w_gate, b_gate)
    hn, cf, nf, mf = mlstm_step(layer, q, k, v, gates, c_all, n0, m0)
    y = mlstm_out(hn, c, z, x, norm_w, skip, w_out, final_w, tm=b, final=final, exact=True)
    conv_new = jnp.concatenate([hist[1:], xm[None]], axis=0).transpose(1, 0, 2)
    return y, cf, nf, mf, conv_new


SAMPLE_ROWS = 8
PAGES_PER_STEP = 8
N_BLK_PAD = 256


def _row_scalars(vals):
    row = lax.broadcasted_iota(jnp.int32, (SAMPLE_ROWS, 1), 0)
    col = jnp.zeros((SAMPLE_ROWS, 1), F32)
    for g, v in enumerate(vals):
        col = jnp.where(row == g, v, col)
    return col


def _nsa_sample_kernel(pt_ref, slopes_ref, *refs, gp, n_groups, past_len):
    cmp_pages, sel_pages = refs[:gp], refs[gp:2 * gp]
    (q_ref, qbd_ref, kvn_ref, cwin_ref, wn_ref, g_ref, wp_ref, pe_ref, w1f_ref, b1_ref, w2_ref, ov_ref, ex_ref, o_ref,
     cmp_s, kc_s, vc_s, bias_s, oc_s, m_s, l_s, acc_s) = refs[2 * gp:]
    step = pl.program_id(1)
    t = past_len
    n_seg = past_len // CMP_STRIDE
    n_blk = past_len // SEL_BLK + 1
    span = gp * PAGE_SIZE
    slope_cols = [_row_scalars([slopes_ref[kh * GROUP + g] for g in range(GROUP)]) for kh in range(N_KV_HEADS)]

    @pl.when(step < n_groups)
    def _stash():
        for i in range(gp):
            row0 = pl.multiple_of((step * gp + i) * PAGE_SIZE, PAGE_SIZE)
            for c in range(2):
                for hp in range(N_KV_HEADS // 2):
                    a = cmp_pages[i][0, 0, c, 2 * hp:2 * hp + 2].reshape(2 * HEAD_DIM, PAGE_SIZE)
                    cmp_s[c * 2 + hp, pl.ds(row0, PAGE_SIZE), :] = a.T

    @pl.when(step == n_groups - 1)
    def _compress_and_select():
        for slab in range(4):
            c, hp = divmod(slab, 2)
            x = jnp.concatenate([cmp_s[slab, pl.ds(r, n_seg, stride=CMP_STRIDE), :] for r in range(CMP_STRIDE)], axis=1)
            part = jnp.dot(x.astype(BF16), wp_ref[c], preferred_element_type=F32)
            hidc = jnp.dot(pe_ref[c], w1f_ref[c], preferred_element_type=F32)[0:1] + b1_ref[c]
            for hh in range(2):
                p0 = part[:, hh * 2 * CMP_HID:hh * 2 * CMP_HID + CMP_HID]
                p1 = part[:, hh * 2 * CMP_HID + CMP_HID:(hh + 1) * 2 * CMP_HID]
                hid = hidc + p0 + pltpu.roll(p1, n_seg - 1, 0)
                tok = jnp.dot(_silu(hid).astype(BF16), w2_ref[c], preferred_element_type=F32)
                if c == 0:
                    kc_s[2 * hp + hh] = tok
                else:
                    vc_s[2 * hp + hh] = tok
        cend = lax.broadcasted_iota(jnp.int32, (1, n_seg), 1) * CMP_STRIDE + (CMP_BLK - 1)
        d_c = (t - cend).astype(F32)
        ok = d_c >= 0.0
        row = lax.broadcasted_iota(jnp.int32, (SAMPLE_ROWS, 1), 0)
        blk = lax.broadcasted_iota(jnp.int32, (1, N_BLK_PAD), 1)
        tb = t // SEL_BLK
        forced = (blk == 0) | (blk == tb) | (blk == tb - 1)
        ii = lax.broadcasted_iota(jnp.int32, (N_BLK_PAD, N_BLK_PAD), 0)
        jj = lax.broadcasted_iota(jnp.int32, (N_BLK_PAD, N_BLK_PAD), 1)
        sel_rows = []
        for kh in range(N_KV_HEADS):
            q8 = (q_ref[0, kh] * Q_SCALE).astype(BF16)
            s = lax.dot_general(q8, kc_s[kh].astype(BF16), NT, preferred_element_type=F32) - slope_cols[kh] * d_c
            s = jnp.where(ok, s, NEG)
            m = jnp.max(s, axis=-1, keepdims=True)
            p = jnp.where(ok, jnp.exp(s - m), 0.0)
            pn = (p / jnp.maximum(jnp.sum(p, axis=-1, keepdims=True), 1e-30)).astype(BF16)
            oc_s[kh] = jnp.dot(pn, vc_s[kh].astype(BF16), preferred_element_type=F32)
            pn_heads = jnp.where(row < GROUP, pn, jnp.zeros_like(pn))
            imp = jnp.sum(jnp.dot(pn_heads, ov_ref[...], preferred_element_type=F32), axis=0, keepdims=True)
            score = jnp.where(forced, FORCE_SCORE, jnp.where(blk * SEL_BLK <= t, imp, -1.0))
            score = jnp.where(blk < n_blk, score, -2.0)
            col = jnp.broadcast_to(score, (SAMPLE_ROWS, N_BLK_PAD)).T[:, 0:1]
            beats = (col > score) | ((ii < jj) & (col == score))
            cnt = jnp.sum(jnp.where(beats, 1.0, 0.0), axis=0, keepdims=True)
            sel = jnp.where((cnt < float(min(N_SEL, n_blk))) & (score > -0.5), 1.0, 0.0)
            sel_rows.append(jnp.broadcast_to(sel, (SAMPLE_ROWS, N_BLK_PAD)))
        picked = jnp.dot(jnp.concatenate(sel_rows, axis=0).astype(BF16), ex_ref[...], preferred_element_type=F32)
        key_pos = lax.broadcasted_iota(jnp.int32, (1, past_len), 1)
        bias = jnp.where(picked > 0.5, jnp.concatenate(slope_cols, axis=0) * (key_pos - t).astype(F32), NEG)
        for gi in range(n_groups):
            bias_s[gi] = bias[:, gi * span:(gi + 1) * span]
        m_s[...] = jnp.full(m_s.shape, M_INIT, F32)
        l_s[...] = jnp.zeros(l_s.shape, F32)
        acc_s[...] = jnp.zeros(acc_s.shape, F32)

    @pl.when(step >= n_groups)
    def _selected():
        gb = step - n_groups
        kk = jnp.concatenate([sel_pages[i][0, 0, 0].reshape(KV_WIDTH, PAGE_SIZE) for i in range(gp)], axis=1).astype(BF16)
        vv = jnp.concatenate([sel_pages[i][0, 0, 1].reshape(KV_WIDTH, PAGE_SIZE) for i in range(gp)], axis=1).astype(BF16)
        s = jnp.dot(qbd_ref[0], kk, preferred_element_type=F32) + bias_s[gb]
        m_old = m_s[:, 0:1]
        m_new = jnp.maximum(m_old, jnp.max(s, axis=-1, keepdims=True))
        alpha = jnp.exp(m_old - m_new)
        p = jnp.exp(s - m_new)
        l_s[...] = jnp.broadcast_to(alpha * l_s[:, 0:1] + jnp.sum(p, axis=-1, keepdims=True), l_s.shape)
        acc_s[...] = alpha * acc_s[...] + lax.dot_general(p.astype(BF16), vv, NT, preferred_element_type=F32)
        m_s[...] = jnp.broadcast_to(m_new, m_s.shape)

    @pl.when(step == 2 * n_groups - 1)
    def _finish():
        wr = cwin_ref.shape[-1]
        d_w = wr - lax.broadcasted_iota(jnp.int32, (1, wr), 1)
        ok_w = (d_w <= WINDOW) & (t - d_w >= 0)
        for kh in range(N_KV_HEADS):
            q8 = q_ref[0, kh] * Q_SCALE
            rows = slice(kh * SAMPLE_ROWS, (kh + 1) * SAMPLE_ROWS)
            s_n = jnp.sum(q8 * kvn_ref[0, 2, kh], axis=-1, keepdims=True)
            m_old = m_s[rows, 0:1]
            m_new = jnp.maximum(m_old, s_n)
            alpha = jnp.exp(m_old - m_new)
            p_n = jnp.exp(s_n - m_new)
            l = alpha * l_s[rows, 0:1] + p_n
            acc = acc_s[rows, kh * HEAD_DIM:(kh + 1) * HEAD_DIM]
            o_sel = (alpha * acc + p_n * kvn_ref[0, 3, kh]) / jnp.maximum(l, 1e-30)
            s_w = jnp.dot(q8.astype(BF16), cwin_ref[0, 0, 0, kh].astype(BF16), preferred_element_type=F32)
            s_w = jnp.where(ok_w, s_w - slope_cols[kh] * d_w.astype(F32), NEG)
            s_wn = jnp.sum(q8 * wn_ref[0, 0, kh], axis=-1, keepdims=True)
            m_w = jnp.maximum(jnp.max(s_w, axis=-1, keepdims=True), s_wn)
            p_w = jnp.where(ok_w, jnp.exp(s_w - m_w), 0.0)
            p_wn = jnp.exp(s_wn - m_w)
            l_w = jnp.sum(p_w, axis=-1, keepdims=True) + p_wn
            o_win = (lax.dot_general(p_w.astype(BF16), cwin_ref[0, 0, 1, kh].astype(BF16), NT, preferred_element_type=F32)
                     + p_wn * wn_ref[0, 1, kh]) / jnp.maximum(l_w, 1e-30)
            gts = g_ref[0, kh]
            o_ref[0, kh] = gts[:, 0:1] * oc_s[kh] + gts[:, 1:2] * o_sel + gts[:, 2:3] * o_win


def _pair_w1(w1):
    wr = w1.reshape(2, CMP_R, CMP_STRIDE, HEAD_DIM, CMP_HID).transpose(0, 2, 3, 1, 4)
    wp = jnp.einsum("crdje,hk->crhdkje", wr, jnp.eye(2, dtype=w1.dtype))
    return wp.reshape(2, CMP_STRIDE * 2 * HEAD_DIM, 2 * CMP_R * CMP_HID).astype(BF16)


def nsa_sample_attention(layer, q, kv, win, gates, cache_kv, cache_win, page_table, pe, w1, b1, w2):
    b = q.shape[0]
    n_pages = page_table.shape[1]
    past_len = n_pages * PAGE_SIZE
    gp = PAGES_PER_STEP
    assert n_pages % gp == 0 and cache_win.shape[2] == WINDOW and past_len // SEL_BLK + 1 <= N_BLK_PAD
    n_groups = n_pages // gp
    n_seg = past_len // CMP_STRIDE
    pad_rows = lambda a: jnp.pad(a, ((0, 0), (0, 0), (0, SAMPLE_ROWS - GROUP), (0, 0)))
    q4 = pad_rows(q.reshape(b, N_KV_HEADS, GROUP, HEAD_DIM))
    g4 = pad_rows(gates[:, :3 * N_HEADS].reshape(b, N_KV_HEADS, GROUP, 3))
    q_bd = jnp.einsum("bkgd,kj->bkgjd", q4 * Q_SCALE, jnp.eye(N_KV_HEADS, dtype=F32))
    q_bd = q_bd.reshape(b, N_KV_HEADS * SAMPLE_ROWS, KV_WIDTH).astype(BF16)
    kvn = kv.reshape(b, 4, N_KV_HEADS, 1, HEAD_DIM)
    wn = win.reshape(b, 2, N_KV_HEADS, 1, HEAD_DIM)
    cache_t = cache_kv.transpose(0, 1, 3, 4, 5, 2)
    cwin_t = cache_win.transpose(0, 1, 3, 4, 5, 2)
    start = np.arange(n_seg) * CMP_STRIDE
    blk = np.arange(N_BLK_PAD) * SEL_BLK
    ov = jnp.asarray((start[:, None] < blk[None, :] + SEL_BLK) & (start[:, None] + CMP_BLK > blk[None, :]), dtype=BF16)
    pe8 = jnp.broadcast_to(pe.reshape(2, 1, CMP_BLK * HEAD_DIM), (2, 8, CMP_BLK * HEAD_DIM)).astype(BF16)
    w1f = w1.reshape(2, CMP_BLK * HEAD_DIM, CMP_HID).astype(BF16)
    expand = (jnp.arange(N_BLK_PAD)[:, None] == jnp.arange(past_len)[None, :] // SEL_BLK).astype(BF16)
    consts = (_pair_w1(w1), pe8, w1f, b1.reshape(2, 1, CMP_HID), w2.astype(BF16), ov, expand)
    page_blk = (1, 1, 2, N_KV_HEADS, HEAD_DIM, PAGE_SIZE)

    def cmp_map(i):
        return lambda bi, s, pt, sl: (layer, pt[bi * n_pages + jnp.minimum(s, n_groups - 1) * gp + i], 0, 0, 0, 0)

    def sel_map(i):
        return lambda bi, s, pt, sl: (layer, pt[bi * n_pages + jnp.maximum(s - n_groups, 0) * gp + i], 1, 0, 0, 0)

    per_b = lambda a: pl.BlockSpec((1,) + a.shape[1:], lambda bi, s, pt, sl: (bi,) + (0,) * (a.ndim - 1))
    const = lambda a: pl.BlockSpec(a.shape, lambda bi, s, pt, sl: (0,) * a.ndim)
    in_specs = [pl.BlockSpec(page_blk, cmp_map(i)) for i in range(gp)] + [pl.BlockSpec(page_blk, sel_map(i)) for i in range(gp)]
    in_specs += [per_b(q4), per_b(q_bd), per_b(kvn),
                 pl.BlockSpec((1, 1) + cwin_t.shape[2:], lambda bi, s, pt, sl: (layer, bi, 0, 0, 0, 0)),
                 per_b(wn), per_b(g4)] + [const(a) for a in consts]
    all_rows = N_KV_HEADS * SAMPLE_ROWS
    small = pltpu.VMEM((all_rows, LANES), F32)
    grid_spec = pltpu.PrefetchScalarGridSpec(
        num_scalar_prefetch=2, grid=(b, 2 * n_groups), in_specs=in_specs, out_specs=per_b(q4),
        scratch_shapes=[pltpu.VMEM((4, past_len, LANES), F32),
                        pltpu.VMEM((N_KV_HEADS, n_seg, HEAD_DIM), F32), pltpu.VMEM((N_KV_HEADS, n_seg, HEAD_DIM), F32),
                        pltpu.VMEM((n_groups, all_rows, gp * PAGE_SIZE), F32),
                        pltpu.VMEM((N_KV_HEADS, SAMPLE_ROWS, HEAD_DIM), F32), small, small,
                        pltpu.VMEM((all_rows, KV_WIDTH), F32)])
    o = pl.pallas_call(
        functools.partial(_nsa_sample_kernel, gp=gp, n_groups=n_groups, past_len=past_len), grid_spec=grid_spec,
        out_shape=jax.ShapeDtypeStruct(q4.shape, F32),
        compiler_params=_cparams("parallel", "arbitrary"), name="nsa_sample_attn",
    )(page_table.reshape(-1), _alibi_slopes(), *([cache_t] * (2 * gp)), q4, q_bd, kvn, cwin_t, wn, g4, *consts)
    return o[:, :, :GROUP].reshape(b, ATT_WIDTH)


def nsa_sample_layer(layer, x, cache_kv, cache_win, page_table, nw, w_in, w_out, pe, w1, b1, w2):
    b = x.shape[0]
    q, kv, win, gates, z = rms_proj(x, nw, _split_nsa_w_in(w_in, F32), NSA_ACTS, tm=b, exact=True, name="nsa_in_proj_sample")
    o = nsa_sample_attention(layer, q, kv, win, gates, cache_kv, cache_win, page_table, pe, w1, b1, w2)
    zero = jnp.zeros_like(o)
    y = nsa_out(o, zero, zero, z, x, w_out, tm=b, exact=True)
    win_new = jnp.concatenate([cache_win[layer][:, 1:], win.reshape(b, 1, 2, N_KV_HEADS, HEAD_DIM)], axis=1)
    return y, kv.reshape(b, 1, 4, N_KV_HEADS, HEAD_DIM), win_new


def kernel(x_prompt, x_sample, cache_kv, cache_win, state_C, state_n, state_m, state_conv, page_table, norm_w, final_norm_w,
           nsa_w_in, nsa_w_out, nsa_cmp_pe, nsa_cmp_w1, nsa_cmp_b1, nsa_cmp_w2, m_w_in, m_conv_w, m_conv_b, m_w_qkv, m_w_gate,
           m_b_gate, m_norm_w, m_skip, m_w_out):
    assert DEPTH % 2 == 0 and x_sample.shape[1] == 1
    yp, ys = x_prompt, x_sample[:, 0]
    outs = {name: [] for name in ("kv_p", "kv_s", "win_p", "win_s", "C_p", "C_s", "n_p", "n_s", "m_p", "m_s", "cv_p", "cv_s")}
    for i in range(DEPTH):
        l = i // 2
        if i % 2 == 0:
            prm = (nsa_w_in[l], nsa_w_out[l], nsa_cmp_pe[l], nsa_cmp_w1[l], nsa_cmp_b1[l], nsa_cmp_w2[l])
            yp, kvp, wp = nsa_prompt_layer_t(yp, norm_w[i], *prm)
            ys, kvs, wsm = nsa_sample_layer(l, ys, cache_kv, cache_win, page_table, norm_w[i], *prm)
            for name, val in (("kv_p", kvp), ("kv_s", kvs), ("win_p", wp), ("win_s", wsm)):
                outs[name].append(val)
        else:
            prm = (m_w_in[l], m_conv_w[l], m_conv_b[l], m_w_qkv[l], m_w_gate[l], m_b_gate[l], m_norm_w[l], m_skip[l], m_w_out[l])
            final = i == DEPTH - 1
            yp, cp, np_, mp, cvp = mlstm_prompt_layer(yp, norm_w[i], *prm, final_norm_w, final=final)
            ys, cs, ns, ms, cvs = mlstm_sample_layer(l, ys, state_conv[l], state_C, state_n[l], state_m[l], norm_w[i], *prm,
                                                     final_norm_w, final=final)
            for name, val in (("C_p", cp), ("C_s", cs), ("n_p", np_), ("n_s", ns), ("m_p", mp), ("m_s", ms),
                              ("cv_p", cvp), ("cv_s", cvs)):
                outs[name].append(val)
    st = {name: jnp.stack(vals) for name, vals in outs.items()}
    return (yp, ys[:, None], st["kv_p"], st["kv_s"], st["win_p"], st["win_s"], st["C_p"], st["C_s"], st["n_p"], st["n_s"],
            st["m_p"], st["m_s"], st["cv_p"], st["cv_s"])
```

```python
import functools
import math

import jax
import jax.numpy as jnp
import numpy as np
from jax import lax
from jax.experimental import pallas as pl
from jax.experimental.pallas import tpu as pltpu

F32 = jnp.float32
BF16 = jnp.bfloat16
HI = lax.Precision.HIGHEST

D_MODEL = 1024
DEPTH = 4
N_HEADS = 16
HEAD_DIM = 64
N_KV_HEADS = 4
GROUP = N_HEADS // N_KV_HEADS
ATT_WIDTH = N_HEADS * HEAD_DIM
KV_WIDTH = N_KV_HEADS * HEAD_DIM
CMP_BLK = 32
CMP_STRIDE = 16
CMP_R = CMP_BLK // CMP_STRIDE
CMP_HID = 2 * HEAD_DIM
SEL_BLK = 64
N_SEL = 16
WINDOW = 512
FORCE_SCORE = 1.0e4
D_INNER = 2 * D_MODEL
M_HEADS = 4
M_HEAD_DIM = D_INNER // M_HEADS
CONV_W = 4
QKV_BLK = 4
RMS_EPS = 1e-6
PAGE_SIZE = 128

LANES = 128
VMEM_LIMIT = 56 * 1024 * 1024
NEG = -1.0e30
M_INIT = -5.0e29
Q_SCALE = HEAD_DIM ** -0.5
NT = (((1,), (1,)), ((), ()))
TN = (((0,), (0,)), ((), ()))


def _cparams(*sem):
    return pltpu.CompilerParams(dimension_semantics=sem, vmem_limit_bytes=VMEM_LIMIT)


def _silu(x):
    return x * jax.nn.sigmoid(x)


def _alibi_slopes():
    return jnp.asarray(np.exp2(-8.0 * np.arange(1, N_HEADS + 1) / N_HEADS), dtype=F32)


def _rmsnorm(x, w):
    return x * lax.rsqrt(jnp.mean(x * x, axis=-1, keepdims=True) + RMS_EPS) * w


def _rms_proj_kernel(x_ref, nw_ref, *refs, acts, n_chunk, exact):
    n = len(acts)
    w_refs, o_refs = refs[:n], refs[n:]
    h = _rmsnorm(x_ref[...], nw_ref[...])
    if not exact:
        h = h.astype(BF16)
    for w_ref, o_ref, act in zip(w_refs, o_refs, acts):
        width = w_ref.shape[1]
        for n0 in range(0, width, n_chunk):
            n1 = min(width, n0 + n_chunk)
            if exact:
                y = jnp.dot(h, w_ref[:, n0:n1], preferred_element_type=F32, precision=HI)
            else:
                y = jnp.dot(h, w_ref[:, n0:n1], preferred_element_type=F32)
            if act == "sigmoid":
                y = jax.nn.sigmoid(y)
            o_ref[:, n0:n1] = y


def rms_proj(x, nw, weights, acts, *, tm, exact=False, name="rms_proj"):
    m, k = x.shape
    assert m % tm == 0
    in_specs = [pl.BlockSpec((tm, k), lambda i: (i, 0)), pl.BlockSpec((1, k), lambda i: (0, 0))]
    in_specs += [pl.BlockSpec(w.shape, lambda i: (0, 0)) for w in weights]
    out_specs = [pl.BlockSpec((tm, w.shape[1]), lambda i: (i, 0)) for w in weights]
    out_shape = [jax.ShapeDtypeStruct((m, w.shape[1]), F32) for w in weights]
    return pl.pallas_call(
        functools.partial(_rms_proj_kernel, acts=tuple(acts), n_chunk=512, exact=exact),
        grid=(m // tm,), in_specs=in_specs, out_specs=out_specs, out_shape=out_shape,
        compiler_params=_cparams("parallel"), name=name,
    )(x, nw.reshape(1, k), *weights)


def _compress_kernel(x_ref, w1_ref, pe_ref, w1f_ref, b1_ref, w2_ref, o_ref, *, n_seg):
    x = x_ref[0, 0, 0]
    p0 = jnp.dot(x, w1_ref[0, 0], preferred_element_type=F32)
    p1 = jnp.dot(x, w1_ref[0, 1], preferred_element_type=F32)
    hidc = jnp.dot(pe_ref[0], w1f_ref[0], preferred_element_type=F32)[0:1] + b1_ref[0]
    hid = hidc + p0 + pltpu.roll(p1, n_seg - 1, 0)
    o_ref[0, 0, 0] = jnp.dot(_silu(hid).astype(BF16), w2_ref[0], preferred_element_type=F32)


def compress_tokens(x, pe, w1, b1, w2):
    b, _, hkv, n_seg, _ = x.shape
    w1r = w1.reshape(2, CMP_R, CMP_STRIDE * HEAD_DIM, CMP_HID).astype(BF16)
    w1f = w1.reshape(2, CMP_BLK * HEAD_DIM, CMP_HID).astype(BF16)
    pe8 = jnp.broadcast_to(pe.reshape(2, 1, CMP_BLK * HEAD_DIM), (2, 8, CMP_BLK * HEAD_DIM)).astype(BF16)
    return pl.pallas_call(
        functools.partial(_compress_kernel, n_seg=n_seg),
        grid=(b, 2, hkv),
        in_specs=[
            pl.BlockSpec((1, 1, 1, n_seg, CMP_STRIDE * HEAD_DIM), lambda i, c, h: (i, c, h, 0, 0)),
            pl.BlockSpec((1, CMP_R, CMP_STRIDE * HEAD_DIM, CMP_HID), lambda i, c, h: (c, 0, 0, 0)),
            pl.BlockSpec((1, 8, CMP_BLK * HEAD_DIM), lambda i, c, h: (c, 0, 0)),
            pl.BlockSpec((1, CMP_BLK * HEAD_DIM, CMP_HID), lambda i, c, h: (c, 0, 0)),
            pl.BlockSpec((1, 1, CMP_HID), lambda i, c, h: (c, 0, 0)),
            pl.BlockSpec((1, CMP_HID, HEAD_DIM), lambda i, c, h: (c, 0, 0)),
        ],
        out_specs=pl.BlockSpec((1, 1, 1, n_seg, HEAD_DIM), lambda i, c, h: (i, c, h, 0, 0)),
        out_shape=jax.ShapeDtypeStruct((b, 2, hkv, n_seg, HEAD_DIM), F32),
        compiler_params=_cparams("parallel", "parallel", "parallel"), name="nsa_compress",
    )(x, w1r, pe8, w1f, b1.reshape(2, 1, CMP_HID), w2.astype(BF16))


def _topk_mask_t(score_t, n_valid, k):
    r = score_t.shape[0]
    jidx = lax.broadcasted_iota(jnp.int32, (r, 1), 0)
    cnt = jnp.zeros(score_t.shape, F32)
    for i in range(n_valid):
        row = score_t[i:i + 1, :]
        beats = (row > score_t) | ((jidx > i) & (row == score_t))
        cnt = cnt + jnp.where(beats, 1.0, 0.0)
    return cnt < float(k)


def _cmp_sel_kernel(slopes_ref, q_ref, kc_ref, vc_ref, g_ref, ov_ref, oc_ref, sel_ref, *, tq, n_seg, n_blk):
    kh = pl.program_id(1)
    t0 = pl.program_id(2) * tq
    t_col = t0 + lax.broadcasted_iota(jnp.int32, (tq, 1), 0)
    cend = lax.broadcasted_iota(jnp.int32, (1, n_seg), 1) * CMP_STRIDE + (CMP_BLK - 1)
    d_c = (t_col - cend).astype(F32)
    ok = d_c >= 0.0
    kc = kc_ref[0, 0, 0].astype(BF16)
    vc = vc_ref[0, 0, 0].astype(BF16)
    imp = jnp.zeros((tq, LANES), F32)
    outs = []
    for g in range(GROUP):
        qg = (q_ref[0, :, g * HEAD_DIM:(g + 1) * HEAD_DIM] * Q_SCALE).astype(BF16)
        s = lax.dot_general(qg, kc, NT, preferred_element_type=F32) - slopes_ref[kh * GROUP + g] * d_c
        s = jnp.where(ok, s, NEG)
        m = jnp.max(s, axis=-1, keepdims=True)
        p = jnp.where(ok, jnp.exp(s - m), 0.0)
        pn = (p / jnp.maximum(jnp.sum(p, axis=-1, keepdims=True), 1e-30)).astype(BF16)
        o = jnp.dot(pn, vc, preferred_element_type=F32)
        outs.append(o * g_ref[0, 0, :, 3 * g:3 * g + 1])
        imp = imp + jnp.dot(pn, ov_ref[...], preferred_element_type=F32)
    oc_ref[0] = jnp.concatenate(outs, axis=1)
    blk = lax.broadcasted_iota(jnp.int32, (1, LANES), 1)
    tb = lax.shift_right_logical(t_col, int(math.log2(SEL_BLK)))
    forced = (blk == 0) | (blk == tb) | (blk == tb - 1)
    score = jnp.where(forced, FORCE_SCORE, jnp.where(blk * SEL_BLK <= t_col, imp, -1.0))
    score = jnp.where(blk < n_blk, score, -2.0)
    score_t = score.T
    sel_t = _topk_mask_t(score_t, n_blk, min(N_SEL, n_blk)) & (score_t > -0.5)
    sel_ref[0, 0] = jnp.where(sel_t, 1.0, 0.0).T


def cmp_select(q, kvc, gates_r, slopes, ov, *, tq):
    b, t, _ = q.shape
    n_seg = kvc.shape[3]
    n_blk = t // SEL_BLK
    grid_spec = pltpu.PrefetchScalarGridSpec(
        num_scalar_prefetch=1, grid=(b, N_KV_HEADS, t // tq),
        in_specs=[
            pl.BlockSpec((1, tq, GROUP * HEAD_DIM), lambda i, k, j, s: (i, j, k)),
            pl.BlockSpec((1, 1, 1, n_seg, HEAD_DIM), lambda i, k, j, s: (i, 0, k, 0, 0)),
            pl.BlockSpec((1, 1, 1, n_seg, HEAD_DIM), lambda i, k, j, s: (i, 1, k, 0, 0)),
            pl.BlockSpec((1, 1, tq, 3 * GROUP), lambda i, k, j, s: (i, k, j, 0)),
            pl.BlockSpec((n_seg, LANES), lambda i, k, j, s: (0, 0)),
        ],
        out_specs=[
            pl.BlockSpec((1, tq, GROUP * HEAD_DIM), lambda i, k, j, s: (i, j, k)),
            pl.BlockSpec((1, 1, tq, LANES), lambda i, k, j, s: (i, k, j, 0)),
        ],
    )
    return pl.pallas_call(
        functools.partial(_cmp_sel_kernel, tq=tq, n_seg=n_seg, n_blk=n_blk),
        grid_spec=grid_spec,
        out_shape=[jax.ShapeDtypeStruct((b, t, ATT_WIDTH), F32), jax.ShapeDtypeStruct((b, N_KV_HEADS, t, LANES), F32)],
        compiler_params=_cparams("parallel", "parallel", "parallel"), name="nsa_cmp_select",
    )(slopes, q, kvc, kvc, gates_r, ov)


def _cmp_to_sel_matrix(n_rows, n_blk):
    start = np.arange(n_rows) * CMP_STRIDE
    blk = np.arange(LANES) * SEL_BLK
    ov = (start[:, None] < blk[None, :] + SEL_BLK) & (start[:, None] + CMP_BLK > blk[None, :]) & (np.arange(LANES)[None, :] < n_blk)
    return jnp.asarray(ov, dtype=BF16)


def _stack_heads(q_blk, extra=None):
    parts = []
    for g in range(GROUP):
        qg = q_blk[:, g * HEAD_DIM:(g + 1) * HEAD_DIM] * Q_SCALE
        if extra is not None:
            qg = jnp.concatenate([qg, extra], axis=1)
        parts.append(qg)
    return jnp.concatenate(parts, axis=0).astype(BF16)


def _head_cols(fn, tq):
    return jnp.concatenate([jnp.full((tq, 1), fn(g), F32) for g in range(GROUP)], axis=0)


def _sel_attn_kernel(slopes_ref, q_ref, sel_ref, ka_ref, v_ref, g_ref, o_ref, *, tq, tk):
    kh = pl.program_id(1)
    t0 = pl.program_id(2) * tq
    mask_feat = (sel_ref[0, 0][:, 0:HEAD_DIM] - 1.0) * 1.0e30
    qa = _stack_heads(q_ref[0], mask_feat)
    slope_col = _head_cols(lambda g: slopes_ref[kh * GROUP + g], tq)
    t_col = t0 + lax.broadcasted_iota(jnp.int32, (tq, 1), 0)
    t_col4 = jnp.concatenate([t_col] * GROUP, axis=0)

    def body(kt, carry):
        m, l, acc = carry
        ks = pl.multiple_of(kt * tk, tk)
        s = lax.dot_general(qa, ka_ref[0, 0, pl.ds(ks, tk), :], NT, preferred_element_type=F32)
        pos = ks + lax.broadcasted_iota(jnp.int32, (1, tk), 1)
        s = s + slope_col * (pos - t0).astype(F32)
        s = jnp.where(pos <= t_col4, s, NEG)
        m_new = jnp.maximum(m, jnp.max(s, axis=-1, keepdims=True))
        alpha = jnp.exp(m - m_new)
        p = jnp.exp(s - m_new)
        l = alpha * l + jnp.sum(p, axis=-1, keepdims=True)
        acc = alpha * acc + jnp.dot(p.astype(BF16), v_ref[0, 0, pl.ds(ks, tk), :], preferred_element_type=F32)
        return m_new, l, acc

    n_kt = (t0 + tq + tk - 1) // tk
    init = (jnp.full((GROUP * tq, 1), M_INIT, F32), jnp.zeros((GROUP * tq, 1), F32), jnp.zeros((GROUP * tq, HEAD_DIM), F32))
    _, l, acc = lax.fori_loop(0, n_kt, body, init)
    o = acc / jnp.maximum(l, 1e-30)
    o_ref[0] = jnp.concatenate(
        [o[g * tq:(g + 1) * tq] * g_ref[0, 0, :, 3 * g + 1:3 * g + 2] for g in range(GROUP)], axis=1)


def sel_attention(q, sel, k_aug, v_sel, gates_r, slopes, *, tq, tk):
    b, t, _ = q.shape
    grid_spec = pltpu.PrefetchScalarGridSpec(
        num_scalar_prefetch=1, grid=(b, N_KV_HEADS, t // tq),
        in_specs=[
            pl.BlockSpec((1, tq, GROUP * HEAD_DIM), lambda i, k, j, s: (i, j, k)),
            pl.BlockSpec((1, 1, tq, LANES), lambda i, k, j, s: (i, k, j, 0)),
            pl.BlockSpec((1, 1, t, LANES), lambda i, k, j, s: (i, k, 0, 0)),
            pl.BlockSpec((1, 1, t, HEAD_DIM), lambda i, k, j, s: (i, k, 0, 0)),
            pl.BlockSpec((1, 1, tq, 3 * GROUP), lambda i, k, j, s: (i, k, j, 0)),
        ],
        out_specs=pl.BlockSpec((1, tq, GROUP * HEAD_DIM), lambda i, k, j, s: (i, j, k)),
    )
    return pl.pallas_call(
        functools.partial(_sel_attn_kernel, tq=tq, tk=tk), grid_spec=grid_spec,
        out_shape=jax.ShapeDtypeStruct((b, t, ATT_WIDTH), F32),
        compiler_params=_cparams("parallel", "parallel", "arbitrary"), name="nsa_sel_attn",
    )(slopes, q, sel, k_aug, v_sel, gates_r)


def _win_attn_kernel(slopes_ref, q_ref, k_ref, v_ref, g_ref, o_ref, *, tq, span):
    kh = pl.program_id(1)
    t0 = pl.program_id(2) * tq
    qs = _stack_heads(q_ref[0])
    slope_col = _head_cols(lambda g: slopes_ref[kh * GROUP + g], tq)
    t_col = t0 + lax.broadcasted_iota(jnp.int32, (tq, 1), 0)
    t_col4 = jnp.concatenate([t_col] * GROUP, axis=0)
    start = pl.multiple_of(jnp.maximum(t0 + tq - span, 0), tq)
    s = lax.dot_general(qs, k_ref[0, 0, pl.ds(start, span), :], NT, preferred_element_type=F32)
    d = t_col4 - (start + lax.broadcasted_iota(jnp.int32, (1, span), 1))
    ok = (d >= 0) & (d <= WINDOW)
    s = jnp.where(ok, s - slope_col * d.astype(F32), NEG)
    m = jnp.max(s, axis=-1, keepdims=True)
    p = jnp.where(ok, jnp.exp(s - m), 0.0)
    l = jnp.sum(p, axis=-1, keepdims=True)
    o = jnp.dot(p.astype(BF16), v_ref[0, 0, pl.ds(start, span), :], preferred_element_type=F32) / jnp.maximum(l, 1e-30)
    o_ref[0] = jnp.concatenate(
        [o[g * tq:(g + 1) * tq] * g_ref[0, 0, :, 3 * g + 2:3 * g + 3] for g in range(GROUP)], axis=1)


def win_attention(q, k_win, v_win, gates_r, slopes, *, tq):
    b, t, _ = q.shape
    span = WINDOW + tq
    assert t >= span and span % tq == 0
    grid_spec = pltpu.PrefetchScalarGridSpec(
        num_scalar_prefetch=1, grid=(b, N_KV_HEADS, t // tq),
        in_specs=[
            pl.BlockSpec((1, tq, GROUP * HEAD_DIM), lambda i, k, j, s: (i, j, k)),
            pl.BlockSpec((1, 1, t, HEAD_DIM), lambda i, k, j, s: (i, k, 0, 0)),
            pl.BlockSpec((1, 1, t, HEAD_DIM), lambda i, k, j, s: (i, k, 0, 0)),
            pl.BlockSpec((1, 1, tq, 3 * GROUP), lambda i, k, j, s: (i, k, j, 0)),
        ],
        out_specs=pl.BlockSpec((1, tq, GROUP * HEAD_DIM), lambda i, k, j, s: (i, j, k)),
    )
    return pl.pallas_call(
        functools.partial(_win_attn_kernel, tq=tq, span=span), grid_spec=grid_spec,
        out_shape=jax.ShapeDtypeStruct((b, t, ATT_WIDTH), F32),
        compiler_params=_cparams("parallel", "parallel", "arbitrary"), name="nsa_win_attn",
    )(slopes, q, k_win, v_win, gates_r)


def _nsa_out_kernel(oc_ref, os_ref, ow_ref, z_ref, x_ref, w_ref, y_ref, *, exact):
    a = (oc_ref[...] + os_ref[...] + ow_ref[...]) * _silu(z_ref[...])
    if exact:
        y_ref[...] = x_ref[...] + jnp.dot(a, w_ref[...], preferred_element_type=F32, precision=HI)
    else:
        y_ref[...] = x_ref[...] + jnp.dot(a.astype(BF16), w_ref[...], preferred_element_type=F32)


def nsa_out(o_c, o_s, o_w, z, x, w_out, *, tm, exact=False):
    m, d = x.shape
    row = pl.BlockSpec((tm, d), lambda i: (i, 0))
    return pl.pallas_call(
        functools.partial(_nsa_out_kernel, exact=exact), grid=(m // tm,),
        in_specs=[row, row, row, row, row, pl.BlockSpec(w_out.shape, lambda i: (0, 0))],
        out_specs=row, out_shape=jax.ShapeDtypeStruct((m, d), F32),
        compiler_params=_cparams("parallel"), name="nsa_out",
    )(o_c, o_s, o_w, z, x, w_out)


LOG2E = math.log2(math.e)
Q_SCALE2 = Q_SCALE * LOG2E
ALIBI_ROWS = 16
BIG = 1.0e30
ONES_ROWS = 16
TQ = 128
TK = 512
WCH = 128


def _alibi_table():
    s = _alibi_slopes() * LOG2E
    s1 = s.astype(BF16).astype(F32)
    s2 = (s - s1).astype(BF16).astype(F32)
    s3 = (s - s1 - s2).astype(BF16).astype(F32)
    tab = jnp.stack([SEL_BLK * s1, SEL_BLK * s2, SEL_BLK * s3, s1, s2, s3, jnp.full_like(s, -BIG), jnp.zeros_like(s)], axis=1)
    return tab.reshape(-1)


def _pos_features(pos, valid, width):
    lane = lax.broadcasted_iota(jnp.int32, (pos.shape[0], width), 1)
    blk = lax.shift_right_logical(pos, int(math.log2(SEL_BLK))).astype(F32)
    rem = (pos & (SEL_BLK - 1)).astype(F32)
    f = jnp.where(lane < 3, blk, jnp.where(lane < 6, rem, 0.0))
    return jnp.where(lane == 6, jnp.where(valid, 0.0, 1.0), f)


def _nsa_in_proj_kernel(x_ref, nw_ref, wt_ref, wn_ref, *rest):
    qt_ref, kvt_ref, wint_ref, gt_ref, zt_ref, kn_ref, kwn_ref = rest[-7:]
    h = _rmsnorm(x_ref[0], nw_ref[...]).astype(BF16)

    def nt(r0, r1):
        return lax.dot_general(wt_ref[r0:r1, :], h, NT, preferred_element_type=F32)

    o1, o2, o3 = ATT_WIDTH, ATT_WIDTH + 4 * KV_WIDTH, ATT_WIDTH + 6 * KV_WIDTH
    o4 = o3 + LANES
    for r0 in range(0, o1, 512):
        qt_ref[0, r0:r0 + 512, :] = (nt(r0, r0 + 512) * Q_SCALE2).astype(BF16)
    for r0 in range(o1, o2, 512):
        kvt_ref[0, 0, r0 - o1:r0 - o1 + 512, :] = nt(r0, r0 + 512)
    wint_ref[0] = nt(o2, o3)
    gt_ref[0] = jax.nn.sigmoid(nt(o3, o4))
    for r0 in range(o4, o4 + ATT_WIDTH, 512):
        zt_ref[0, r0 - o4:r0 - o4 + 512, :] = nt(r0, r0 + 512)
    yn = jnp.dot(h, wn_ref[...], preferred_element_type=F32)
    for j in range(3):
        for hh in range(N_KV_HEADS):
            c0 = (j * N_KV_HEADS + hh) * HEAD_DIM
            kn_ref[0, j, hh] = yn[:, c0:c0 + HEAD_DIM].astype(BF16)
    for hh in range(N_KV_HEADS):
        c0 = (3 * N_KV_HEADS + hh) * HEAD_DIM
        kwn_ref[0, hh] = yn[:, c0:c0 + HEAD_DIM].astype(BF16)


def nsa_in_proj(layer, n_layers, x, nw, w_in, kvt_all, *, tm):
    b, t, d = x.shape
    o1, o2, o3 = ATT_WIDTH, ATT_WIDTH + 4 * KV_WIDTH, ATT_WIDTH + 6 * KV_WIDTH
    o4 = o3 + 3 * N_HEADS
    w_t = w_in.T
    wt = jnp.concatenate([w_t[:o4], jnp.zeros((LANES - 3 * N_HEADS, d), w_in.dtype), w_t[o4:]], axis=0).astype(BF16)
    wn = jnp.concatenate([w_in[:, o1:o1 + 3 * KV_WIDTH], w_in[:, o2:o2 + KV_WIDTH]], axis=1).astype(BF16)
    tok = lambda rows: pl.BlockSpec((1, rows, tm), lambda i, j: (i, 0, j))
    full = lambda a: pl.BlockSpec(a.shape, lambda i, j: (0,) * a.ndim)
    tshape = lambda rows, dt: jax.ShapeDtypeStruct((b, rows, t), dt)
    args = [x, nw.reshape(1, d), wt, wn]
    in_specs = [pl.BlockSpec((1, tm, d), lambda i, j: (i, j, 0)), pl.BlockSpec((1, d), lambda i, j: (0, 0)), full(wt), full(wn)]
    aliases = {}
    if kvt_all is not None:
        args.append(kvt_all)
        in_specs.append(pl.BlockSpec(memory_space=pl.ANY))
        aliases = {len(args) - 1: 1}
    return pl.pallas_call(
        _nsa_in_proj_kernel, grid=(b, t // tm), in_specs=in_specs,
        out_specs=[tok(ATT_WIDTH), pl.BlockSpec((1, 1, 4 * KV_WIDTH, tm), lambda i, j: (layer, i, 0, j)),
                   tok(2 * KV_WIDTH), tok(LANES), tok(ATT_WIDTH),
                   pl.BlockSpec((1, 3, N_KV_HEADS, tm, HEAD_DIM), lambda i, j: (i, 0, 0, j, 0)),
                   pl.BlockSpec((1, N_KV_HEADS, tm, HEAD_DIM), lambda i, j: (i, 0, j, 0))],
        out_shape=[tshape(ATT_WIDTH, BF16), jax.ShapeDtypeStruct((n_layers, b, 4 * KV_WIDTH, t), F32),
                   tshape(2 * KV_WIDTH, F32), tshape(LANES, F32),
                   tshape(ATT_WIDTH, F32), jax.ShapeDtypeStruct((b, 3, N_KV_HEADS, t, HEAD_DIM), BF16),
                   jax.ShapeDtypeStruct((b, N_KV_HEADS, t, HEAD_DIM), BF16)],
        input_output_aliases=aliases,
        compiler_params=_cparams("parallel", "parallel"), name="nsa_in_proj",
    )(*args)


def _topk_mask_rows(score, k):
    r = score.shape[0]
    groups = [score[8 * v:8 * v + 8] for v in range(r // 8)]
    sub = lax.broadcasted_iota(jnp.int32, (8, 1), 0)
    cnt = [jnp.zeros(g.shape, F32) for g in groups]
    for i in range(r):
        row = score[i:i + 1, :]
        for v, g in enumerate(groups):
            if 8 * v > i:
                beats = row >= g
            elif 8 * v + 7 < i:
                beats = row > g
            else:
                beats = (row > g) | ((sub > i - 8 * v) & (row == g))
            cnt[v] = cnt[v] + jnp.where(beats, 1.0, 0.0)
    return jnp.concatenate(cnt, axis=0) < float(k)


def _nsa_attn_kernel(tab_ref, qt_ref, gt_ref, kc_ref, vc_ref, ksel_ref, kwin_ref, vselt_ref, vwint_ref, ovt_ref, ot_ref,
                     ksa_s, kwa_s, kca_s, vst_s, vwt_s, vct_s, sc_s, m_s, acc_s, *, t_len, n_seg, n_blk):
    kh = pl.program_id(1)
    qi = pl.program_id(2)
    t0 = qi * TQ
    cols = GROUP * TQ
    ones = lambda n: jnp.ones((ONES_ROWS, n), BF16)

    @pl.when(qi == 0)
    def _build_keys():
        pos = lax.broadcasted_iota(jnp.int32, (t_len, 1), 0)
        onehot = jnp.where(lax.shift_right_logical(pos, int(math.log2(SEL_BLK)))
                           == lax.broadcasted_iota(jnp.int32, (1, HEAD_DIM), 1), 1.0, 0.0)
        ksa_s[:, 0:LANES] = jnp.concatenate([ksel_ref[0, 0, 0].astype(F32), onehot], axis=1).astype(BF16)
        ksa_s[:, LANES:2 * LANES] = _pos_features(pos, pos >= 0, LANES).astype(BF16)
        no_key = _pos_features(jnp.zeros((WINDOW, 1), jnp.int32), jnp.zeros((WINDOW, 1), jnp.bool_), HEAD_DIM)
        kwa_s[0:WINDOW, :] = jnp.concatenate([jnp.zeros((WINDOW, HEAD_DIM), F32), no_key], axis=1).astype(BF16)
        kwa_s[WINDOW:WINDOW + t_len, :] = jnp.concatenate(
            [kwin_ref[0, 0].astype(F32), _pos_features(pos, pos >= 0, HEAD_DIM)], axis=1).astype(BF16)
        cend = lax.broadcasted_iota(jnp.int32, (n_seg, 1), 0) * CMP_STRIDE + (CMP_BLK - 1)
        kca_s[...] = jnp.concatenate([kc_ref[0, 0, 0], _pos_features(cend, cend >= 0, HEAD_DIM)], axis=1).astype(BF16)
        vc_pad = jnp.concatenate([vc_ref[0, 0, 0], jnp.zeros((n_seg, LANES - HEAD_DIM), F32)], axis=1)
        vct_s[...] = jnp.concatenate([vc_pad.T[0:HEAD_DIM].astype(BF16), ones(n_seg)], axis=0)
        for c in range(t_len // TK):
            vst_s[c] = jnp.concatenate([vselt_ref[0, 0, :, c * TK:(c + 1) * TK].astype(BF16), ones(TK)], axis=0)
        for c in range(WINDOW // WCH):
            vwt_s[c] = jnp.zeros((HEAD_DIM + ONES_ROWS, WCH), BF16)
        for c in range(t_len // WCH):
            vwt_s[WINDOW // WCH + c] = jnp.concatenate([vwint_ref[0, :, c * WCH:(c + 1) * WCH].astype(BF16), ones(WCH)], axis=0)

    qb = qt_ref[0]
    qw = jnp.concatenate([qb[g * HEAD_DIM:(g + 1) * HEAD_DIM, :] for g in range(GROUP)], axis=1)
    frow = lax.broadcasted_iota(jnp.int32, (ALIBI_ROWS, cols), 0)
    fhead = lax.broadcasted_iota(jnp.int32, (ALIBI_ROWS, cols), 1) // TQ
    feat = jnp.zeros((ALIBI_ROWS, cols), F32)
    for g in range(GROUP):
        for r in range(7):
            feat = jnp.where((frow == r) & (fhead == g), tab_ref[(kh * GROUP + g) * 8 + r], feat)
    feat = feat.astype(BF16)
    q_base = jnp.concatenate([qw, feat, jnp.zeros((LANES - HEAD_DIM - ALIBI_ROWS, cols), BF16)], axis=0)
    t_row = t0 + lax.broadcasted_iota(jnp.int32, (1, cols), 1) % TQ

    def finish(acc):
        return acc[0:HEAD_DIM] * (1.0 / jnp.maximum(acc[HEAD_DIM:HEAD_DIM + 1], 1e-30))

    cend = lax.broadcasted_iota(jnp.int32, (n_seg, 1), 0) * CMP_STRIDE + (CMP_BLK - 1)
    ok_c = cend <= t_row
    s_c = jnp.where(ok_c, jnp.dot(kca_s[...], q_base, preferred_element_type=F32), NEG)
    m_c = jnp.max(s_c, axis=0, keepdims=True)
    p_c = jnp.where(ok_c, jnp.exp2(s_c - m_c), 0.0).astype(BF16)
    acc_c = jnp.dot(vct_s[...], p_c, preferred_element_type=F32)
    inv_c = 1.0 / jnp.maximum(acc_c[HEAD_DIM:HEAD_DIM + 1], 1e-30)
    o_cmp = acc_c[0:HEAD_DIM] * inv_c
    imp_c = jnp.dot(ovt_ref[...], p_c, preferred_element_type=F32) * inv_c
    imp = imp_c[:, 0:TQ]
    for g in range(1, GROUP):
        imp = imp + imp_c[:, g * TQ:(g + 1) * TQ]
    blk = lax.broadcasted_iota(jnp.int32, (HEAD_DIM, 1), 0)
    tq_row = t_row[:, 0:TQ]
    tb = lax.shift_right_logical(tq_row, int(math.log2(SEL_BLK)))
    forced = (blk == 0) | (blk == tb) | (blk == tb - 1)
    score = jnp.where(forced, FORCE_SCORE, jnp.where(blk * SEL_BLK <= tq_row, imp, -1.0))
    score = jnp.where(blk < n_blk, score, -2.0)
    chosen = _topk_mask_rows(score, min(N_SEL, n_blk)) & (score > -0.5)

    span = WINDOW + TQ
    k_w = kwa_s[pl.ds(pl.multiple_of(t0, TQ), span), :]
    v_w = jnp.concatenate([vwt_s[qi * (TQ // WCH) + c] for c in range(span // WCH)], axis=1)
    jj = lax.broadcasted_iota(jnp.int32, (TQ, 1), 0)
    ii = lax.broadcasted_iota(jnp.int32, (1, cols), 1) % TQ
    s_w = jnp.dot(k_w, q_base, preferred_element_type=F32)
    s_w = jnp.concatenate([jnp.where(jj >= ii, s_w[0:TQ], NEG), s_w[TQ:WINDOW],
                           jnp.where(jj <= ii, s_w[WINDOW:span], NEG)], axis=0)
    m_w = jnp.maximum(jnp.max(s_w, axis=0, keepdims=True), M_INIT)
    p_w = jnp.exp2((s_w - m_w).astype(BF16))
    o_win = finish(jnp.dot(v_w, p_w, preferred_element_type=F32))

    mrow = jnp.concatenate([jnp.where(chosen, 0.0, -BIG).astype(BF16)] * GROUP, axis=1)
    q_sel = jnp.concatenate([qw, mrow, feat, jnp.zeros((LANES - ALIBI_ROWS, cols), BF16)], axis=0)
    n_pairs = (t0 + TQ - 1) // (2 * TK) + 1
    max_pairs = t_len // (2 * TK)
    key_off = lax.broadcasted_iota(jnp.int32, (TK, 1), 0)

    def issue(j):
        for u in range(2):
            ks = (2 * j + u) * TK
            s = jnp.dot(ksa_s[ks:ks + TK, :], q_sel, preferred_element_type=F32)
            sc_s[j % 2, u] = jnp.where(ks + key_off <= t_row, s, NEG)

    def absorb(j):
        s0, s1 = sc_s[j % 2, 0], sc_s[j % 2, 1]
        m = m_s[0:1, :]
        m_new = jnp.maximum(m, jnp.maximum(jnp.max(s0, axis=0, keepdims=True), jnp.max(s1, axis=0, keepdims=True)))
        acc = jnp.exp2(m - m_new) * acc_s[...]
        for u, s in enumerate((s0, s1)):
            acc = acc + jnp.dot(vst_s[2 * j + u], jnp.exp2((s - m_new).astype(BF16)), preferred_element_type=F32)
        m_s[0:1, :] = m_new
        acc_s[...] = acc

    m_s[0:1, :] = jnp.full((1, cols), M_INIT, F32)
    acc_s[...] = jnp.zeros(acc_s.shape, F32)
    issue(0)
    for j in range(max_pairs):
        @pl.when(j < n_pairs)
        def _pair(j=j):
            if j + 1 < max_pairs:
                issue(j + 1)
            absorb(j)
    o_sel = finish(acc_s[...])

    for g in range(GROUP):
        sl = slice(g * TQ, (g + 1) * TQ)
        gate = lambda j: gt_ref[0, pl.ds(kh * 3 * GROUP + 3 * g + j, 1), :]
        ot_ref[0, g * HEAD_DIM:(g + 1) * HEAD_DIM, :] = gate(0) * o_cmp[:, sl] + gate(1) * o_sel[:, sl] + gate(2) * o_win[:, sl]


def nsa_attention(layer, qt, gt, kvc, kn, kwn, kvt, wint):
    b, _, t = qt.shape
    n_seg = kvc.shape[3]
    n_blk = t // SEL_BLK
    assert t % (2 * TK) == 0 and n_blk <= HEAD_DIM and t >= WINDOW + TQ
    start = np.arange(n_seg) * CMP_STRIDE
    blk = np.arange(HEAD_DIM) * SEL_BLK
    ovt = jnp.asarray(((start[None, :] < blk[:, None] + SEL_BLK) & (start[None, :] + CMP_BLK > blk[:, None])
                       & (np.arange(HEAD_DIM)[:, None] < n_blk)), dtype=BF16)
    vrows = HEAD_DIM + ONES_ROWS
    grid_spec = pltpu.PrefetchScalarGridSpec(
        num_scalar_prefetch=1, grid=(b, N_KV_HEADS, t // TQ),
        in_specs=[
            pl.BlockSpec((1, GROUP * HEAD_DIM, TQ), lambda i, k, j, s: (i, k, j)),
            pl.BlockSpec((1, LANES, TQ), lambda i, k, j, s: (i, 0, j)),
            pl.BlockSpec((1, 1, 1, n_seg, HEAD_DIM), lambda i, k, j, s: (i, 0, k, 0, 0)),
            pl.BlockSpec((1, 1, 1, n_seg, HEAD_DIM), lambda i, k, j, s: (i, 1, k, 0, 0)),
            pl.BlockSpec((1, 1, 1, t, HEAD_DIM), lambda i, k, j, s: (i, 2, k, 0, 0)),
            pl.BlockSpec((1, 1, t, HEAD_DIM), lambda i, k, j, s: (i, k, 0, 0)),
            pl.BlockSpec((1, 1, HEAD_DIM, t), lambda i, k, j, s: (layer, i, 3 * N_KV_HEADS + k, 0)),
            pl.BlockSpec((1, HEAD_DIM, t), lambda i, k, j, s: (i, N_KV_HEADS + k, 0)),
            pl.BlockSpec((HEAD_DIM, n_seg), lambda i, k, j, s: (0, 0)),
        ],
        out_specs=pl.BlockSpec((1, GROUP * HEAD_DIM, TQ), lambda i, k, j, s: (i, k, j)),
        scratch_shapes=[pltpu.VMEM((t, 2 * LANES), BF16), pltpu.VMEM((WINDOW + t, LANES), BF16), pltpu.VMEM((n_seg, LANES), BF16),
                        pltpu.VMEM((t // TK, vrows, TK), BF16), pltpu.VMEM(((WINDOW + t) // WCH, vrows, WCH), BF16),
                        pltpu.VMEM((vrows, n_seg), BF16),
                        pltpu.VMEM((2, 2, TK, GROUP * TQ), F32), pltpu.VMEM((8, GROUP * TQ), F32),
                        pltpu.VMEM((vrows, GROUP * TQ), F32)])
    return pl.pallas_call(
        functools.partial(_nsa_attn_kernel, t_len=t, n_seg=n_seg, n_blk=n_blk), grid_spec=grid_spec,
        out_shape=jax.ShapeDtypeStruct((b, ATT_WIDTH, t), F32),
        compiler_params=_cparams("parallel", "parallel", "arbitrary"), name="nsa_attn",
    )(_alibi_table(), qt, gt, kvc, kvc, kn, kwn, kvt, wint, ovt)


def _nsa_out_t_kernel(ot_ref, zt_ref, x_ref, w_ref, y_ref):
    a_t = (ot_ref[0] * _silu(zt_ref[0])).astype(BF16)
    y_ref[0] = x_ref[0] + lax.dot_general(a_t, w_ref[...], TN, preferred_element_type=F32)


def nsa_out_t(ot, zt, x, w_out, *, tm):
    b, t, d = x.shape
    tok = pl.BlockSpec((1, ATT_WIDTH, tm), lambda i, j: (i, 0, j))
    row = pl.BlockSpec((1, tm, d), lambda i, j: (i, j, 0))
    return pl.pallas_call(
        _nsa_out_t_kernel, grid=(b, t // tm),
        in_specs=[tok, tok, row, pl.BlockSpec(w_out.shape, lambda i, j: (0, 0))],
        out_specs=row, out_shape=jax.ShapeDtypeStruct((b, t, d), F32),
        compiler_params=_cparams("parallel", "parallel"), name="nsa_out",
    )(ot, zt, x, w_out)


def nsa_prompt_layer_t(layer, n_layers, x, kvt_all, nw, w_in, w_out, pe, w1, b1, w2):
    b, t, d = x.shape
    qt, kvt_all, wint, gt, zt, kn, kwn = nsa_in_proj(layer, n_layers, x, nw, w_in, kvt_all, tm=256)
    n_seg = t // CMP_STRIDE
    kvc = compress_tokens(kn[:, 0:2].reshape(b, 2, N_KV_HEADS, n_seg, CMP_STRIDE * HEAD_DIM), pe, w1, b1, w2)
    ot = nsa_attention(layer, qt, gt, kvc, kn, kwn, kvt_all, wint)
    y = nsa_out_t(ot, zt, x, w_out.astype(BF16), tm=256)
    wr = min(WINDOW, t)
    win5 = wint[:, :, t - wr:].reshape(b, 2, N_KV_HEADS, HEAD_DIM, wr).transpose(0, 4, 1, 2, 3)
    return y, kvt_all, win5


def _split_nsa_w_in(w_in, dtype):
    o1 = ATT_WIDTH
    o2 = o1 + 4 * KV_WIDTH
    o3 = o2 + 2 * KV_WIDTH
    o4 = o3 + 3 * N_HEADS
    wg = jnp.pad(w_in[:, o3:o4], ((0, 0), (0, LANES - 3 * N_HEADS)))
    return [w.astype(dtype) for w in (w_in[:, :o1], w_in[:, o1:o2], w_in[:, o2:o3], wg, w_in[:, o4:])]


NSA_ACTS = (None, None, None, "sigmoid", None)


def nsa_prompt_layer(x, nw, w_in, w_out, pe, w1, b1, w2):
    b, t, d = x.shape
    n_blk = t // SEL_BLK
    assert n_blk <= HEAD_DIM
    q, kv, win, gates, z = rms_proj(x.reshape(b * t, d), nw, _split_nsa_w_in(w_in, BF16), NSA_ACTS, tm=256, name="nsa_in_proj")
    q = q.reshape(b, t, ATT_WIDTH)
    kv5 = kv.reshape(b, t, 4, N_KV_HEADS, HEAD_DIM)
    win5 = win.reshape(b, t, 2, N_KV_HEADS, HEAD_DIM)
    kvt = kv5.transpose(0, 2, 3, 1, 4).astype(BF16)
    wint = win5.transpose(0, 2, 3, 1, 4).astype(BF16)
    gates_r = gates[:, :3 * N_HEADS].reshape(b, t, N_KV_HEADS, 3 * GROUP).transpose(0, 2, 1, 3)
    slopes = _alibi_slopes()
    n_seg = t // CMP_STRIDE
    kvc = compress_tokens(kvt[:, 0:2].reshape(b, 2, N_KV_HEADS, n_seg, CMP_STRIDE * HEAD_DIM), pe, w1, b1, w2)
    o_c, sel = cmp_select(q, kvc, gates_r, slopes, _cmp_to_sel_matrix(n_seg, n_blk), tq=128)
    onehot = (jnp.arange(t)[:, None] // SEL_BLK == jnp.arange(HEAD_DIM)[None, :]).astype(BF16)
    k_aug = jnp.concatenate([kvt[:, 2], jnp.broadcast_to(onehot, (b, N_KV_HEADS, t, HEAD_DIM))], axis=-1)
    o_s = sel_attention(q, sel, k_aug, kvt[:, 3], gates_r, slopes, tq=128, tk=512)
    o_w = win_attention(q, wint[:, 0], wint[:, 1], gates_r, slopes, tq=128)
    r2 = lambda a: a.reshape(b * t, -1)
    y = nsa_out(r2(o_c), r2(o_s), r2(o_w), z, r2(x), w_out.astype(BF16), tm=256)
    wr = min(WINDOW, t)
    return y.reshape(b, t, d), kv5, win5[:, t - wr:]


def _log_sigmoid(x):
    return jnp.minimum(x, 0.0) - jnp.log(1.0 + jnp.exp(-jnp.abs(x)))


def _mconv_body(shifted, xm, cw_ref, cb_ref, wbd_ref, wg_ref, bg_ref, q_ref, k_ref, v_ref, c_ref, g_ref, exact):
    conv = cb_ref[...]
    for j in range(CONV_W):
        conv = conv + shifted[j] * cw_ref[j:j + 1, :]
    c = _silu(conv)
    c_ref[...] = c
    cast = (lambda a: a) if exact else (lambda a: a.astype(BF16))
    kw = dict(preferred_element_type=F32, precision=HI) if exact else dict(preferred_element_type=F32)
    gpre = bg_ref[...]
    for m, (src, dst) in enumerate(((c, q_ref), (c, k_ref), (xm, v_ref))):
        for gi in range(D_INNER // LANES):
            sl = slice(gi * LANES, (gi + 1) * LANES)
            y = jnp.dot(cast(src[:, sl]), wbd_ref[m, gi], **kw)
            gpre = gpre + jnp.dot(cast(y), wg_ref[m * D_INNER + gi * LANES:m * D_INNER + (gi + 1) * LANES, :], **kw)
            dst[:, sl] = y * (M_HEAD_DIM ** -0.5) if m == 1 else y
    lane = lax.broadcasted_iota(jnp.int32, gpre.shape, 1)
    g_ref[...] = jnp.where(lane < M_HEADS, gpre, _log_sigmoid(gpre))


def _mconv_prompt_kernel(xm_ref, halo_ref, cw_ref, cb_ref, wbd_ref, wg_ref, bg_ref, q_ref, k_ref, v_ref, c_ref, g_ref, *, tm):
    xm = xm_ref[0]
    halo = jnp.where(pl.program_id(1) == 0, 0.0, halo_ref[0])
    ext = jnp.concatenate([halo, xm], axis=0)
    shifted = [ext[5 + j:5 + j + tm] for j in range(CONV_W - 1)] + [xm]
    _mconv_body(shifted, xm, cw_ref, cb_ref, wbd_ref, wg_ref, bg_ref, q_ref.at[0], k_ref.at[0], v_ref.at[0], c_ref.at[0],
                g_ref.at[0], False)


def _mlstm_small_weights(w_qkv, w_gate, b_gate, dtype):
    nb = LANES // QKV_BLK
    w = w_qkv.reshape(3, D_INNER // LANES, nb, QKV_BLK, QKV_BLK)
    eye = jnp.eye(nb, dtype=w.dtype)
    wbd = jnp.einsum("mgnji,nk->mgnjki", w, eye).reshape(3, D_INNER // LANES, LANES, LANES)
    wg = jnp.pad(w_gate, ((0, 0), (0, LANES - 2 * M_HEADS)))
    bg = jnp.pad(b_gate, (0, LANES - 2 * M_HEADS)).reshape(1, LANES)
    return wbd.astype(dtype), wg.astype(dtype), bg


def mconv_prompt(xm, conv_w, conv_b, w_qkv, w_gate, b_gate, *, tm):
    b, t, _ = xm.shape
    wbd, wg, bg = _mlstm_small_weights(w_qkv, w_gate, b_gate, BF16)
    row = pl.BlockSpec((1, tm, D_INNER), lambda i, j: (i, j, 0))
    full = lambda a: pl.BlockSpec(a.shape, lambda i, j: (0,) * a.ndim)
    cb = conv_b.reshape(1, D_INNER)
    return pl.pallas_call(
        functools.partial(_mconv_prompt_kernel, tm=tm), grid=(b, t // tm),
        in_specs=[row, pl.BlockSpec((1, 8, D_INNER), lambda i, j: (i, jnp.maximum(j * (tm // 8) - 1, 0), 0)),
                  full(conv_w), full(cb), full(wbd), full(wg), full(bg)],
        out_specs=[row, row, row, row, pl.BlockSpec((1, tm, LANES), lambda i, j: (i, j, 0))],
        out_shape=[jax.ShapeDtypeStruct((b, t, D_INNER), F32)] * 4 + [jax.ShapeDtypeStruct((b, t, LANES), F32)],
        compiler_params=_cparams("parallel", "parallel"), name="mlstm_conv_qkv",
    )(xm, xm, conv_w, cb, wbd, wg, bg)


def _mlstm_cell_kernel(q_ref, k_ref, v_ref, gc_ref, gr_ref, c0_ref, n0_ref, m0_ref, h_ref, cf_ref, nf_ref, mf_ref,
                       c_s, n_s, m_s, *, chunk, n_chunks):
    ci = pl.program_id(2)

    @pl.when(ci == 0)
    def _():
        c_s[...] = c0_ref[0, 0]
        n_s[...] = n0_ref[0, 0]
        m_s[...] = m0_ref[0, 0]

    q, k, v = q_ref[0], k_ref[0], v_ref[0]
    icol, fcol = gc_ref[0, 0, :, 0:1], gc_ref[0, 0, :, 1:2]
    irow, frow = gr_ref[0, 0, 0:1, :], gr_ref[0, 0, 1:2, :]
    ri = lax.broadcasted_iota(jnp.int32, (chunk, chunk), 0)
    cj = lax.broadcasted_iota(jnp.int32, (chunk, chunk), 1)
    causal = cj <= ri
    b_col = jnp.dot(jnp.where(causal, 1.0, 0.0), jnp.broadcast_to(fcol, (chunk, LANES)),
                    preferred_element_type=F32, precision=HI)[:, 0:1]
    b_row = jnp.dot(jnp.broadcast_to(frow, (8, chunk)), jnp.where(ri <= cj, 1.0, 0.0),
                    preferred_element_type=F32, precision=HI)[0:1, :]
    dmat = jnp.where(causal, b_col - b_row + irow, -jnp.inf)
    m_prev = m_s[0:1, 0:1]
    inter = b_col + m_prev
    mt = jnp.maximum(jnp.max(dmat, axis=-1, keepdims=True), inter)
    qb, kb, vb = q.astype(BF16), k.astype(BF16), v.astype(BF16)
    s = lax.dot_general(qb, kb, NT, preferred_element_type=F32) * jnp.exp(dmat - mt)
    decay = jnp.exp(inter - mt)
    num = (jnp.dot(s.astype(BF16), vb, preferred_element_type=F32)
           + decay * jnp.dot(qb, c_s[...].astype(BF16), preferred_element_type=F32))
    den = jnp.sum(s, axis=-1, keepdims=True) + decay * jnp.sum(q * n_s[...], axis=-1, keepdims=True)
    hc = num / jnp.maximum(jnp.abs(den), jnp.exp(-mt))
    h_ref[0] = hc * lax.rsqrt(jnp.mean(hc * hc, axis=-1, keepdims=True) + RMS_EPS)
    m_new = mt[chunk - 1:chunk, :]
    b_last = b_col[chunk - 1:chunk, :]
    kw = k * jnp.exp(b_last - b_col + icol - m_new)
    carry = jnp.exp(b_last + m_prev - m_new)
    c_s[...] = carry * c_s[...] + lax.dot_general(kw.astype(BF16), vb, TN, preferred_element_type=F32)
    n_s[...] = carry * n_s[...] + jnp.sum(kw, axis=0, keepdims=True)
    m_s[...] = jnp.broadcast_to(m_new, m_s.shape)

    @pl.when(ci == n_chunks - 1)
    def _():
        cf_ref[0, 0] = c_s[...]
        nf_ref[0, 0] = n_s[...]
        mf_ref[0, 0] = m_s[...]


def mlstm_cell(q, k, v, gates, c0, n0, m0, *, chunk):
    b, t, _ = q.shape
    n_chunks = t // chunk
    g_col = jnp.stack([gates[..., :M_HEADS], gates[..., M_HEADS:2 * M_HEADS]], axis=-1).transpose(0, 2, 1, 3)
    g_row = g_col.transpose(0, 1, 3, 2)
    m0b = jnp.broadcast_to(m0[:, :, None, None], (b, M_HEADS, 8, LANES))
    n0r = n0.reshape(b, M_HEADS, 1, M_HEAD_DIM)
    head = pl.BlockSpec((1, chunk, M_HEAD_DIM), lambda i, h, c: (i, c, h))
    st = lambda *blk: pl.BlockSpec((1, 1) + blk, lambda i, h, c: (i, h, 0, 0))
    hn, cf, nf, mf = pl.pallas_call(
        functools.partial(_mlstm_cell_kernel, chunk=chunk, n_chunks=n_chunks), grid=(b, M_HEADS, n_chunks),
        in_specs=[head, head, head,
                  pl.BlockSpec((1, 1, chunk, 2), lambda i, h, c: (i, h, c, 0)),
                  pl.BlockSpec((1, 1, 2, chunk), lambda i, h, c: (i, h, 0, c)),
                  st(M_HEAD_DIM, M_HEAD_DIM), st(1, M_HEAD_DIM), st(8, LANES)],
        out_specs=[head, st(M_HEAD_DIM, M_HEAD_DIM), st(1, M_HEAD_DIM), st(8, LANES)],
        out_shape=[jax.ShapeDtypeStruct((b, t, D_INNER), F32), jax.ShapeDtypeStruct((b, M_HEADS, M_HEAD_DIM, M_HEAD_DIM), F32),
                   jax.ShapeDtypeStruct((b, M_HEADS, 1, M_HEAD_DIM), F32), jax.ShapeDtypeStruct((b, M_HEADS, 8, LANES), F32)],
        scratch_shapes=[pltpu.VMEM((M_HEAD_DIM, M_HEAD_DIM), F32), pltpu.VMEM((1, M_HEAD_DIM), F32), pltpu.VMEM((8, LANES), F32)],
        compiler_params=_cparams("parallel", "parallel", "arbitrary"), name="mlstm_cell",
    )(q, k, v, g_col, g_row, c0, n0r, m0b)
    return hn, cf, nf.reshape(b, M_HEADS, M_HEAD_DIM), mf[:, :, 0, 0]


def _mlstm_out_kernel(hn_ref, c_ref, z_ref, x_ref, nw_ref, sk_ref, w_ref, fw_ref, y_ref, *, final, exact):
    a = (hn_ref[...] * nw_ref[...] + sk_ref[...] * c_ref[...]) * _silu(z_ref[...])
    if exact:
        y = x_ref[...] + jnp.dot(a, w_ref[...], preferred_element_type=F32, precision=HI)
    else:
        y = x_ref[...] + jnp.dot(a.astype(BF16), w_ref[...], preferred_element_type=F32)
    y_ref[...] = _rmsnorm(y, fw_ref[...]) if final else y


def mlstm_out(hn, c, z, x, norm_w, skip, w_out, final_w, *, tm, final, exact=False):
    m, d = x.shape
    wide = pl.BlockSpec((tm, D_INNER), lambda i: (i, 0))
    row = pl.BlockSpec((tm, d), lambda i: (i, 0))
    vec = lambda n: pl.BlockSpec((1, n), lambda i: (0, 0))
    return pl.pallas_call(
        functools.partial(_mlstm_out_kernel, final=final, exact=exact), grid=(m // tm,),
        in_specs=[wide, wide, wide, row, vec(D_INNER), vec(D_INNER), pl.BlockSpec(w_out.shape, lambda i: (0, 0)), vec(d)],
        out_specs=row, out_shape=jax.ShapeDtypeStruct((m, d), F32),
        compiler_params=_cparams("parallel"), name="mlstm_out",
    )(hn, c, z, x, norm_w.reshape(1, D_INNER), skip.reshape(1, D_INNER), w_out, final_w.reshape(1, d))


def mlstm_prompt_layer(x, nw, w_in, conv_w, conv_b, w_qkv, w_gate, b_gate, norm_w, skip, w_out, final_w, *, final):
    b, t, d = x.shape
    w_in = w_in.astype(BF16)
    xm, z = rms_proj(x.reshape(b * t, d), nw, [w_in[:, :D_INNER], w_in[:, D_INNER:]], (None, None), tm=256, name="mlstm_in_proj")
    xm3 = xm.reshape(b, t, D_INNER)
    q, k, v, c, gates = mconv_prompt(xm3, conv_w, conv_b, w_qkv, w_gate, b_gate, tm=256)
    c0 = jnp.zeros((b, M_HEADS, M_HEAD_DIM, M_HEAD_DIM), F32)
    n0 = jnp.zeros((b, M_HEADS, M_HEAD_DIM), F32)
    m0 = jnp.full((b, M_HEADS), -jnp.inf, F32)
    hn, cf, nf, mf = mlstm_cell(q, k, v, gates, c0, n0, m0, chunk=min(256, t))
    y = mlstm_out(hn.reshape(b * t, D_INNER), c.reshape(b * t, D_INNER), z, x.reshape(b * t, d), norm_w, skip,
                  w_out.astype(BF16), final_w, tm=256, final=final)
    return y.reshape(b, t, d), cf, nf, mf, xm3[:, t - (CONV_W - 1):]


def _mconv_sample_kernel(xm_ref, hist_ref, cw_ref, cb_ref, wbd_ref, wg_ref, bg_ref, q_ref, k_ref, v_ref, c_ref, g_ref):
    xm = xm_ref[...]
    shifted = [hist_ref[j] for j in range(CONV_W - 1)] + [xm]
    _mconv_body(shifted, xm, cw_ref, cb_ref, wbd_ref, wg_ref, bg_ref, q_ref, k_ref, v_ref, c_ref, g_ref, True)


def mconv_sample(xm, hist, conv_w, conv_b, w_qkv, w_gate, b_gate):
    b = xm.shape[0]
    wbd, wg, bg = _mlstm_small_weights(w_qkv, w_gate, b_gate, F32)
    cb = conv_b.reshape(1, D_INNER)
    full = lambda a: pl.BlockSpec(a.shape, lambda i: (0,) * a.ndim)
    args = (xm, hist, conv_w, cb, wbd, wg, bg)
    row = pl.BlockSpec((b, D_INNER), lambda i: (0, 0))
    return pl.pallas_call(
        _mconv_sample_kernel, grid=(1,), in_specs=[full(a) for a in args],
        out_specs=[row, row, row, row, pl.BlockSpec((b, LANES), lambda i: (0, 0))],
        out_shape=[jax.ShapeDtypeStruct((b, D_INNER), F32)] * 4 + [jax.ShapeDtypeStruct((b, LANES), F32)],
        compiler_params=_cparams("arbitrary"), name="mlstm_conv_qkv_sample",
    )(*args)


def _mlstm_step_kernel(q_ref, k_ref, v_ref, g_ref, c0_ref, n0_ref, *rest):
    h_ref, cf_ref, nf_ref, mf_ref = rest[-4:]
    q, k, v = q_ref[0], k_ref[0], v_ref[0]
    g = g_ref[0, 0]
    ig, fl, m0 = g[:, 0:1], g[:, 1:2], g[:, 2:3]
    c0, n0 = c0_ref[0, 0, 0], n0_ref[0, 0]
    m_new = jnp.maximum(fl + m0, ig)
    decay = jnp.exp(fl + m0 - m_new)
    sw = jnp.exp(ig - m_new)
    s = jnp.sum(q * k, axis=-1, keepdims=True) * sw
    half = LANES // 2
    qk_col = jnp.concatenate([jnp.broadcast_to(q, (half, M_HEAD_DIM)), jnp.broadcast_to(k, (half, M_HEAD_DIM))], axis=0).T
    q_col, k_col = qk_col[:, 0:1], qk_col[:, half:half + 1]
    qc = jnp.sum(q_col * c0, axis=0, keepdims=True)
    num = s * v + decay * qc
    den = s + decay * jnp.sum(q * n0, axis=-1, keepdims=True)
    hc = num / jnp.maximum(jnp.abs(den), jnp.exp(-m_new))
    h_ref[0] = hc * lax.rsqrt(jnp.mean(hc * hc, axis=-1, keepdims=True) + RMS_EPS)
    cf_ref[0, 0, 0] = decay * c0 + (k_col * sw) * v
    nf_ref[0, 0] = decay * n0 + sw * k
    mf_ref[0, 0] = jnp.broadcast_to(m_new, (1, LANES))


def mlstm_step(layer, q, k, v, gates, c_all, n0, m0, c_new_all):
    b = q.shape[0]
    gsm = jnp.stack([gates[:, :M_HEADS], gates[:, M_HEADS:2 * M_HEADS], m0], axis=-1)
    gsm = jnp.pad(gsm, ((0, 0), (0, 0), (0, LANES - 3))).reshape(b, M_HEADS, 1, LANES)
    r3 = lambda a: a.reshape(b, 1, D_INNER)
    head = pl.BlockSpec((1, 1, M_HEAD_DIM), lambda i, h: (i, 0, h))
    st = lambda *blk: pl.BlockSpec((1, 1) + blk, lambda i, h: (i, h, 0, 0))
    c_blk = pl.BlockSpec((1, 1, 1, M_HEAD_DIM, M_HEAD_DIM), lambda i, h: (layer, i, h, 0, 0))
    args = [r3(q), r3(k), r3(v), gsm, c_all, n0.reshape(b, M_HEADS, 1, M_HEAD_DIM)]
    in_specs = [head, head, head, st(1, LANES), c_blk, st(1, M_HEAD_DIM)]
    aliases = {}
    if c_new_all is not None:
        args.append(c_new_all)
        in_specs.append(pl.BlockSpec(memory_space=pl.ANY))
        aliases = {len(args) - 1: 1}
    hn, cf, nf, mf = pl.pallas_call(
        _mlstm_step_kernel, grid=(b, M_HEADS), in_specs=in_specs,
        out_specs=[head, c_blk, st(1, M_HEAD_DIM), st(1, LANES)],
        out_shape=[jax.ShapeDtypeStruct((b, 1, D_INNER), F32), jax.ShapeDtypeStruct(c_all.shape, F32),
                   jax.ShapeDtypeStruct((b, M_HEADS, 1, M_HEAD_DIM), F32), jax.ShapeDtypeStruct((b, M_HEADS, 1, LANES), F32)],
        input_output_aliases=aliases,
        compiler_params=_cparams("parallel", "parallel"), name="mlstm_step",
    )(*args)
    return hn.reshape(b, D_INNER), cf, nf.reshape(b, M_HEADS, M_HEAD_DIM), mf[:, :, 0, 0]


def mlstm_sample_layer(layer, x, conv_state, c_all, c_new_all, n0, m0, nw, w_in, conv_w, conv_b, w_qkv, w_gate, b_gate, norm_w,
                       skip, w_out, final_w, *, final):
    b = x.shape[0]
    xm, z = rms_proj(x, nw, [w_in[:, :D_INNER], w_in[:, D_INNER:]], (None, None), tm=b, exact=True, name="mlstm_in_proj_sample")
    hist = conv_state.transpose(1, 0, 2)
    q, k, v, c, gates = mconv_sample(xm, hist, conv_w, conv_b, w_qkv, w_gate, b_gate)
    hn, cf, nf, mf = mlstm_step(layer, q, k, v, gates, c_all, n0, m0, c_new_all)
    y = mlstm_out(hn, c, z, x, norm_w, skip, w_out, final_w, tm=b, final=final, exact=True)
    conv_new = jnp.concatenate([hist[1:], xm[None]], axis=0).transpose(1, 0, 2)
    return y, cf, nf, mf, conv_new


SAMPLE_ROWS = 8
PAGES_PER_STEP = 16
N_BLK_PAD = 256


def _row_scalars(vals):
    row = lax.broadcasted_iota(jnp.int32, (SAMPLE_ROWS, 1), 0)
    col = jnp.zeros((SAMPLE_ROWS, 1), F32)
    for g, v in enumerate(vals):
        col = jnp.where(row == g, v, col)
    return col


def _nsa_sample_kernel(pt_ref, slopes_ref, *refs, gp, n_groups, past_len):
    cmp_pages, sel_pages = refs[:gp], refs[gp:2 * gp]
    (q_ref, qbd_ref, kvn_ref, cwin_ref, wn_ref, g_ref, wp_ref, pe_ref, w1f_ref, b1_ref, w2_ref, ov_ref, ex_ref, o_ref,
     cmp_s, kc_s, vc_s, bias_s, oc_s, m_s, l_s, acc_s) = refs[2 * gp:]
    step = pl.program_id(1)
    t = past_len
    n_seg = past_len // CMP_STRIDE
    n_blk = past_len // SEL_BLK + 1
    span = gp * PAGE_SIZE
    slope_cols = [_row_scalars([slopes_ref[kh * GROUP + g] for g in range(GROUP)]) for kh in range(N_KV_HEADS)]

    @pl.when(step < n_groups)
    def _stash():
        for i in range(gp):
            row0 = pl.multiple_of((step * gp + i) * PAGE_SIZE, PAGE_SIZE)
            for c in range(2):
                for hp in range(N_KV_HEADS // 2):
                    a = cmp_pages[i][0, 0, c, 2 * hp:2 * hp + 2].reshape(2 * HEAD_DIM, PAGE_SIZE)
                    cmp_s[c * 2 + hp, pl.ds(row0, PAGE_SIZE), :] = a.T

    @pl.when(step == n_groups - 1)
    def _compress_and_select():
        for slab in range(4):
            c, hp = divmod(slab, 2)
            x = jnp.concatenate([cmp_s[slab, pl.ds(r, n_seg, stride=CMP_STRIDE), :] for r in range(CMP_STRIDE)], axis=1)
            part = jnp.dot(x.astype(BF16), wp_ref[c], preferred_element_type=F32)
            hidc = jnp.dot(pe_ref[c], w1f_ref[c], preferred_element_type=F32)[0:1] + b1_ref[c]
            for hh in range(2):
                p0 = part[:, hh * 2 * CMP_HID:hh * 2 * CMP_HID + CMP_HID]
                p1 = part[:, hh * 2 * CMP_HID + CMP_HID:(hh + 1) * 2 * CMP_HID]
                hid = hidc + p0 + pltpu.roll(p1, n_seg - 1, 0)
                tok = jnp.dot(_silu(hid).astype(BF16), w2_ref[c], preferred_element_type=F32)
                if c == 0:
                    kc_s[2 * hp + hh] = tok
                else:
                    vc_s[2 * hp + hh] = tok
        cend = lax.broadcasted_iota(jnp.int32, (1, n_seg), 1) * CMP_STRIDE + (CMP_BLK - 1)
        d_c = (t - cend).astype(F32)
        ok = d_c >= 0.0
        row = lax.broadcasted_iota(jnp.int32, (SAMPLE_ROWS, 1), 0)
        blk = lax.broadcasted_iota(jnp.int32, (1, N_BLK_PAD), 1)
        tb = t // SEL_BLK
        forced = (blk == 0) | (blk == tb) | (blk == tb - 1)
        ii = lax.broadcasted_iota(jnp.int32, (N_BLK_PAD, N_BLK_PAD), 0)
        jj = lax.broadcasted_iota(jnp.int32, (N_BLK_PAD, N_BLK_PAD), 1)
        sel_rows = []
        for kh in range(N_KV_HEADS):
            q8 = (q_ref[0, kh] * Q_SCALE).astype(BF16)
            s = lax.dot_general(q8, kc_s[kh].astype(BF16), NT, preferred_element_type=F32) - slope_cols[kh] * d_c
            s = jnp.where(ok, s, NEG)
            m = jnp.max(s, axis=-1, keepdims=True)
            p = jnp.where(ok, jnp.exp(s - m), 0.0)
            pn = (p / jnp.maximum(jnp.sum(p, axis=-1, keepdims=True), 1e-30)).astype(BF16)
            oc_s[kh] = jnp.dot(pn, vc_s[kh].astype(BF16), preferred_element_type=F32)
            pn_heads = jnp.where(row < GROUP, pn, jnp.zeros_like(pn))
            imp = jnp.sum(jnp.dot(pn_heads, ov_ref[...], preferred_element_type=F32), axis=0, keepdims=True)
            score = jnp.where(forced, FORCE_SCORE, jnp.where(blk * SEL_BLK <= t, imp, -1.0))
            score = jnp.where(blk < n_blk, score, -2.0)
            col = jnp.broadcast_to(score, (SAMPLE_ROWS, N_BLK_PAD)).T[:, 0:1]
            beats = (col > score) | ((ii < jj) & (col == score))
            cnt = jnp.sum(jnp.where(beats, 1.0, 0.0), axis=0, keepdims=True)
            sel = jnp.where((cnt < float(min(N_SEL, n_blk))) & (score > -0.5), 1.0, 0.0)
            sel_rows.append(jnp.broadcast_to(sel, (SAMPLE_ROWS, N_BLK_PAD)))
        picked = jnp.dot(jnp.concatenate(sel_rows, axis=0).astype(BF16), ex_ref[...], preferred_element_type=F32)
        key_pos = lax.broadcasted_iota(jnp.int32, (1, past_len), 1)
        bias = jnp.where(picked > 0.5, jnp.concatenate(slope_cols, axis=0) * (key_pos - t).astype(F32), NEG)
        for gi in range(n_groups):
            bias_s[gi] = bias[:, gi * span:(gi + 1) * span]
        m_s[...] = jnp.full(m_s.shape, M_INIT, F32)
        l_s[...] = jnp.zeros(l_s.shape, F32)
        acc_s[...] = jnp.zeros(acc_s.shape, F32)

    @pl.when(step >= n_groups)
    def _selected():
        gb = step - n_groups
        kk = jnp.concatenate([sel_pages[i][0, 0, 0].reshape(KV_WIDTH, PAGE_SIZE) for i in range(gp)], axis=1).astype(BF16)
        vv = jnp.concatenate([sel_pages[i][0, 0, 1].reshape(KV_WIDTH, PAGE_SIZE) for i in range(gp)], axis=1).astype(BF16)
        s = jnp.dot(qbd_ref[0], kk, preferred_element_type=F32) + bias_s[gb]
        m_old = m_s[:, 0:1]
        m_new = jnp.maximum(m_old, jnp.max(s, axis=-1, keepdims=True))
        alpha = jnp.exp(m_old - m_new)
        p = jnp.exp(s - m_new)
        l_s[...] = jnp.broadcast_to(alpha * l_s[:, 0:1] + jnp.sum(p, axis=-1, keepdims=True), l_s.shape)
        acc_s[...] = alpha * acc_s[...] + lax.dot_general(p.astype(BF16), vv, NT, preferred_element_type=F32)
        m_s[...] = jnp.broadcast_to(m_new, m_s.shape)

    @pl.when(step == 2 * n_groups - 1)
    def _finish():
        wr = cwin_ref.shape[-1]
        d_w = wr - lax.broadcasted_iota(jnp.int32, (1, wr), 1)
        ok_w = (d_w <= WINDOW) & (t - d_w >= 0)
        for kh in range(N_KV_HEADS):
            q8 = q_ref[0, kh] * Q_SCALE
            rows = slice(kh * SAMPLE_ROWS, (kh + 1) * SAMPLE_ROWS)
            s_n = jnp.sum(q8 * kvn_ref[0, 2, kh], axis=-1, keepdims=True)
            m_old = m_s[rows, 0:1]
            m_new = jnp.maximum(m_old, s_n)
            alpha = jnp.exp(m_old - m_new)
            p_n = jnp.exp(s_n - m_new)
            l = alpha * l_s[rows, 0:1] + p_n
            acc = acc_s[rows, kh * HEAD_DIM:(kh + 1) * HEAD_DIM]
            o_sel = (alpha * acc + p_n * kvn_ref[0, 3, kh]) / jnp.maximum(l, 1e-30)
            s_w = jnp.dot(q8.astype(BF16), cwin_ref[0, 0, 0, kh].astype(BF16), preferred_element_type=F32)
            s_w = jnp.where(ok_w, s_w - slope_cols[kh] * d_w.astype(F32), NEG)
            s_wn = jnp.sum(q8 * wn_ref[0, 0, kh], axis=-1, keepdims=True)
            m_w = jnp.maximum(jnp.max(s_w, axis=-1, keepdims=True), s_wn)
            p_w = jnp.where(ok_w, jnp.exp(s_w - m_w), 0.0)
            p_wn = jnp.exp(s_wn - m_w)
            l_w = jnp.sum(p_w, axis=-1, keepdims=True) + p_wn
            o_win = (lax.dot_general(p_w.astype(BF16), cwin_ref[0, 0, 1, kh].astype(BF16), NT, preferred_element_type=F32)
                     + p_wn * wn_ref[0, 1, kh]) / jnp.maximum(l_w, 1e-30)
            gts = g_ref[0, kh]
            o_ref[0, kh] = gts[:, 0:1] * oc_s[kh] + gts[:, 1:2] * o_sel + gts[:, 2:3] * o_win


def _pair_w1(w1):
    wr = w1.reshape(2, CMP_R, CMP_STRIDE, HEAD_DIM, CMP_HID).transpose(0, 2, 3, 1, 4)
    wp = jnp.einsum("crdje,hk->crhdkje", wr, jnp.eye(2, dtype=w1.dtype))
    return wp.reshape(2, CMP_STRIDE * 2 * HEAD_DIM, 2 * CMP_R * CMP_HID).astype(BF16)


def nsa_sample_attention(layer, q, kv, win, gates, cache_kv, cache_win, page_table, pe, w1, b1, w2):
    b = q.shape[0]
    n_pages = page_table.shape[1]
    past_len = n_pages * PAGE_SIZE
    gp = PAGES_PER_STEP
    assert n_pages % gp == 0 and cache_win.shape[2] == WINDOW and past_len // SEL_BLK + 1 <= N_BLK_PAD
    n_groups = n_pages // gp
    n_seg = past_len // CMP_STRIDE
    pad_rows = lambda a: jnp.pad(a, ((0, 0), (0, 0), (0, SAMPLE_ROWS - GROUP), (0, 0)))
    q4 = pad_rows(q.reshape(b, N_KV_HEADS, GROUP, HEAD_DIM))
    g4 = pad_rows(gates[:, :3 * N_HEADS].reshape(b, N_KV_HEADS, GROUP, 3))
    q_bd = jnp.einsum("bkgd,kj->bkgjd", q4 * Q_SCALE, jnp.eye(N_KV_HEADS, dtype=F32))
    q_bd = q_bd.reshape(b, N_KV_HEADS * SAMPLE_ROWS, KV_WIDTH).astype(BF16)
    kvn = kv.reshape(b, 4, N_KV_HEADS, 1, HEAD_DIM)
    wn = win.reshape(b, 2, N_KV_HEADS, 1, HEAD_DIM)
    cache_t = cache_kv.transpose(0, 1, 3, 4, 5, 2)
    cwin_t = cache_win.transpose(0, 1, 3, 4, 5, 2)
    start = np.arange(n_seg) * CMP_STRIDE
    blk = np.arange(N_BLK_PAD) * SEL_BLK
    ov = jnp.asarray((start[:, None] < blk[None, :] + SEL_BLK) & (start[:, None] + CMP_BLK > blk[None, :]), dtype=BF16)
    pe8 = jnp.broadcast_to(pe.reshape(2, 1, CMP_BLK * HEAD_DIM), (2, 8, CMP_BLK * HEAD_DIM)).astype(BF16)
    w1f = w1.reshape(2, CMP_BLK * HEAD_DIM, CMP_HID).astype(BF16)
    expand = (jnp.arange(N_BLK_PAD)[:, None] == jnp.arange(past_len)[None, :] // SEL_BLK).astype(BF16)
    consts = (_pair_w1(w1), pe8, w1f, b1.reshape(2, 1, CMP_HID), w2.astype(BF16), ov, expand)
    page_blk = (1, 1, 2, N_KV_HEADS, HEAD_DIM, PAGE_SIZE)

    def cmp_map(i):
        return lambda bi, s, pt, sl: (layer, pt[bi * n_pages + jnp.minimum(s, n_groups - 1) * gp + i], 0, 0, 0, 0)

    def sel_map(i):
        return lambda bi, s, pt, sl: (layer, pt[bi * n_pages + jnp.maximum(s - n_groups, 0) * gp + i], 1, 0, 0, 0)

    per_b = lambda a: pl.BlockSpec((1,) + a.shape[1:], lambda bi, s, pt, sl: (bi,) + (0,) * (a.ndim - 1))
    const = lambda a: pl.BlockSpec(a.shape, lambda bi, s, pt, sl: (0,) * a.ndim)
    in_specs = [pl.BlockSpec(page_blk, cmp_map(i)) for i in range(gp)] + [pl.BlockSpec(page_blk, sel_map(i)) for i in range(gp)]
    in_specs += [per_b(q4), per_b(q_bd), per_b(kvn),
                 pl.BlockSpec((1, 1) + cwin_t.shape[2:], lambda bi, s, pt, sl: (layer, bi, 0, 0, 0, 0)),
                 per_b(wn), per_b(g4)] + [const(a) for a in consts]
    all_rows = N_KV_HEADS * SAMPLE_ROWS
    small = pltpu.VMEM((all_rows, LANES), F32)
    grid_spec = pltpu.PrefetchScalarGridSpec(
        num_scalar_prefetch=2, grid=(b, 2 * n_groups), in_specs=in_specs, out_specs=per_b(q4),
        scratch_shapes=[pltpu.VMEM((4, past_len, LANES), F32),
                        pltpu.VMEM((N_KV_HEADS, n_seg, HEAD_DIM), F32), pltpu.VMEM((N_KV_HEADS, n_seg, HEAD_DIM), F32),
                        pltpu.VMEM((n_groups, all_rows, gp * PAGE_SIZE), F32),
                        pltpu.VMEM((N_KV_HEADS, SAMPLE_ROWS, HEAD_DIM), F32), small, small,
                        pltpu.VMEM((all_rows, KV_WIDTH), F32)])
    o = pl.pallas_call(
        functools.partial(_nsa_sample_kernel, gp=gp, n_groups=n_groups, past_len=past_len), grid_spec=grid_spec,
        out_shape=jax.ShapeDtypeStruct(q4.shape, F32),
        compiler_params=_cparams("parallel", "arbitrary"), name="nsa_sample_attn",
    )(page_table.reshape(-1), _alibi_slopes(), *([cache_t] * (2 * gp)), q4, q_bd, kvn, cwin_t, wn, g4, *consts)
    return o[:, :, :GROUP].reshape(b, ATT_WIDTH)


def nsa_sample_layer(layer, x, cache_kv, cache_win, page_table, nw, w_in, w_out, pe, w1, b1, w2):
    b = x.shape[0]
    q, kv, win, gates, z = rms_proj(x, nw, _split_nsa_w_in(w_in, F32), NSA_ACTS, tm=b, exact=True, name="nsa_in_proj_sample")
    o = nsa_sample_attention(layer, q, kv, win, gates, cache_kv, cache_win, page_table, pe, w1, b1, w2)
    zero = jnp.zeros_like(o)
    y = nsa_out(o, zero, zero, z, x, w_out, tm=b, exact=True)
    win_new = jnp.concatenate([cache_win[layer][:, 1:], win.reshape(b, 1, 2, N_KV_HEADS, HEAD_DIM)], axis=1)
    return y, kv.reshape(b, 1, 4, N_KV_HEADS, HEAD_DIM), win_new


def kernel(x_prompt, x_sample, cache_kv, cache_win, state_C, state_n, state_m, state_conv, page_table, norm_w, final_norm_w,
           nsa_w_in, nsa_w_out, nsa_cmp_pe, nsa_cmp_w1, nsa_cmp_b1, nsa_cmp_w2, m_w_in, m_conv_w, m_conv_b, m_w_qkv, m_w_gate,
           m_b_gate, m_norm_w, m_skip, m_w_out):
    assert DEPTH % 2 == 0 and x_sample.shape[1] == 1
    yp, ys = x_prompt, x_sample[:, 0]
    outs = {name: [] for name in ("kv_s", "win_p", "win_s", "C_p", "n_p", "n_s", "m_p", "m_s", "cv_p", "cv_s")}
    c_sample = None
    kvt_prompt = None
    n_nsa = (DEPTH + 1) // 2
    for i in range(DEPTH):
        l = i // 2
        if i % 2 == 0:
            prm = (nsa_w_in[l], nsa_w_out[l], nsa_cmp_pe[l], nsa_cmp_w1[l], nsa_cmp_b1[l], nsa_cmp_w2[l])
            yp, kvt_prompt, wp = nsa_prompt_layer_t(l, n_nsa, yp, kvt_prompt, norm_w[i], *prm)
            ys, kvs, wsm = nsa_sample_layer(l, ys, cache_kv, cache_win, page_table, norm_w[i], *prm)
            for name, val in (("kv_s", kvs), ("win_p", wp), ("win_s", wsm)):
                outs[name].append(val)
        else:
            prm = (m_w_in[l], m_conv_w[l], m_conv_b[l], m_w_qkv[l], m_w_gate[l], m_b_gate[l], m_norm_w[l], m_skip[l], m_w_out[l])
            final = i == DEPTH - 1
            yp, cp, np_, mp, cvp = mlstm_prompt_layer(yp, norm_w[i], *prm, final_norm_w, final=final)
            ys, c_sample, ns, ms, cvs = mlstm_sample_layer(l, ys, state_conv[l], state_C, c_sample, state_n[l], state_m[l],
                                                           norm_w[i], *prm, final_norm_w, final=final)
            for name, val in (("C_p", cp), ("n_p", np_), ("n_s", ns), ("m_p", mp), ("m_s", ms), ("cv_p", cvp), ("cv_s", cvs)):
                outs[name].append(val)
    st = {name: jnp.stack(vals) for name, vals in outs.items()}
    b, t = x_prompt.shape[:2]
    kv_prompt = kvt_prompt.reshape(n_nsa, b, 4, N_KV_HEADS, HEAD_DIM, t).transpose(0, 1, 5, 2, 3, 4)
    return (yp, ys[:, None], kv_prompt, st["kv_s"], st["win_p"], st["win_s"], st["C_p"], c_sample, st["n_p"], st["n_s"],
            st["m_p"], st["m_s"], st["cv_p"], st["cv_s"])
```

```python
import functools
import math

import jax
import jax.numpy as jnp
import numpy as np
from jax import lax
from jax.experimental import pallas as pl
from jax.experimental.pallas import tpu as pltpu

F32 = jnp.float32
BF16 = jnp.bfloat16
HI = lax.Precision.HIGHEST

D_MODEL = 1024
DEPTH = 4
N_HEADS = 16
HEAD_DIM = 64
N_KV_HEADS = 4
GROUP = N_HEADS // N_KV_HEADS
ATT_WIDTH = N_HEADS * HEAD_DIM
KV_WIDTH = N_KV_HEADS * HEAD_DIM
CMP_BLK = 32
CMP_STRIDE = 16
CMP_R = CMP_BLK // CMP_STRIDE
CMP_HID = 2 * HEAD_DIM
SEL_BLK = 64
N_SEL = 16
WINDOW = 512
FORCE_SCORE = 1.0e4
D_INNER = 2 * D_MODEL
M_HEADS = 4
M_HEAD_DIM = D_INNER // M_HEADS
CONV_W = 4
QKV_BLK = 4
RMS_EPS = 1e-6
PAGE_SIZE = 128

LANES = 128
VMEM_LIMIT = 56 * 1024 * 1024
NEG = -1.0e30
M_INIT = -5.0e29
Q_SCALE = HEAD_DIM ** -0.5
NT = (((1,), (1,)), ((), ()))
TN = (((0,), (0,)), ((), ()))


def _cparams(*sem):
    return pltpu.CompilerParams(dimension_semantics=sem, vmem_limit_bytes=VMEM_LIMIT)


def _silu(x):
    return x * jax.nn.sigmoid(x)


def _alibi_slopes():
    return jnp.asarray(np.exp2(-8.0 * np.arange(1, N_HEADS + 1) / N_HEADS), dtype=F32)


def _rmsnorm(x, w):
    return x * lax.rsqrt(jnp.mean(x * x, axis=-1, keepdims=True) + RMS_EPS) * w


def _rms_proj_kernel(x_ref, nw_ref, *refs, acts, n_chunk, exact):
    n = len(acts)
    w_refs, o_refs = refs[:n], refs[n:]
    h = _rmsnorm(x_ref[...], nw_ref[...])
    if not exact:
        h = h.astype(BF16)
    for w_ref, o_ref, act in zip(w_refs, o_refs, acts):
        width = w_ref.shape[1]
        for n0 in range(0, width, n_chunk):
            n1 = min(width, n0 + n_chunk)
            if exact:
                y = jnp.dot(h, w_ref[:, n0:n1], preferred_element_type=F32, precision=HI)
            else:
                y = jnp.dot(h, w_ref[:, n0:n1], preferred_element_type=F32)
            if act == "sigmoid":
                y = jax.nn.sigmoid(y)
            o_ref[:, n0:n1] = y.astype(o_ref.dtype)


def rms_proj(x, nw, weights, acts, *, tm, exact=False, out_dtypes=None, name="rms_proj"):
    m, k = x.shape
    assert m % tm == 0
    out_dtypes = out_dtypes or (F32,) * len(weights)
    in_specs = [pl.BlockSpec((tm, k), lambda i: (i, 0)), pl.BlockSpec((1, k), lambda i: (0, 0))]
    in_specs += [pl.BlockSpec(w.shape, lambda i: (0, 0)) for w in weights]
    out_specs = [pl.BlockSpec((tm, w.shape[1]), lambda i: (i, 0)) for w in weights]
    out_shape = [jax.ShapeDtypeStruct((m, w.shape[1]), dt) for w, dt in zip(weights, out_dtypes)]
    return pl.pallas_call(
        functools.partial(_rms_proj_kernel, acts=tuple(acts), n_chunk=512, exact=exact),
        grid=(m // tm,), in_specs=in_specs, out_specs=out_specs, out_shape=out_shape,
        compiler_params=_cparams("parallel"), name=name,
    )(x, nw.reshape(1, k), *weights)


def _compress_kernel(x_ref, w1_ref, pe_ref, w1f_ref, b1_ref, w2_ref, o_ref, *, n_seg):
    x = x_ref[0, 0, 0]
    p0 = jnp.dot(x, w1_ref[0, 0], preferred_element_type=F32)
    p1 = jnp.dot(x, w1_ref[0, 1], preferred_element_type=F32)
    hidc = jnp.dot(pe_ref[0], w1f_ref[0], preferred_element_type=F32)[0:1] + b1_ref[0]
    hid = hidc + p0 + pltpu.roll(p1, n_seg - 1, 0)
    o_ref[0, 0, 0] = jnp.dot(_silu(hid).astype(BF16), w2_ref[0], preferred_element_type=F32)


def compress_tokens(x, pe, w1, b1, w2):
    b, _, hkv, n_seg, _ = x.shape
    w1r = w1.reshape(2, CMP_R, CMP_STRIDE * HEAD_DIM, CMP_HID).astype(BF16)
    w1f = w1.reshape(2, CMP_BLK * HEAD_DIM, CMP_HID).astype(BF16)
    pe8 = jnp.broadcast_to(pe.reshape(2, 1, CMP_BLK * HEAD_DIM), (2, 8, CMP_BLK * HEAD_DIM)).astype(BF16)
    return pl.pallas_call(
        functools.partial(_compress_kernel, n_seg=n_seg),
        grid=(b, 2, hkv),
        in_specs=[
            pl.BlockSpec((1, 1, 1, n_seg, CMP_STRIDE * HEAD_DIM), lambda i, c, h: (i, c, h, 0, 0)),
            pl.BlockSpec((1, CMP_R, CMP_STRIDE * HEAD_DIM, CMP_HID), lambda i, c, h: (c, 0, 0, 0)),
            pl.BlockSpec((1, 8, CMP_BLK * HEAD_DIM), lambda i, c, h: (c, 0, 0)),
            pl.BlockSpec((1, CMP_BLK * HEAD_DIM, CMP_HID), lambda i, c, h: (c, 0, 0)),
            pl.BlockSpec((1, 1, CMP_HID), lambda i, c, h: (c, 0, 0)),
            pl.BlockSpec((1, CMP_HID, HEAD_DIM), lambda i, c, h: (c, 0, 0)),
        ],
        out_specs=pl.BlockSpec((1, 1, 1, n_seg, HEAD_DIM), lambda i, c, h: (i, c, h, 0, 0)),
        out_shape=jax.ShapeDtypeStruct((b, 2, hkv, n_seg, HEAD_DIM), F32),
        compiler_params=_cparams("parallel", "parallel", "parallel"), name="nsa_compress",
    )(x, w1r, pe8, w1f, b1.reshape(2, 1, CMP_HID), w2.astype(BF16))


def _topk_mask_t(score_t, n_valid, k):
    r = score_t.shape[0]
    jidx = lax.broadcasted_iota(jnp.int32, (r, 1), 0)
    cnt = jnp.zeros(score_t.shape, F32)
    for i in range(n_valid):
        row = score_t[i:i + 1, :]
        beats = (row > score_t) | ((jidx > i) & (row == score_t))
        cnt = cnt + jnp.where(beats, 1.0, 0.0)
    return cnt < float(k)


def _cmp_sel_kernel(slopes_ref, q_ref, kc_ref, vc_ref, g_ref, ov_ref, oc_ref, sel_ref, *, tq, n_seg, n_blk):
    kh = pl.program_id(1)
    t0 = pl.program_id(2) * tq
    t_col = t0 + lax.broadcasted_iota(jnp.int32, (tq, 1), 0)
    cend = lax.broadcasted_iota(jnp.int32, (1, n_seg), 1) * CMP_STRIDE + (CMP_BLK - 1)
    d_c = (t_col - cend).astype(F32)
    ok = d_c >= 0.0
    kc = kc_ref[0, 0, 0].astype(BF16)
    vc = vc_ref[0, 0, 0].astype(BF16)
    imp = jnp.zeros((tq, LANES), F32)
    outs = []
    for g in range(GROUP):
        qg = (q_ref[0, :, g * HEAD_DIM:(g + 1) * HEAD_DIM] * Q_SCALE).astype(BF16)
        s = lax.dot_general(qg, kc, NT, preferred_element_type=F32) - slopes_ref[kh * GROUP + g] * d_c
        s = jnp.where(ok, s, NEG)
        m = jnp.max(s, axis=-1, keepdims=True)
        p = jnp.where(ok, jnp.exp(s - m), 0.0)
        pn = (p / jnp.maximum(jnp.sum(p, axis=-1, keepdims=True), 1e-30)).astype(BF16)
        o = jnp.dot(pn, vc, preferred_element_type=F32)
        outs.append(o * g_ref[0, 0, :, 3 * g:3 * g + 1])
        imp = imp + jnp.dot(pn, ov_ref[...], preferred_element_type=F32)
    oc_ref[0] = jnp.concatenate(outs, axis=1)
    blk = lax.broadcasted_iota(jnp.int32, (1, LANES), 1)
    tb = lax.shift_right_logical(t_col, int(math.log2(SEL_BLK)))
    forced = (blk == 0) | (blk == tb) | (blk == tb - 1)
    score = jnp.where(forced, FORCE_SCORE, jnp.where(blk * SEL_BLK <= t_col, imp, -1.0))
    score = jnp.where(blk < n_blk, score, -2.0)
    score_t = score.T
    sel_t = _topk_mask_t(score_t, n_blk, min(N_SEL, n_blk)) & (score_t > -0.5)
    sel_ref[0, 0] = jnp.where(sel_t, 1.0, 0.0).T


def cmp_select(q, kvc, gates_r, slopes, ov, *, tq):
    b, t, _ = q.shape
    n_seg = kvc.shape[3]
    n_blk = t // SEL_BLK
    grid_spec = pltpu.PrefetchScalarGridSpec(
        num_scalar_prefetch=1, grid=(b, N_KV_HEADS, t // tq),
        in_specs=[
            pl.BlockSpec((1, tq, GROUP * HEAD_DIM), lambda i, k, j, s: (i, j, k)),
            pl.BlockSpec((1, 1, 1, n_seg, HEAD_DIM), lambda i, k, j, s: (i, 0, k, 0, 0)),
            pl.BlockSpec((1, 1, 1, n_seg, HEAD_DIM), lambda i, k, j, s: (i, 1, k, 0, 0)),
            pl.BlockSpec((1, 1, tq, 3 * GROUP), lambda i, k, j, s: (i, k, j, 0)),
            pl.BlockSpec((n_seg, LANES), lambda i, k, j, s: (0, 0)),
        ],
        out_specs=[
            pl.BlockSpec((1, tq, GROUP * HEAD_DIM), lambda i, k, j, s: (i, j, k)),
            pl.BlockSpec((1, 1, tq, LANES), lambda i, k, j, s: (i, k, j, 0)),
        ],
    )
    return pl.pallas_call(
        functools.partial(_cmp_sel_kernel, tq=tq, n_seg=n_seg, n_blk=n_blk),
        grid_spec=grid_spec,
        out_shape=[jax.ShapeDtypeStruct((b, t, ATT_WIDTH), F32), jax.ShapeDtypeStruct((b, N_KV_HEADS, t, LANES), F32)],
        compiler_params=_cparams("parallel", "parallel", "parallel"), name="nsa_cmp_select",
    )(slopes, q, kvc, kvc, gates_r, ov)


def _cmp_to_sel_matrix(n_rows, n_blk):
    start = np.arange(n_rows) * CMP_STRIDE
    blk = np.arange(LANES) * SEL_BLK
    ov = (start[:, None] < blk[None, :] + SEL_BLK) & (start[:, None] + CMP_BLK > blk[None, :]) & (np.arange(LANES)[None, :] < n_blk)
    return jnp.asarray(ov, dtype=BF16)


def _stack_heads(q_blk, extra=None):
    parts = []
    for g in range(GROUP):
        qg = q_blk[:, g * HEAD_DIM:(g + 1) * HEAD_DIM] * Q_SCALE
        if extra is not None:
            qg = jnp.concatenate([qg, extra], axis=1)
        parts.append(qg)
    return jnp.concatenate(parts, axis=0).astype(BF16)


def _head_cols(fn, tq):
    return jnp.concatenate([jnp.full((tq, 1), fn(g), F32) for g in range(GROUP)], axis=0)


def _sel_attn_kernel(slopes_ref, q_ref, sel_ref, ka_ref, v_ref, g_ref, o_ref, *, tq, tk):
    kh = pl.program_id(1)
    t0 = pl.program_id(2) * tq
    mask_feat = (sel_ref[0, 0][:, 0:HEAD_DIM] - 1.0) * 1.0e30
    qa = _stack_heads(q_ref[0], mask_feat)
    slope_col = _head_cols(lambda g: slopes_ref[kh * GROUP + g], tq)
    t_col = t0 + lax.broadcasted_iota(jnp.int32, (tq, 1), 0)
    t_col4 = jnp.concatenate([t_col] * GROUP, axis=0)

    def body(kt, carry):
        m, l, acc = carry
        ks = pl.multiple_of(kt * tk, tk)
        s = lax.dot_general(qa, ka_ref[0, 0, pl.ds(ks, tk), :], NT, preferred_element_type=F32)
        pos = ks + lax.broadcasted_iota(jnp.int32, (1, tk), 1)
        s = s + slope_col * (pos - t0).astype(F32)
        s = jnp.where(pos <= t_col4, s, NEG)
        m_new = jnp.maximum(m, jnp.max(s, axis=-1, keepdims=True))
        alpha = jnp.exp(m - m_new)
        p = jnp.exp(s - m_new)
        l = alpha * l + jnp.sum(p, axis=-1, keepdims=True)
        acc = alpha * acc + jnp.dot(p.astype(BF16), v_ref[0, 0, pl.ds(ks, tk), :], preferred_element_type=F32)
        return m_new, l, acc

    n_kt = (t0 + tq + tk - 1) // tk
    init = (jnp.full((GROUP * tq, 1), M_INIT, F32), jnp.zeros((GROUP * tq, 1), F32), jnp.zeros((GROUP * tq, HEAD_DIM), F32))
    _, l, acc = lax.fori_loop(0, n_kt, body, init)
    o = acc / jnp.maximum(l, 1e-30)
    o_ref[0] = jnp.concatenate(
        [o[g * tq:(g + 1) * tq] * g_ref[0, 0, :, 3 * g + 1:3 * g + 2] for g in range(GROUP)], axis=1)


def sel_attention(q, sel, k_aug, v_sel, gates_r, slopes, *, tq, tk):
    b, t, _ = q.shape
    grid_spec = pltpu.PrefetchScalarGridSpec(
        num_scalar_prefetch=1, grid=(b, N_KV_HEADS, t // tq),
        in_specs=[
            pl.BlockSpec((1, tq, GROUP * HEAD_DIM), lambda i, k, j, s: (i, j, k)),
            pl.BlockSpec((1, 1, tq, LANES), lambda i, k, j, s: (i, k, j, 0)),
            pl.BlockSpec((1, 1, t, LANES), lambda i, k, j, s: (i, k, 0, 0)),
            pl.BlockSpec((1, 1, t, HEAD_DIM), lambda i, k, j, s: (i, k, 0, 0)),
            pl.BlockSpec((1, 1, tq, 3 * GROUP), lambda i, k, j, s: (i, k, j, 0)),
        ],
        out_specs=pl.BlockSpec((1, tq, GROUP * HEAD_DIM), lambda i, k, j, s: (i, j, k)),
    )
    return pl.pallas_call(
        functools.partial(_sel_attn_kernel, tq=tq, tk=tk), grid_spec=grid_spec,
        out_shape=jax.ShapeDtypeStruct((b, t, ATT_WIDTH), F32),
        compiler_params=_cparams("parallel", "parallel", "arbitrary"), name="nsa_sel_attn",
    )(slopes, q, sel, k_aug, v_sel, gates_r)


def _win_attn_kernel(slopes_ref, q_ref, k_ref, v_ref, g_ref, o_ref, *, tq, span):
    kh = pl.program_id(1)
    t0 = pl.program_id(2) * tq
    qs = _stack_heads(q_ref[0])
    slope_col = _head_cols(lambda g: slopes_ref[kh * GROUP + g], tq)
    t_col = t0 + lax.broadcasted_iota(jnp.int32, (tq, 1), 0)
    t_col4 = jnp.concatenate([t_col] * GROUP, axis=0)
    start = pl.multiple_of(jnp.maximum(t0 + tq - span, 0), tq)
    s = lax.dot_general(qs, k_ref[0, 0, pl.ds(start, span), :], NT, preferred_element_type=F32)
    d = t_col4 - (start + lax.broadcasted_iota(jnp.int32, (1, span), 1))
    ok = (d >= 0) & (d <= WINDOW)
    s = jnp.where(ok, s - slope_col * d.astype(F32), NEG)
    m = jnp.max(s, axis=-1, keepdims=True)
    p = jnp.where(ok, jnp.exp(s - m), 0.0)
    l = jnp.sum(p, axis=-1, keepdims=True)
    o = jnp.dot(p.astype(BF16), v_ref[0, 0, pl.ds(start, span), :], preferred_element_type=F32) / jnp.maximum(l, 1e-30)
    o_ref[0] = jnp.concatenate(
        [o[g * tq:(g + 1) * tq] * g_ref[0, 0, :, 3 * g + 2:3 * g + 3] for g in range(GROUP)], axis=1)


def win_attention(q, k_win, v_win, gates_r, slopes, *, tq):
    b, t, _ = q.shape
    span = WINDOW + tq
    assert t >= span and span % tq == 0
    grid_spec = pltpu.PrefetchScalarGridSpec(
        num_scalar_prefetch=1, grid=(b, N_KV_HEADS, t // tq),
        in_specs=[
            pl.BlockSpec((1, tq, GROUP * HEAD_DIM), lambda i, k, j, s: (i, j, k)),
            pl.BlockSpec((1, 1, t, HEAD_DIM), lambda i, k, j, s: (i, k, 0, 0)),
            pl.BlockSpec((1, 1, t, HEAD_DIM), lambda i, k, j, s: (i, k, 0, 0)),
            pl.BlockSpec((1, 1, tq, 3 * GROUP), lambda i, k, j, s: (i, k, j, 0)),
        ],
        out_specs=pl.BlockSpec((1, tq, GROUP * HEAD_DIM), lambda i, k, j, s: (i, j, k)),
    )
    return pl.pallas_call(
        functools.partial(_win_attn_kernel, tq=tq, span=span), grid_spec=grid_spec,
        out_shape=jax.ShapeDtypeStruct((b, t, ATT_WIDTH), F32),
        compiler_params=_cparams("parallel", "parallel", "arbitrary"), name="nsa_win_attn",
    )(slopes, q, k_win, v_win, gates_r)


def _nsa_out_kernel(oc_ref, os_ref, ow_ref, z_ref, x_ref, w_ref, y_ref, *, exact):
    a = (oc_ref[...] + os_ref[...] + ow_ref[...]) * _silu(z_ref[...])
    if exact:
        y_ref[...] = x_ref[...] + jnp.dot(a, w_ref[...], preferred_element_type=F32, precision=HI)
    else:
        y_ref[...] = x_ref[...] + jnp.dot(a.astype(BF16), w_ref[...], preferred_element_type=F32)


def nsa_out(o_c, o_s, o_w, z, x, w_out, *, tm, exact=False):
    m, d = x.shape
    row = pl.BlockSpec((tm, d), lambda i: (i, 0))
    return pl.pallas_call(
        functools.partial(_nsa_out_kernel, exact=exact), grid=(m // tm,),
        in_specs=[row, row, row, row, row, pl.BlockSpec(w_out.shape, lambda i: (0, 0))],
        out_specs=row, out_shape=jax.ShapeDtypeStruct((m, d), F32),
        compiler_params=_cparams("parallel"), name="nsa_out",
    )(o_c, o_s, o_w, z, x, w_out)


LOG2E = math.log2(math.e)
Q_SCALE2 = Q_SCALE * LOG2E
ALIBI_ROWS = 16
BIG = 1.0e30
ONES_ROWS = 16
TQ = 256
TK = 512
WCH = 128


def _alibi_table():
    s = _alibi_slopes() * LOG2E
    s1 = s.astype(BF16).astype(F32)
    s2 = (s - s1).astype(BF16).astype(F32)
    s3 = (s - s1 - s2).astype(BF16).astype(F32)
    tab = jnp.stack([SEL_BLK * s1, SEL_BLK * s2, SEL_BLK * s3, s1, s2, s3, jnp.full_like(s, -BIG), jnp.zeros_like(s)], axis=1)
    return tab.reshape(-1)


def _pos_features(pos, valid, width):
    lane = lax.broadcasted_iota(jnp.int32, (pos.shape[0], width), 1)
    blk = lax.shift_right_logical(pos, int(math.log2(SEL_BLK))).astype(F32)
    rem = (pos & (SEL_BLK - 1)).astype(F32)
    f = jnp.where(lane < 3, blk, jnp.where(lane < 6, rem, 0.0))
    return jnp.where(lane == 6, jnp.where(valid, 0.0, 1.0), f)


def _nsa_in_proj_kernel(x_ref, nw_ref, wt_ref, wn_ref, *rest):
    qt_ref, kvt_ref, wint_ref, gt_ref, zt_ref, kn_ref, kwn_ref = rest[-7:]
    h = _rmsnorm(x_ref[0], nw_ref[...]).astype(BF16)

    def nt(r0, r1):
        return lax.dot_general(wt_ref[r0:r1, :], h, NT, preferred_element_type=F32)

    o1, o2, o3 = ATT_WIDTH, ATT_WIDTH + 4 * KV_WIDTH, ATT_WIDTH + 6 * KV_WIDTH
    o4 = o3 + LANES
    for r0 in range(0, o1, 512):
        qt_ref[0, r0:r0 + 512, :] = (nt(r0, r0 + 512) * Q_SCALE2).astype(BF16)
    for r0 in range(o1, o2, 512):
        kvt_ref[0, 0, r0 - o1:r0 - o1 + 512, :] = nt(r0, r0 + 512)
    wint_ref[0] = nt(o2, o3)
    gt_ref[0] = jax.nn.sigmoid(nt(o3, o4))
    for r0 in range(o4, o4 + ATT_WIDTH, 512):
        zt_ref[0, r0 - o4:r0 - o4 + 512, :] = nt(r0, r0 + 512).astype(zt_ref.dtype)
    yn = jnp.dot(h, wn_ref[...], preferred_element_type=F32)
    for j in range(3):
        for hh in range(N_KV_HEADS):
            c0 = (j * N_KV_HEADS + hh) * HEAD_DIM
            kn_ref[0, j, hh] = yn[:, c0:c0 + HEAD_DIM].astype(BF16)
    for hh in range(N_KV_HEADS):
        c0 = (3 * N_KV_HEADS + hh) * HEAD_DIM
        kwn_ref[0, hh] = yn[:, c0:c0 + HEAD_DIM].astype(BF16)


def nsa_in_proj(layer, n_layers, x, nw, w_in, kvt_all, *, tm):
    b, t, d = x.shape
    o1, o2, o3 = ATT_WIDTH, ATT_WIDTH + 4 * KV_WIDTH, ATT_WIDTH + 6 * KV_WIDTH
    o4 = o3 + 3 * N_HEADS
    w_t = w_in.T
    wt = jnp.concatenate([w_t[:o4], jnp.zeros((LANES - 3 * N_HEADS, d), w_in.dtype), w_t[o4:]], axis=0).astype(BF16)
    wn = jnp.concatenate([w_in[:, o1:o1 + 3 * KV_WIDTH], w_in[:, o2:o2 + KV_WIDTH]], axis=1).astype(BF16)
    tok = lambda rows: pl.BlockSpec((1, rows, tm), lambda i, j: (i, 0, j))
    full = lambda a: pl.BlockSpec(a.shape, lambda i, j: (0,) * a.ndim)
    tshape = lambda rows, dt: jax.ShapeDtypeStruct((b, rows, t), dt)
    args = [x, nw.reshape(1, d), wt, wn]
    in_specs = [pl.BlockSpec((1, tm, d), lambda i, j: (i, j, 0)), pl.BlockSpec((1, d), lambda i, j: (0, 0)), full(wt), full(wn)]
    aliases = {}
    if kvt_all is not None:
        args.append(kvt_all)
        in_specs.append(pl.BlockSpec(memory_space=pl.ANY))
        aliases = {len(args) - 1: 1}
    return pl.pallas_call(
        _nsa_in_proj_kernel, grid=(b, t // tm), in_specs=in_specs,
        out_specs=[tok(ATT_WIDTH), pl.BlockSpec((1, 1, 4 * KV_WIDTH, tm), lambda i, j: (layer, i, 0, j)),
                   tok(2 * KV_WIDTH), tok(LANES), tok(ATT_WIDTH),
                   pl.BlockSpec((1, 3, N_KV_HEADS, tm, HEAD_DIM), lambda i, j: (i, 0, 0, j, 0)),
                   pl.BlockSpec((1, N_KV_HEADS, tm, HEAD_DIM), lambda i, j: (i, 0, j, 0))],
        out_shape=[tshape(ATT_WIDTH, BF16), jax.ShapeDtypeStruct((n_layers, b, 4 * KV_WIDTH, t), F32),
                   tshape(2 * KV_WIDTH, F32), tshape(LANES, F32),
                   tshape(ATT_WIDTH, BF16), jax.ShapeDtypeStruct((b, 3, N_KV_HEADS, t, HEAD_DIM), BF16),
                   jax.ShapeDtypeStruct((b, N_KV_HEADS, t, HEAD_DIM), BF16)],
        input_output_aliases=aliases,
        compiler_params=_cparams("parallel", "parallel"), name="nsa_in_proj",
    )(*args)


def _topk_mask_rows(score, k):
    r = score.shape[0]
    groups = [score[8 * v:8 * v + 8] for v in range(r // 8)]
    sub = lax.broadcasted_iota(jnp.int32, (8, 1), 0)
    cnt = [jnp.zeros(g.shape, F32) for g in groups]
    for i in range(r):
        row = score[i:i + 1, :]
        for v, g in enumerate(groups):
            if 8 * v > i:
                beats = row >= g
            elif 8 * v + 7 < i:
                beats = row > g
            else:
                beats = (row > g) | ((sub > i - 8 * v) & (row == g))
            cnt[v] = cnt[v] + jnp.where(beats, 1.0, 0.0)
    return jnp.concatenate(cnt, axis=0) < float(k)


def _nsa_attn_kernel(tab_ref, qt_ref, gt_ref, kc_ref, vc_ref, ksel_ref, kwin_ref, vselt_ref, vwint_ref, ovt_ref, ot_ref,
                     ksa_s, kwa_s, kca_s, vst_s, vwt_s, vct_s, sc_s, m_s, acc_s, *, t_len, n_seg, n_blk):
    kh = pl.program_id(1)
    qi = pl.program_id(2)
    t0 = qi * TQ
    cols = GROUP * TQ
    ones = lambda n: jnp.ones((ONES_ROWS, n), BF16)

    @pl.when(qi == 0)
    def _build_keys():
        pos = lax.broadcasted_iota(jnp.int32, (t_len, 1), 0)
        onehot = jnp.where(lax.shift_right_logical(pos, int(math.log2(SEL_BLK)))
                           == lax.broadcasted_iota(jnp.int32, (1, HEAD_DIM), 1), 1.0, 0.0)
        ksa_s[:, 0:LANES] = jnp.concatenate([ksel_ref[0, 0, 0].astype(F32), onehot], axis=1).astype(BF16)
        ksa_s[:, LANES:2 * LANES] = _pos_features(pos, pos >= 0, LANES).astype(BF16)
        no_key = _pos_features(jnp.zeros((WINDOW, 1), jnp.int32), jnp.zeros((WINDOW, 1), jnp.bool_), HEAD_DIM)
        kwa_s[0:WINDOW, :] = jnp.concatenate([jnp.zeros((WINDOW, HEAD_DIM), F32), no_key], axis=1).astype(BF16)
        kwa_s[WINDOW:WINDOW + t_len, :] = jnp.concatenate(
            [kwin_ref[0, 0].astype(F32), _pos_features(pos, pos >= 0, HEAD_DIM)], axis=1).astype(BF16)
        cend = lax.broadcasted_iota(jnp.int32, (n_seg, 1), 0) * CMP_STRIDE + (CMP_BLK - 1)
        kca_s[...] = jnp.concatenate([kc_ref[0, 0, 0], _pos_features(cend, cend >= 0, HEAD_DIM)], axis=1).astype(BF16)
        vc_pad = jnp.concatenate([vc_ref[0, 0, 0], jnp.zeros((n_seg, LANES - HEAD_DIM), F32)], axis=1)
        vct_s[...] = jnp.concatenate([vc_pad.T[0:HEAD_DIM].astype(BF16), ones(n_seg)], axis=0)
        for c in range(t_len // TK):
            vst_s[c] = jnp.concatenate([vselt_ref[0, 0, :, c * TK:(c + 1) * TK].astype(BF16), ones(TK)], axis=0)
        for c in range(WINDOW // WCH):
            vwt_s[c] = jnp.zeros((HEAD_DIM + ONES_ROWS, WCH), BF16)
        for c in range(t_len // WCH):
            vwt_s[WINDOW // WCH + c] = jnp.concatenate([vwint_ref[0, :, c * WCH:(c + 1) * WCH].astype(BF16), ones(WCH)], axis=0)

    qb = qt_ref[0]
    qw = jnp.concatenate([qb[g * HEAD_DIM:(g + 1) * HEAD_DIM, :] for g in range(GROUP)], axis=1)
    frow = lax.broadcasted_iota(jnp.int32, (ALIBI_ROWS, cols), 0)
    fhead = lax.broadcasted_iota(jnp.int32, (ALIBI_ROWS, cols), 1) // TQ
    feat = jnp.zeros((ALIBI_ROWS, cols), F32)
    for g in range(GROUP):
        for r in range(7):
            feat = jnp.where((frow == r) & (fhead == g), tab_ref[(kh * GROUP + g) * 8 + r], feat)
    feat = feat.astype(BF16)
    q_base = jnp.concatenate([qw, feat, jnp.zeros((LANES - HEAD_DIM - ALIBI_ROWS, cols), BF16)], axis=0)
    t_row = t0 + lax.broadcasted_iota(jnp.int32, (1, cols), 1) % TQ

    def finish(acc):
        return acc[0:HEAD_DIM] * (1.0 / jnp.maximum(acc[HEAD_DIM:HEAD_DIM + 1], 1e-30))

    cend = lax.broadcasted_iota(jnp.int32, (n_seg, 1), 0) * CMP_STRIDE + (CMP_BLK - 1)
    ok_c = cend <= t_row
    s_c = jnp.where(ok_c, jnp.dot(kca_s[...], q_base, preferred_element_type=F32), NEG)
    m_c = jnp.max(s_c, axis=0, keepdims=True)
    p_c = jnp.where(ok_c, jnp.exp2(s_c - m_c), 0.0).astype(BF16)
    acc_c = jnp.dot(vct_s[...], p_c, preferred_element_type=F32)
    inv_c = 1.0 / jnp.maximum(acc_c[HEAD_DIM:HEAD_DIM + 1], 1e-30)
    o_cmp = acc_c[0:HEAD_DIM] * inv_c
    imp_c = jnp.dot(ovt_ref[...], p_c, preferred_element_type=F32) * inv_c
    imp = imp_c[:, 0:TQ]
    for g in range(1, GROUP):
        imp = imp + imp_c[:, g * TQ:(g + 1) * TQ]
    blk = lax.broadcasted_iota(jnp.int32, (HEAD_DIM, 1), 0)
    tq_row = t_row[:, 0:TQ]
    tb = lax.shift_right_logical(tq_row, int(math.log2(SEL_BLK)))
    forced = (blk == 0) | (blk == tb) | (blk == tb - 1)
    score = jnp.where(forced, FORCE_SCORE, jnp.where(blk * SEL_BLK <= tq_row, imp, -1.0))
    score = jnp.where(blk < n_blk, score, -2.0)
    chosen = _topk_mask_rows(score, min(N_SEL, n_blk)) & (score > -0.5)

    span = WINDOW + TQ
    k_w = kwa_s[pl.ds(pl.multiple_of(t0, TQ), span), :]
    v_w = jnp.concatenate([vwt_s[qi * (TQ // WCH) + c] for c in range(span // WCH)], axis=1)
    jj = lax.broadcasted_iota(jnp.int32, (TQ, 1), 0)
    ii = lax.broadcasted_iota(jnp.int32, (1, cols), 1) % TQ
    s_w = jnp.dot(k_w, q_base, preferred_element_type=F32)
    s_w = jnp.concatenate([jnp.where(jj >= ii, s_w[0:TQ], NEG), s_w[TQ:WINDOW],
                           jnp.where(jj <= ii, s_w[WINDOW:span], NEG)], axis=0)
    m_w = jnp.maximum(jnp.max(s_w, axis=0, keepdims=True), M_INIT)
    p_w = jnp.exp2((s_w - m_w).astype(BF16))
    o_win = finish(jnp.dot(v_w, p_w, preferred_element_type=F32))

    mrow = jnp.concatenate([jnp.where(chosen, 0.0, -BIG).astype(BF16)] * GROUP, axis=1)
    q_sel = jnp.concatenate([qw, mrow, feat, jnp.zeros((LANES - ALIBI_ROWS, cols), BF16)], axis=0)
    n_pairs = (t0 + TQ - 1) // (2 * TK) + 1
    max_pairs = t_len // (2 * TK)
    key_off = lax.broadcasted_iota(jnp.int32, (TK, 1), 0)

    def issue(j):
        for u in range(2):
            ks = (2 * j + u) * TK
            s = jnp.dot(ksa_s[ks:ks + TK, :], q_sel, preferred_element_type=F32)
            sc_s[j % 2, u] = jnp.where(ks + key_off <= t_row, s, NEG)

    def absorb(j):
        s0, s1 = sc_s[j % 2, 0], sc_s[j % 2, 1]
        m = m_s[0:1, :]
        m_new = jnp.maximum(m, jnp.maximum(jnp.max(s0, axis=0, keepdims=True), jnp.max(s1, axis=0, keepdims=True)))
        acc = jnp.exp2(m - m_new) * acc_s[...]
        for u, s in enumerate((s0, s1)):
            acc = acc + jnp.dot(vst_s[2 * j + u], jnp.exp2((s - m_new).astype(BF16)), preferred_element_type=F32)
        m_s[0:1, :] = m_new
        acc_s[...] = acc

    m_s[0:1, :] = jnp.full((1, cols), M_INIT, F32)
    acc_s[...] = jnp.zeros(acc_s.shape, F32)
    for n in range(1, max_pairs + 1):
        @pl.when(n_pairs == n)
        def _pairs(n=n):
            issue(0)
            for j in range(n):
                if j + 1 < n:
                    issue(j + 1)
                absorb(j)
    o_sel = finish(acc_s[...])

    for g in range(GROUP):
        sl = slice(g * TQ, (g + 1) * TQ)
        gate = lambda j: gt_ref[0, pl.ds(kh * 3 * GROUP + 3 * g + j, 1), :]
        mix = gate(0) * o_cmp[:, sl] + gate(1) * o_sel[:, sl] + gate(2) * o_win[:, sl]
        ot_ref[0, g * HEAD_DIM:(g + 1) * HEAD_DIM, :] = mix.astype(ot_ref.dtype)


def nsa_attention(layer, qt, gt, kvc, kn, kwn, kvt, wint):
    b, _, t = qt.shape
    n_seg = kvc.shape[3]
    n_blk = t // SEL_BLK
    assert t % (2 * TK) == 0 and n_blk <= HEAD_DIM and t >= WINDOW + TQ
    start = np.arange(n_seg) * CMP_STRIDE
    blk = np.arange(HEAD_DIM) * SEL_BLK
    ovt = jnp.asarray(((start[None, :] < blk[:, None] + SEL_BLK) & (start[None, :] + CMP_BLK > blk[:, None])
                       & (np.arange(HEAD_DIM)[:, None] < n_blk)), dtype=BF16)
    vrows = HEAD_DIM + ONES_ROWS
    grid_spec = pltpu.PrefetchScalarGridSpec(
        num_scalar_prefetch=1, grid=(b, N_KV_HEADS, t // TQ),
        in_specs=[
            pl.BlockSpec((1, GROUP * HEAD_DIM, TQ), lambda i, k, j, s: (i, k, j)),
            pl.BlockSpec((1, LANES, TQ), lambda i, k, j, s: (i, 0, j)),
            pl.BlockSpec((1, 1, 1, n_seg, HEAD_DIM), lambda i, k, j, s: (i, 0, k, 0, 0)),
            pl.BlockSpec((1, 1, 1, n_seg, HEAD_DIM), lambda i, k, j, s: (i, 1, k, 0, 0)),
            pl.BlockSpec((1, 1, 1, t, HEAD_DIM), lambda i, k, j, s: (i, 2, k, 0, 0)),
            pl.BlockSpec((1, 1, t, HEAD_DIM), lambda i, k, j, s: (i, k, 0, 0)),
            pl.BlockSpec((1, 1, HEAD_DIM, t), lambda i, k, j, s: (layer, i, 3 * N_KV_HEADS + k, 0)),
            pl.BlockSpec((1, HEAD_DIM, t), lambda i, k, j, s: (i, N_KV_HEADS + k, 0)),
            pl.BlockSpec((HEAD_DIM, n_seg), lambda i, k, j, s: (0, 0)),
        ],
        out_specs=pl.BlockSpec((1, GROUP * HEAD_DIM, TQ), lambda i, k, j, s: (i, k, j)),
        scratch_shapes=[pltpu.VMEM((t, 2 * LANES), BF16), pltpu.VMEM((WINDOW + t, LANES), BF16), pltpu.VMEM((n_seg, LANES), BF16),
                        pltpu.VMEM((t // TK, vrows, TK), BF16), pltpu.VMEM(((WINDOW + t) // WCH, vrows, WCH), BF16),
                        pltpu.VMEM((vrows, n_seg), BF16),
                        pltpu.VMEM((2, 2, TK, GROUP * TQ), F32), pltpu.VMEM((8, GROUP * TQ), F32),
                        pltpu.VMEM((vrows, GROUP * TQ), F32)])
    return pl.pallas_call(
        functools.partial(_nsa_attn_kernel, t_len=t, n_seg=n_seg, n_blk=n_blk), grid_spec=grid_spec,
        out_shape=jax.ShapeDtypeStruct((b, ATT_WIDTH, t), BF16),
        compiler_params=_cparams("parallel", "parallel", "arbitrary"), name="nsa_attn",
    )(_alibi_table(), qt, gt, kvc, kvc, kn, kwn, kvt, wint, ovt)


def _nsa_out_t_kernel(ot_ref, zt_ref, x_ref, w_ref, y_ref):
    a_t = (ot_ref[0].astype(F32) * _silu(zt_ref[0].astype(F32))).astype(BF16)
    y_ref[0] = x_ref[0] + lax.dot_general(a_t, w_ref[...], TN, preferred_element_type=F32)


def nsa_out_t(ot, zt, x, w_out, *, tm):
    b, t, d = x.shape
    tok = pl.BlockSpec((1, ATT_WIDTH, tm), lambda i, j: (i, 0, j))
    row = pl.BlockSpec((1, tm, d), lambda i, j: (i, j, 0))
    return pl.pallas_call(
        _nsa_out_t_kernel, grid=(b, t // tm),
        in_specs=[tok, tok, row, pl.BlockSpec(w_out.shape, lambda i, j: (0, 0))],
        out_specs=row, out_shape=jax.ShapeDtypeStruct((b, t, d), F32),
        compiler_params=_cparams("parallel", "parallel"), name="nsa_out",
    )(ot, zt, x, w_out)


def nsa_prompt_layer_t(layer, n_layers, x, kvt_all, nw, w_in, w_out, pe, w1, b1, w2):
    b, t, d = x.shape
    qt, kvt_all, wint, gt, zt, kn, kwn = nsa_in_proj(layer, n_layers, x, nw, w_in, kvt_all, tm=256)
    n_seg = t // CMP_STRIDE
    kvc = compress_tokens(kn[:, 0:2].reshape(b, 2, N_KV_HEADS, n_seg, CMP_STRIDE * HEAD_DIM), pe, w1, b1, w2)
    ot = nsa_attention(layer, qt, gt, kvc, kn, kwn, kvt_all, wint)
    y = nsa_out_t(ot, zt, x, w_out.astype(BF16), tm=256)
    wr = min(WINDOW, t)
    win5 = wint[:, :, t - wr:].reshape(b, 2, N_KV_HEADS, HEAD_DIM, wr).transpose(0, 4, 1, 2, 3)
    return y, kvt_all, win5


def _split_nsa_w_in(w_in, dtype):
    o1 = ATT_WIDTH
    o2 = o1 + 4 * KV_WIDTH
    o3 = o2 + 2 * KV_WIDTH
    o4 = o3 + 3 * N_HEADS
    wg = jnp.pad(w_in[:, o3:o4], ((0, 0), (0, LANES - 3 * N_HEADS)))
    return [w.astype(dtype) for w in (w_in[:, :o1], w_in[:, o1:o2], w_in[:, o2:o3], wg, w_in[:, o4:])]


NSA_ACTS = (None, None, None, "sigmoid", None)


def nsa_prompt_layer(x, nw, w_in, w_out, pe, w1, b1, w2):
    b, t, d = x.shape
    n_blk = t // SEL_BLK
    assert n_blk <= HEAD_DIM
    q, kv, win, gates, z = rms_proj(x.reshape(b * t, d), nw, _split_nsa_w_in(w_in, BF16), NSA_ACTS, tm=256, name="nsa_in_proj")
    q = q.reshape(b, t, ATT_WIDTH)
    kv5 = kv.reshape(b, t, 4, N_KV_HEADS, HEAD_DIM)
    win5 = win.reshape(b, t, 2, N_KV_HEADS, HEAD_DIM)
    kvt = kv5.transpose(0, 2, 3, 1, 4).astype(BF16)
    wint = win5.transpose(0, 2, 3, 1, 4).astype(BF16)
    gates_r = gates[:, :3 * N_HEADS].reshape(b, t, N_KV_HEADS, 3 * GROUP).transpose(0, 2, 1, 3)
    slopes = _alibi_slopes()
    n_seg = t // CMP_STRIDE
    kvc = compress_tokens(kvt[:, 0:2].reshape(b, 2, N_KV_HEADS, n_seg, CMP_STRIDE * HEAD_DIM), pe, w1, b1, w2)
    o_c, sel = cmp_select(q, kvc, gates_r, slopes, _cmp_to_sel_matrix(n_seg, n_blk), tq=128)
    onehot = (jnp.arange(t)[:, None] // SEL_BLK == jnp.arange(HEAD_DIM)[None, :]).astype(BF16)
    k_aug = jnp.concatenate([kvt[:, 2], jnp.broadcast_to(onehot, (b, N_KV_HEADS, t, HEAD_DIM))], axis=-1)
    o_s = sel_attention(q, sel, k_aug, kvt[:, 3], gates_r, slopes, tq=128, tk=512)
    o_w = win_attention(q, wint[:, 0], wint[:, 1], gates_r, slopes, tq=128)
    r2 = lambda a: a.reshape(b * t, -1)
    y = nsa_out(r2(o_c), r2(o_s), r2(o_w), z, r2(x), w_out.astype(BF16), tm=256)
    wr = min(WINDOW, t)
    return y.reshape(b, t, d), kv5, win5[:, t - wr:]


def _log_sigmoid(x):
    return jnp.minimum(x, 0.0) - jnp.log(1.0 + jnp.exp(-jnp.abs(x)))


def _mconv_body(shifted, xm, cw_ref, cb_ref, wbd_ref, wg_ref, bg_ref, q_ref, k_ref, v_ref, c_ref, g_ref, exact):
    conv = cb_ref[...]
    for j in range(CONV_W):
        conv = conv + shifted[j] * cw_ref[j:j + 1, :]
    c = _silu(conv)
    c_ref[...] = c.astype(c_ref.dtype)
    cast = (lambda a: a) if exact else (lambda a: a.astype(BF16))
    kw = dict(preferred_element_type=F32, precision=HI) if exact else dict(preferred_element_type=F32)
    gpre = bg_ref[...]
    for m, (src, dst) in enumerate(((c, q_ref), (c, k_ref), (xm, v_ref))):
        for gi in range(D_INNER // LANES):
            sl = slice(gi * LANES, (gi + 1) * LANES)
            y = jnp.dot(cast(src[:, sl]), wbd_ref[m, gi], **kw)
            gpre = gpre + jnp.dot(cast(y), wg_ref[m * D_INNER + gi * LANES:m * D_INNER + (gi + 1) * LANES, :], **kw)
            dst[:, sl] = (y * (M_HEAD_DIM ** -0.5) if m == 1 else y).astype(dst.dtype)
    lane = lax.broadcasted_iota(jnp.int32, gpre.shape, 1)
    g_ref[...] = jnp.where(lane < M_HEADS, gpre, _log_sigmoid(gpre))


def _mconv_prompt_kernel(xm_ref, halo_ref, cw_ref, cb_ref, wbd_ref, wg_ref, bg_ref, q_ref, k_ref, v_ref, c_ref, g_ref, *, tm):
    xm = xm_ref[0]
    halo = jnp.where(pl.program_id(1) == 0, 0.0, halo_ref[0])
    ext = jnp.concatenate([halo, xm], axis=0)
    shifted = [ext[5 + j:5 + j + tm] for j in range(CONV_W - 1)] + [xm]
    _mconv_body(shifted, xm, cw_ref, cb_ref, wbd_ref, wg_ref, bg_ref, q_ref.at[0], k_ref.at[0], v_ref.at[0], c_ref.at[0],
                g_ref.at[0], False)


def _mlstm_small_weights(w_qkv, w_gate, b_gate, dtype):
    nb = LANES // QKV_BLK
    w = w_qkv.reshape(3, D_INNER // LANES, nb, QKV_BLK, QKV_BLK)
    eye = jnp.eye(nb, dtype=w.dtype)
    wbd = jnp.einsum("mgnji,nk->mgnjki", w, eye).reshape(3, D_INNER // LANES, LANES, LANES)
    wg = jnp.pad(w_gate, ((0, 0), (0, LANES - 2 * M_HEADS)))
    bg = jnp.pad(b_gate, (0, LANES - 2 * M_HEADS)).reshape(1, LANES)
    return wbd.astype(dtype), wg.astype(dtype), bg


def mconv_prompt(xm, conv_w, conv_b, w_qkv, w_gate, b_gate, *, tm):
    b, t, _ = xm.shape
    wbd, wg, bg = _mlstm_small_weights(w_qkv, w_gate, b_gate, BF16)
    row = pl.BlockSpec((1, tm, D_INNER), lambda i, j: (i, j, 0))
    full = lambda a: pl.BlockSpec(a.shape, lambda i, j: (0,) * a.ndim)
    cb = conv_b.reshape(1, D_INNER)
    return pl.pallas_call(
        functools.partial(_mconv_prompt_kernel, tm=tm), grid=(b, t // tm),
        in_specs=[row, pl.BlockSpec((1, 8, D_INNER), lambda i, j: (i, jnp.maximum(j * (tm // 8) - 1, 0), 0)),
                  full(conv_w), full(cb), full(wbd), full(wg), full(bg)],
        out_specs=[row, row, row, row, pl.BlockSpec((1, tm, LANES), lambda i, j: (i, j, 0))],
        out_shape=[jax.ShapeDtypeStruct((b, t, D_INNER), BF16)] * 4 + [jax.ShapeDtypeStruct((b, t, LANES), F32)],
        compiler_params=_cparams("parallel", "parallel"), name="mlstm_conv_qkv",
    )(xm, xm, conv_w, cb, wbd, wg, bg)


def _mlstm_cell_kernel(q_ref, k_ref, v_ref, gc_ref, gr_ref, c0_ref, n0_ref, m0_ref, h_ref, cf_ref, nf_ref, mf_ref,
                       c_s, n_s, m_s, *, chunk, n_chunks):
    ci = pl.program_id(2)

    @pl.when(ci == 0)
    def _():
        c_s[...] = c0_ref[0, 0]
        n_s[...] = n0_ref[0, 0]
        m_s[...] = m0_ref[0, 0]

    q, k, v = q_ref[0], k_ref[0], v_ref[0]
    icol, fcol = gc_ref[0, 0, :, 0:1], gc_ref[0, 0, :, 1:2]
    irow, frow = gr_ref[0, 0, 0:1, :], gr_ref[0, 0, 1:2, :]
    ri = lax.broadcasted_iota(jnp.int32, (chunk, chunk), 0)
    cj = lax.broadcasted_iota(jnp.int32, (chunk, chunk), 1)
    causal = cj <= ri
    b_col = jnp.dot(jnp.where(causal, 1.0, 0.0), jnp.broadcast_to(fcol, (chunk, LANES)),
                    preferred_element_type=F32, precision=HI)[:, 0:1]
    b_row = jnp.dot(jnp.broadcast_to(frow, (8, chunk)), jnp.where(ri <= cj, 1.0, 0.0),
                    preferred_element_type=F32, precision=HI)[0:1, :]
    dmat = jnp.where(causal, b_col - b_row + irow, -jnp.inf)
    m_prev = m_s[0:1, 0:1]
    inter = b_col + m_prev
    mt = jnp.maximum(jnp.max(dmat, axis=-1, keepdims=True), inter)
    qb, kb, vb = q.astype(BF16), k.astype(BF16), v.astype(BF16)
    s = lax.dot_general(qb, kb, NT, preferred_element_type=F32) * jnp.exp(dmat - mt)
    decay = jnp.exp(inter - mt)
    num = (jnp.dot(s.astype(BF16), vb, preferred_element_type=F32)
           + decay * jnp.dot(qb, c_s[...].astype(BF16), preferred_element_type=F32))
    den = jnp.sum(s, axis=-1, keepdims=True) + decay * jnp.sum(q * n_s[...], axis=-1, keepdims=True)
    hc = num / jnp.maximum(jnp.abs(den), jnp.exp(-mt))
    h_ref[0] = (hc * lax.rsqrt(jnp.mean(hc * hc, axis=-1, keepdims=True) + RMS_EPS)).astype(h_ref.dtype)
    m_new = mt[chunk - 1:chunk, :]
    b_last = b_col[chunk - 1:chunk, :]
    kw = k * jnp.exp(b_last - b_col + icol - m_new)
    carry = jnp.exp(b_last + m_prev - m_new)
    c_s[...] = carry * c_s[...] + lax.dot_general(kw.astype(BF16), vb, TN, preferred_element_type=F32)
    n_s[...] = carry * n_s[...] + jnp.sum(kw, axis=0, keepdims=True)
    m_s[...] = jnp.broadcast_to(m_new, m_s.shape)

    @pl.when(ci == n_chunks - 1)
    def _():
        cf_ref[0, 0] = c_s[...]
        nf_ref[0, 0] = n_s[...]
        mf_ref[0, 0] = m_s[...]


def mlstm_cell(q, k, v, gates, c0, n0, m0, *, chunk):
    b, t, _ = q.shape
    n_chunks = t // chunk
    g_col = jnp.stack([gates[..., :M_HEADS], gates[..., M_HEADS:2 * M_HEADS]], axis=-1).transpose(0, 2, 1, 3)
    g_row = g_col.transpose(0, 1, 3, 2)
    m0b = jnp.broadcast_to(m0[:, :, None, None], (b, M_HEADS, 8, LANES))
    n0r = n0.reshape(b, M_HEADS, 1, M_HEAD_DIM)
    head = pl.BlockSpec((1, chunk, M_HEAD_DIM), lambda i, h, c: (i, c, h))
    st = lambda *blk: pl.BlockSpec((1, 1) + blk, lambda i, h, c: (i, h, 0, 0))
    hn, cf, nf, mf = pl.pallas_call(
        functools.partial(_mlstm_cell_kernel, chunk=chunk, n_chunks=n_chunks), grid=(b, M_HEADS, n_chunks),
        in_specs=[head, head, head,
                  pl.BlockSpec((1, 1, chunk, 2), lambda i, h, c: (i, h, c, 0)),
                  pl.BlockSpec((1, 1, 2, chunk), lambda i, h, c: (i, h, 0, c)),
                  st(M_HEAD_DIM, M_HEAD_DIM), st(1, M_HEAD_DIM), st(8, LANES)],
        out_specs=[head, st(M_HEAD_DIM, M_HEAD_DIM), st(1, M_HEAD_DIM), st(8, LANES)],
        out_shape=[jax.ShapeDtypeStruct((b, t, D_INNER), BF16), jax.ShapeDtypeStruct((b, M_HEADS, M_HEAD_DIM, M_HEAD_DIM), F32),
                   jax.ShapeDtypeStruct((b, M_HEADS, 1, M_HEAD_DIM), F32), jax.ShapeDtypeStruct((b, M_HEADS, 8, LANES), F32)],
        scratch_shapes=[pltpu.VMEM((M_HEAD_DIM, M_HEAD_DIM), F32), pltpu.VMEM((1, M_HEAD_DIM), F32), pltpu.VMEM((8, LANES), F32)],
        compiler_params=_cparams("parallel", "parallel", "arbitrary"), name="mlstm_cell",
    )(q, k, v, g_col, g_row, c0, n0r, m0b)
    return hn, cf, nf.reshape(b, M_HEADS, M_HEAD_DIM), mf[:, :, 0, 0]


def _mlstm_out_kernel(hn_ref, c_ref, z_ref, x_ref, nw_ref, sk_ref, w_ref, fw_ref, y_ref, *, final, exact):
    a = (hn_ref[...].astype(F32) * nw_ref[...] + sk_ref[...] * c_ref[...].astype(F32)) * _silu(z_ref[...].astype(F32))
    if exact:
        y = x_ref[...] + jnp.dot(a, w_ref[...], preferred_element_type=F32, precision=HI)
    else:
        y = x_ref[...] + jnp.dot(a.astype(BF16), w_ref[...], preferred_element_type=F32)
    y_ref[...] = _rmsnorm(y, fw_ref[...]) if final else y


def mlstm_out(hn, c, z, x, norm_w, skip, w_out, final_w, *, tm, final, exact=False):
    m, d = x.shape
    wide = pl.BlockSpec((tm, D_INNER), lambda i: (i, 0))
    row = pl.BlockSpec((tm, d), lambda i: (i, 0))
    vec = lambda n: pl.BlockSpec((1, n), lambda i: (0, 0))
    return pl.pallas_call(
        functools.partial(_mlstm_out_kernel, final=final, exact=exact), grid=(m // tm,),
        in_specs=[wide, wide, wide, row, vec(D_INNER), vec(D_INNER), pl.BlockSpec(w_out.shape, lambda i: (0, 0)), vec(d)],
        out_specs=row, out_shape=jax.ShapeDtypeStruct((m, d), F32),
        compiler_params=_cparams("parallel"), name="mlstm_out",
    )(hn, c, z, x, norm_w.reshape(1, D_INNER), skip.reshape(1, D_INNER), w_out, final_w.reshape(1, d))


def mlstm_prompt_layer(x, nw, w_in, conv_w, conv_b, w_qkv, w_gate, b_gate, norm_w, skip, w_out, final_w, *, final):
    b, t, d = x.shape
    w_in = w_in.astype(BF16)
    xm, z = rms_proj(x.reshape(b * t, d), nw, [w_in[:, :D_INNER], w_in[:, D_INNER:]], (None, None), tm=256, out_dtypes=(F32, BF16),
                     name="mlstm_in_proj")
    xm3 = xm.reshape(b, t, D_INNER)
    q, k, v, c, gates = mconv_prompt(xm3, conv_w, conv_b, w_qkv, w_gate, b_gate, tm=256)
    c0 = jnp.zeros((b, M_HEADS, M_HEAD_DIM, M_HEAD_DIM), F32)
    n0 = jnp.zeros((b, M_HEADS, M_HEAD_DIM), F32)
    m0 = jnp.full((b, M_HEADS), -jnp.inf, F32)
    hn, cf, nf, mf = mlstm_cell(q, k, v, gates, c0, n0, m0, chunk=min(256, t))
    y = mlstm_out(hn.reshape(b * t, D_INNER), c.reshape(b * t, D_INNER), z, x.reshape(b * t, d), norm_w, skip,
                  w_out.astype(BF16), final_w, tm=256, final=final)
    return y.reshape(b, t, d), cf, nf, mf, xm3[:, t - (CONV_W - 1):]


def _mconv_sample_kernel(xm_ref, hist_ref, cw_ref, cb_ref, wbd_ref, wg_ref, bg_ref, q_ref, k_ref, v_ref, c_ref, g_ref):
    xm = xm_ref[...]
    shifted = [hist_ref[j] for j in range(CONV_W - 1)] + [xm]
    _mconv_body(shifted, xm, cw_ref, cb_ref, wbd_ref, wg_ref, bg_ref, q_ref, k_ref, v_ref, c_ref, g_ref, True)


def mconv_sample(xm, hist, conv_w, conv_b, w_qkv, w_gate, b_gate):
    b = xm.shape[0]
    wbd, wg, bg = _mlstm_small_weights(w_qkv, w_gate, b_gate, F32)
    cb = conv_b.reshape(1, D_INNER)
    full = lambda a: pl.BlockSpec(a.shape, lambda i: (0,) * a.ndim)
    args = (xm, hist, conv_w, cb, wbd, wg, bg)
    row = pl.BlockSpec((b, D_INNER), lambda i: (0, 0))
    return pl.pallas_call(
        _mconv_sample_kernel, grid=(1,), in_specs=[full(a) for a in args],
        out_specs=[row, row, row, row, pl.BlockSpec((b, LANES), lambda i: (0, 0))],
        out_shape=[jax.ShapeDtypeStruct((b, D_INNER), F32)] * 4 + [jax.ShapeDtypeStruct((b, LANES), F32)],
        compiler_params=_cparams("arbitrary"), name="mlstm_conv_qkv_sample",
    )(*args)


def _mlstm_step_kernel(q_ref, k_ref, v_ref, g_ref, c0_ref, n0_ref, *rest):
    h_ref, cf_ref, nf_ref, mf_ref = rest[-4:]
    q, k, v = q_ref[0], k_ref[0], v_ref[0]
    g = g_ref[0, 0]
    ig, fl, m0 = g[:, 0:1], g[:, 1:2], g[:, 2:3]
    c0, n0 = c0_ref[0, 0, 0], n0_ref[0, 0]
    m_new = jnp.maximum(fl + m0, ig)
    decay = jnp.exp(fl + m0 - m_new)
    sw = jnp.exp(ig - m_new)
    s = jnp.sum(q * k, axis=-1, keepdims=True) * sw
    half = LANES // 2
    qk_col = jnp.concatenate([jnp.broadcast_to(q, (half, M_HEAD_DIM)), jnp.broadcast_to(k, (half, M_HEAD_DIM))], axis=0).T
    q_col, k_col = qk_col[:, 0:1], qk_col[:, half:half + 1]
    qc = jnp.sum(q_col * c0, axis=0, keepdims=True)
    num = s * v + decay * qc
    den = s + decay * jnp.sum(q * n0, axis=-1, keepdims=True)
    hc = num / jnp.maximum(jnp.abs(den), jnp.exp(-m_new))
    h_ref[0] = hc * lax.rsqrt(jnp.mean(hc * hc, axis=-1, keepdims=True) + RMS_EPS)
    cf_ref[0, 0, 0] = decay * c0 + (k_col * sw) * v
    nf_ref[0, 0] = decay * n0 + sw * k
    mf_ref[0, 0] = jnp.broadcast_to(m_new, (1, LANES))


def mlstm_step(layer, q, k, v, gates, c_all, n0, m0, c_new_all):
    b = q.shape[0]
    gsm = jnp.stack([gates[:, :M_HEADS], gates[:, M_HEADS:2 * M_HEADS], m0], axis=-1)
    gsm = jnp.pad(gsm, ((0, 0), (0, 0), (0, LANES - 3))).reshape(b, M_HEADS, 1, LANES)
    r3 = lambda a: a.reshape(b, 1, D_INNER)
    head = pl.BlockSpec((1, 1, M_HEAD_DIM), lambda i, h: (i, 0, h))
    st = lambda *blk: pl.BlockSpec((1, 1) + blk, lambda i, h: (i, h, 0, 0))
    c_blk = pl.BlockSpec((1, 1, 1, M_HEAD_DIM, M_HEAD_DIM), lambda i, h: (layer, i, h, 0, 0))
    args = [r3(q), r3(k), r3(v), gsm, c_all, n0.reshape(b, M_HEADS, 1, M_HEAD_DIM)]
    in_specs = [head, head, head, st(1, LANES), c_blk, st(1, M_HEAD_DIM)]
    aliases = {}
    if c_new_all is not None:
        args.append(c_new_all)
        in_specs.append(pl.BlockSpec(memory_space=pl.ANY))
        aliases = {len(args) - 1: 1}
    hn, cf, nf, mf = pl.pallas_call(
        _mlstm_step_kernel, grid=(b, M_HEADS), in_specs=in_specs,
        out_specs=[head, c_blk, st(1, M_HEAD_DIM), st(1, LANES)],
        out_shape=[jax.ShapeDtypeStruct((b, 1, D_INNER), F32), jax.ShapeDtypeStruct(c_all.shape, F32),
                   jax.ShapeDtypeStruct((b, M_HEADS, 1, M_HEAD_DIM), F32), jax.ShapeDtypeStruct((b, M_HEADS, 1, LANES), F32)],
        input_output_aliases=aliases,
        compiler_params=_cparams("parallel", "parallel"), name="mlstm_step",
    )(*args)
    return hn.reshape(b, D_INNER), cf, nf.reshape(b, M_HEADS, M_HEAD_DIM), mf[:, :, 0, 0]


def mlstm_sample_layer(layer, x, conv_state, c_all, c_new_all, n0, m0, nw, w_in, conv_w, conv_b, w_qkv, w_gate, b_gate, norm_w,
                       skip, w_out, final_w, *, final):
    b = x.shape[0]
    xm, z = rms_proj(x, nw, [w_in[:, :D_INNER], w_in[:, D_INNER:]], (None, None), tm=b, exact=True, name="mlstm_in_proj_sample")
    hist = conv_state.transpose(1, 0, 2)
    q, k, v, c, gates = mconv_sample(xm, hist, conv_w, conv_b, w_qkv, w_gate, b_gate)
    hn, cf, nf, mf = mlstm_step(layer, q, k, v, gates, c_all, n0, m0, c_new_all)
    y = mlstm_out(hn, c, z, x, norm_w, skip, w_out, final_w, tm=b, final=final, exact=True)
    conv_new = jnp.concatenate([hist[1:], xm[None]], axis=0).transpose(1, 0, 2)
    return y, cf, nf, mf, conv_new


SAMPLE_ROWS = 8
PAGES_PER_STEP = 16
N_BLK_PAD = 256


def _row_scalars(vals):
    row = lax.broadcasted_iota(jnp.int32, (SAMPLE_ROWS, 1), 0)
    col = jnp.zeros((SAMPLE_ROWS, 1), F32)
    for g, v in enumerate(vals):
        col = jnp.where(row == g, v, col)
    return col


def _nsa_sample_kernel(pt_ref, slopes_ref, *refs, gp, n_groups, past_len):
    cmp_pages, sel_pages = refs[:gp], refs[gp:2 * gp]
    (q_ref, qbd_ref, kvn_ref, cwin_ref, wn_ref, g_ref, wp_ref, pe_ref, w1f_ref, b1_ref, w2_ref, ov_ref, ex_ref, o_ref,
     cmp_s, kc_s, vc_s, bias_s, oc_s, m_s, l_s, acc_s) = refs[2 * gp:]
    step = pl.program_id(1)
    t = past_len
    n_seg = past_len // CMP_STRIDE
    n_blk = past_len // SEL_BLK + 1
    span = gp * PAGE_SIZE
    slope_cols = [_row_scalars([slopes_ref[kh * GROUP + g] for g in range(GROUP)]) for kh in range(N_KV_HEADS)]

    @pl.when(step < n_groups)
    def _stash():
        for i in range(gp):
            row0 = pl.multiple_of((step * gp + i) * PAGE_SIZE, PAGE_SIZE)
            for c in range(2):
                for hp in range(N_KV_HEADS // 2):
                    a = cmp_pages[i][0, 0, c, 2 * hp:2 * hp + 2].reshape(2 * HEAD_DIM, PAGE_SIZE)
                    cmp_s[c * 2 + hp, pl.ds(row0, PAGE_SIZE), :] = a.T

    @pl.when(step == n_groups - 1)
    def _compress_and_select():
        for slab in range(4):
            c, hp = divmod(slab, 2)
            x = jnp.concatenate([cmp_s[slab, pl.ds(r, n_seg, stride=CMP_STRIDE), :] for r in range(CMP_STRIDE)], axis=1)
            part = jnp.dot(x.astype(BF16), wp_ref[c], preferred_element_type=F32)
            hidc = jnp.dot(pe_ref[c], w1f_ref[c], preferred_element_type=F32)[0:1] + b1_ref[c]
            for hh in range(2):
                p0 = part[:, hh * 2 * CMP_HID:hh * 2 * CMP_HID + CMP_HID]
                p1 = part[:, hh * 2 * CMP_HID + CMP_HID:(hh + 1) * 2 * CMP_HID]
                hid = hidc + p0 + pltpu.roll(p1, n_seg - 1, 0)
                tok = jnp.dot(_silu(hid).astype(BF16), w2_ref[c], preferred_element_type=F32)
                if c == 0:
                    kc_s[2 * hp + hh] = tok
                else:
                    vc_s[2 * hp + hh] = tok
        cend = lax.broadcasted_iota(jnp.int32, (1, n_seg), 1) * CMP_STRIDE + (CMP_BLK - 1)
        d_c = (t - cend).astype(F32)
        ok = d_c >= 0.0
        row = lax.broadcasted_iota(jnp.int32, (SAMPLE_ROWS, 1), 0)
        blk = lax.broadcasted_iota(jnp.int32, (1, N_BLK_PAD), 1)
        tb = t // SEL_BLK
        forced = (blk == 0) | (blk == tb) | (blk == tb - 1)
        ii = lax.broadcasted_iota(jnp.int32, (N_BLK_PAD, N_BLK_PAD), 0)
        jj = lax.broadcasted_iota(jnp.int32, (N_BLK_PAD, N_BLK_PAD), 1)
        sel_rows = []
        for kh in range(N_KV_HEADS):
            q8 = (q_ref[0, kh] * Q_SCALE).astype(BF16)
            s = lax.dot_general(q8, kc_s[kh].astype(BF16), NT, preferred_element_type=F32) - slope_cols[kh] * d_c
            s = jnp.where(ok, s, NEG)
            m = jnp.max(s, axis=-1, keepdims=True)
            p = jnp.where(ok, jnp.exp(s - m), 0.0)
            pn = (p / jnp.maximum(jnp.sum(p, axis=-1, keepdims=True), 1e-30)).astype(BF16)
            oc_s[kh] = jnp.dot(pn, vc_s[kh].astype(BF16), preferred_element_type=F32)
            pn_heads = jnp.where(row < GROUP, pn, jnp.zeros_like(pn))
            imp = jnp.sum(jnp.dot(pn_heads, ov_ref[...], preferred_element_type=F32), axis=0, keepdims=True)
            score = jnp.where(forced, FORCE_SCORE, jnp.where(blk * SEL_BLK <= t, imp, -1.0))
            score = jnp.where(blk < n_blk, score, -2.0)
            col = jnp.broadcast_to(score, (SAMPLE_ROWS, N_BLK_PAD)).T[:, 0:1]
            beats = (col > score) | ((ii < jj) & (col == score))
            cnt = jnp.sum(jnp.where(beats, 1.0, 0.0), axis=0, keepdims=True)
            sel = jnp.where((cnt < float(min(N_SEL, n_blk))) & (score > -0.5), 1.0, 0.0)
            sel_rows.append(jnp.broadcast_to(sel, (SAMPLE_ROWS, N_BLK_PAD)))
        picked = jnp.dot(jnp.concatenate(sel_rows, axis=0).astype(BF16), ex_ref[...], preferred_element_type=F32)
        key_pos = lax.broadcasted_iota(jnp.int32, (1, past_len), 1)
        bias = jnp.where(picked > 0.5, jnp.concatenate(slope_cols, axis=0) * (key_pos - t).astype(F32), NEG)
        for gi in range(n_groups):
            bias_s[gi] = bias[:, gi * span:(gi + 1) * span]
        m_s[...] = jnp.full(m_s.shape, M_INIT, F32)
        l_s[...] = jnp.zeros(l_s.shape, F32)
        acc_s[...] = jnp.zeros(acc_s.shape, F32)

    @pl.when(step >= n_groups)
    def _selected():
        gb = step - n_groups
        kk = jnp.concatenate([sel_pages[i][0, 0, 0].reshape(KV_WIDTH, PAGE_SIZE) for i in range(gp)], axis=1).astype(BF16)
        vv = jnp.concatenate([sel_pages[i][0, 0, 1].reshape(KV_WIDTH, PAGE_SIZE) for i in range(gp)], axis=1).astype(BF16)
        s = jnp.dot(qbd_ref[0], kk, preferred_element_type=F32) + bias_s[gb]
        m_old = m_s[:, 0:1]
        m_new = jnp.maximum(m_old, jnp.max(s, axis=-1, keepdims=True))
        alpha = jnp.exp(m_old - m_new)
        p = jnp.exp(s - m_new)
        l_s[...] = jnp.broadcast_to(alpha * l_s[:, 0:1] + jnp.sum(p, axis=-1, keepdims=True), l_s.shape)
        acc_s[...] = alpha * acc_s[...] + lax.dot_general(p.astype(BF16), vv, NT, preferred_element_type=F32)
        m_s[...] = jnp.broadcast_to(m_new, m_s.shape)

    @pl.when(step == 2 * n_groups - 1)
    def _finish():
        wr = cwin_ref.shape[-1]
        d_w = wr - lax.broadcasted_iota(jnp.int32, (1, wr), 1)
        ok_w = (d_w <= WINDOW) & (t - d_w >= 0)
        for kh in range(N_KV_HEADS):
            q8 = q_ref[0, kh] * Q_SCALE
            rows = slice(kh * SAMPLE_ROWS, (kh + 1) * SAMPLE_ROWS)
            s_n = jnp.sum(q8 * kvn_ref[0, 2, kh], axis=-1, keepdims=True)
            m_old = m_s[rows, 0:1]
            m_new = jnp.maximum(m_old, s_n)
            alpha = jnp.exp(m_old - m_new)
            p_n = jnp.exp(s_n - m_new)
            l = alpha * l_s[rows, 0:1] + p_n
            acc = acc_s[rows, kh * HEAD_DIM:(kh + 1) * HEAD_DIM]
            o_sel = (alpha * acc + p_n * kvn_ref[0, 3, kh]) / jnp.maximum(l, 1e-30)
            s_w = jnp.dot(q8.astype(BF16), cwin_ref[0, 0, 0, kh].astype(BF16), preferred_element_type=F32)
            s_w = jnp.where(ok_w, s_w - slope_cols[kh] * d_w.astype(F32), NEG)
            s_wn = jnp.sum(q8 * wn_ref[0, 0, kh], axis=-1, keepdims=True)
            m_w = jnp.maximum(jnp.max(s_w, axis=-1, keepdims=True), s_wn)
            p_w = jnp.where(ok_w, jnp.exp(s_w - m_w), 0.0)
            p_wn = jnp.exp(s_wn - m_w)
            l_w = jnp.sum(p_w, axis=-1, keepdims=True) + p_wn
            o_win = (lax.dot_general(p_w.astype(BF16), cwin_ref[0, 0, 1, kh].astype(BF16), NT, preferred_element_type=F32)
                     + p_wn * wn_ref[0, 1, kh]) / jnp.maximum(l_w, 1e-30)
            gts = g_ref[0, kh]
            o_ref[0, kh] = gts[:, 0:1] * oc_s[kh] + gts[:, 1:2] * o_sel + gts[:, 2:3] * o_win


def _pair_w1(w1):
    wr = w1.reshape(2, CMP_R, CMP_STRIDE, HEAD_DIM, CMP_HID).transpose(0, 2, 3, 1, 4)
    wp = jnp.einsum("crdje,hk->crhdkje", wr, jnp.eye(2, dtype=w1.dtype))
    return wp.reshape(2, CMP_STRIDE * 2 * HEAD_DIM, 2 * CMP_R * CMP_HID).astype(BF16)


def nsa_sample_attention(layer, q, kv, win, gates, cache_kv, cache_win, page_table, pe, w1, b1, w2):
    b = q.shape[0]
    n_pages = page_table.shape[1]
    past_len = n_pages * PAGE_SIZE
    gp = PAGES_PER_STEP
    assert n_pages % gp == 0 and cache_win.shape[2] == WINDOW and past_len // SEL_BLK + 1 <= N_BLK_PAD
    n_groups = n_pages // gp
    n_seg = past_len // CMP_STRIDE
    pad_rows = lambda a: jnp.pad(a, ((0, 0), (0, 0), (0, SAMPLE_ROWS - GROUP), (0, 0)))
    q4 = pad_rows(q.reshape(b, N_KV_HEADS, GROUP, HEAD_DIM))
    g4 = pad_rows(gates[:, :3 * N_HEADS].reshape(b, N_KV_HEADS, GROUP, 3))
    q_bd = jnp.einsum("bkgd,kj->bkgjd", q4 * Q_SCALE, jnp.eye(N_KV_HEADS, dtype=F32))
    q_bd = q_bd.reshape(b, N_KV_HEADS * SAMPLE_ROWS, KV_WIDTH).astype(BF16)
    kvn = kv.reshape(b, 4, N_KV_HEADS, 1, HEAD_DIM)
    wn = win.reshape(b, 2, N_KV_HEADS, 1, HEAD_DIM)
    cache_t = cache_kv.transpose(0, 1, 3, 4, 5, 2)
    cwin_t = cache_win.transpose(0, 1, 3, 4, 5, 2)
    start = np.arange(n_seg) * CMP_STRIDE
    blk = np.arange(N_BLK_PAD) * SEL_BLK
    ov = jnp.asarray((start[:, None] < blk[None, :] + SEL_BLK) & (start[:, None] + CMP_BLK > blk[None, :]), dtype=BF16)
    pe8 = jnp.broadcast_to(pe.reshape(2, 1, CMP_BLK * HEAD_DIM), (2, 8, CMP_BLK * HEAD_DIM)).astype(BF16)
    w1f = w1.reshape(2, CMP_BLK * HEAD_DIM, CMP_HID).astype(BF16)
    expand = (jnp.arange(N_BLK_PAD)[:, None] == jnp.arange(past_len)[None, :] // SEL_BLK).astype(BF16)
    consts = (_pair_w1(w1), pe8, w1f, b1.reshape(2, 1, CMP_HID), w2.astype(BF16), ov, expand)
    page_blk = (1, 1, 2, N_KV_HEADS, HEAD_DIM, PAGE_SIZE)

    def cmp_map(i):
        return lambda bi, s, pt, sl: (layer, pt[bi * n_pages + jnp.minimum(s, n_groups - 1) * gp + i], 0, 0, 0, 0)

    def sel_map(i):
        return lambda bi, s, pt, sl: (layer, pt[bi * n_pages + jnp.maximum(s - n_groups, 0) * gp + i], 1, 0, 0, 0)

    per_b = lambda a: pl.BlockSpec((1,) + a.shape[1:], lambda bi, s, pt, sl: (bi,) + (0,) * (a.ndim - 1))
    const = lambda a: pl.BlockSpec(a.shape, lambda bi, s, pt, sl: (0,) * a.ndim)
    in_specs = [pl.BlockSpec(page_blk, cmp_map(i)) for i in range(gp)] + [pl.BlockSpec(page_blk, sel_map(i)) for i in range(gp)]
    in_specs += [per_b(q4), per_b(q_bd), per_b(kvn),
                 pl.BlockSpec((1, 1) + cwin_t.shape[2:], lambda bi, s, pt, sl: (layer, bi, 0, 0, 0, 0)),
                 per_b(wn), per_b(g4)] + [const(a) for a in consts]
    all_rows = N_KV_HEADS * SAMPLE_ROWS
    small = pltpu.VMEM((all_rows, LANES), F32)
    grid_spec = pltpu.PrefetchScalarGridSpec(
        num_scalar_prefetch=2, grid=(b, 2 * n_groups), in_specs=in_specs, out_specs=per_b(q4),
        scratch_shapes=[pltpu.VMEM((4, past_len, LANES), F32),
                        pltpu.VMEM((N_KV_HEADS, n_seg, HEAD_DIM), F32), pltpu.VMEM((N_KV_HEADS, n_seg, HEAD_DIM), F32),
                        pltpu.VMEM((n_groups, all_rows, gp * PAGE_SIZE), F32),
                        pltpu.VMEM((N_KV_HEADS, SAMPLE_ROWS, HEAD_DIM), F32), small, small,
                        pltpu.VMEM((all_rows, KV_WIDTH), F32)])
    o = pl.pallas_call(
        functools.partial(_nsa_sample_kernel, gp=gp, n_groups=n_groups, past_len=past_len), grid_spec=grid_spec,
        out_shape=jax.ShapeDtypeStruct(q4.shape, F32),
        compiler_params=_cparams("parallel", "arbitrary"), name="nsa_sample_attn",
    )(page_table.reshape(-1), _alibi_slopes(), *([cache_t] * (2 * gp)), q4, q_bd, kvn, cwin_t, wn, g4, *consts)
    return o[:, :, :GROUP].reshape(b, ATT_WIDTH)


def nsa_sample_layer(layer, x, cache_kv, cache_win, page_table, nw, w_in, w_out, pe, w1, b1, w2):
    b = x.shape[0]
    q, kv, win, gates, z = rms_proj(x, nw, _split_nsa_w_in(w_in, F32), NSA_ACTS, tm=b, exact=True, name="nsa_in_proj_sample")
    o = nsa_sample_attention(layer, q, kv, win, gates, cache_kv, cache_win, page_table, pe, w1, b1, w2)
    zero = jnp.zeros_like(o)
    y = nsa_out(o, zero, zero, z, x, w_out, tm=b, exact=True)
    win_new = jnp.concatenate([cache_win[layer][:, 1:], win.reshape(b, 1, 2, N_KV_HEADS, HEAD_DIM)], axis=1)
    return y, kv.reshape(b, 1, 4, N_KV_HEADS, HEAD_DIM), win_new


def kernel(x_prompt, x_sample, cache_kv, cache_win, state_C, state_n, state_m, state_conv, page_table, norm_w, final_norm_w,
           nsa_w_in, nsa_w_out, nsa_cmp_pe, nsa_cmp_w1, nsa_cmp_b1, nsa_cmp_w2, m_w_in, m_conv_w, m_conv_b, m_w_qkv, m_w_gate,
           m_b_gate, m_norm_w, m_skip, m_w_out):
    assert DEPTH % 2 == 0 and x_sample.shape[1] == 1
    yp, ys = x_prompt, x_sample[:, 0]
    outs = {name: [] for name in ("kv_s", "win_p", "win_s", "C_p", "n_p", "n_s", "m_p", "m_s", "cv_p", "cv_s")}
    c_sample = None
    kvt_prompt = None
    n_nsa = (DEPTH + 1) // 2
    for i in range(DEPTH):
        l = i // 2
        if i % 2 == 0:
            prm = (nsa_w_in[l], nsa_w_out[l], nsa_cmp_pe[l], nsa_cmp_w1[l], nsa_cmp_b1[l], nsa_cmp_w2[l])
            yp, kvt_prompt, wp = nsa_prompt_layer_t(l, n_nsa, yp, kvt_prompt, norm_w[i], *prm)
            ys, kvs, wsm = nsa_sample_layer(l, ys, cache_kv, cache_win, page_table, norm_w[i], *prm)
            for name, val in (("kv_s", kvs), ("win_p", wp), ("win_s", wsm)):
                outs[name].append(val)
        else:
            prm = (m_w_in[l], m_conv_w[l], m_conv_b[l], m_w_qkv[l], m_w_gate[l], m_b_gate[l], m_norm_w[l], m_skip[l], m_w_out[l])
            final = i == DEPTH - 1
            yp, cp, np_, mp, cvp = mlstm_prompt_layer(yp, norm_w[i], *prm, final_norm_w, final=final)
            ys, c_sample, ns, ms, cvs = mlstm_sample_layer(l, ys, state_conv[l], state_C, c_sample, state_n[l], state_m[l],
                                                           norm_w[i], *prm, final_norm_w, final=final)
            for name, val in (("C_p", cp), ("n_p", np_), ("n_s", ns), ("m_p", mp), ("m_s", ms), ("cv_p", cvp), ("cv_s", cvs)):
                outs[name].append(val)
    st = {name: jnp.stack(vals) for name, vals in outs.items()}
    b, t = x_prompt.shape[:2]
    kv_prompt = kvt_prompt.reshape(n_nsa, b, 4, N_KV_HEADS, HEAD_DIM, t).transpose(0, 1, 5, 2, 3, 4)
    return (yp, ys[:, None], kv_prompt, st["kv_s"], st["win_p"], st["win_s"], st["C_p"], c_sample, st["n_p"], st["n_s"],
            st["m_p"], st["m_s"], st["cv_p"], st["cv_s"])
```

```python
import functools
import math

import jax
import jax.numpy as jnp
import numpy as np
from jax import lax
from jax.experimental import pallas as pl
from jax.experimental.pallas import tpu as pltpu

F32 = jnp.float32
BF16 = jnp.bfloat16
HI = lax.Precision.HIGHEST

D_MODEL = 1024
DEPTH = 4
N_HEADS = 16
HEAD_DIM = 64
N_KV_HEADS = 4
GROUP = N_HEADS // N_KV_HEADS
ATT_WIDTH = N_HEADS * HEAD_DIM
KV_WIDTH = N_KV_HEADS * HEAD_DIM
CMP_BLK = 32
CMP_STRIDE = 16
CMP_R = CMP_BLK // CMP_STRIDE
CMP_HID = 2 * HEAD_DIM
SEL_BLK = 64
N_SEL = 16
WINDOW = 512
FORCE_SCORE = 1.0e4
D_INNER = 2 * D_MODEL
M_HEADS = 4
M_HEAD_DIM = D_INNER // M_HEADS
CONV_W = 4
QKV_BLK = 4
RMS_EPS = 1e-6
PAGE_SIZE = 128

LANES = 128
VMEM_LIMIT = 56 * 1024 * 1024
NEG = -1.0e30
M_INIT = -5.0e29
Q_SCALE = HEAD_DIM ** -0.5
NT = (((1,), (1,)), ((), ()))
TN = (((0,), (0,)), ((), ()))


def _cparams(*sem):
    return pltpu.CompilerParams(dimension_semantics=sem, vmem_limit_bytes=VMEM_LIMIT)


def _silu(x):
    return x * jax.nn.sigmoid(x)


def _alibi_slopes():
    return jnp.asarray(np.exp2(-8.0 * np.arange(1, N_HEADS + 1) / N_HEADS), dtype=F32)


def _rmsnorm(x, w):
    return x * lax.rsqrt(jnp.mean(x * x, axis=-1, keepdims=True) + RMS_EPS) * w


def _rms_proj_kernel(x_ref, nw_ref, *refs, acts, n_chunk, exact):
    n = len(acts)
    w_refs, o_refs = refs[:n], refs[n:]
    h = _rmsnorm(x_ref[...], nw_ref[...])
    if not exact:
        h = h.astype(BF16)
    for w_ref, o_ref, act in zip(w_refs, o_refs, acts):
        width = w_ref.shape[1]
        for n0 in range(0, width, n_chunk):
            n1 = min(width, n0 + n_chunk)
            if exact:
                y = jnp.dot(h, w_ref[:, n0:n1], preferred_element_type=F32, precision=HI)
            else:
                y = jnp.dot(h, w_ref[:, n0:n1], preferred_element_type=F32)
            if act == "sigmoid":
                y = jax.nn.sigmoid(y)
            o_ref[:, n0:n1] = y.astype(o_ref.dtype)


def rms_proj(x, nw, weights, acts, *, tm, exact=False, out_dtypes=None, name="rms_proj"):
    m, k = x.shape
    assert m % tm == 0
    out_dtypes = out_dtypes or (F32,) * len(weights)
    in_specs = [pl.BlockSpec((tm, k), lambda i: (i, 0)), pl.BlockSpec((1, k), lambda i: (0, 0))]
    in_specs += [pl.BlockSpec(w.shape, lambda i: (0, 0)) for w in weights]
    out_specs = [pl.BlockSpec((tm, w.shape[1]), lambda i: (i, 0)) for w in weights]
    out_shape = [jax.ShapeDtypeStruct((m, w.shape[1]), dt) for w, dt in zip(weights, out_dtypes)]
    return pl.pallas_call(
        functools.partial(_rms_proj_kernel, acts=tuple(acts), n_chunk=512, exact=exact),
        grid=(m // tm,), in_specs=in_specs, out_specs=out_specs, out_shape=out_shape,
        compiler_params=_cparams("parallel"), name=name,
    )(x, nw.reshape(1, k), *weights)


def _compress_kernel(x_ref, w1_ref, pe_ref, w1f_ref, b1_ref, w2_ref, o_ref, *, n_seg):
    x = x_ref[0, 0, 0]
    p0 = jnp.dot(x, w1_ref[0, 0], preferred_element_type=F32)
    p1 = jnp.dot(x, w1_ref[0, 1], preferred_element_type=F32)
    hidc = jnp.dot(pe_ref[0], w1f_ref[0], preferred_element_type=F32)[0:1] + b1_ref[0]
    hid = hidc + p0 + pltpu.roll(p1, n_seg - 1, 0)
    o_ref[0, 0, 0] = jnp.dot(_silu(hid).astype(BF16), w2_ref[0], preferred_element_type=F32)


def compress_tokens(x, pe, w1, b1, w2):
    b, _, hkv, n_seg, _ = x.shape
    w1r = w1.reshape(2, CMP_R, CMP_STRIDE * HEAD_DIM, CMP_HID).astype(BF16)
    w1f = w1.reshape(2, CMP_BLK * HEAD_DIM, CMP_HID).astype(BF16)
    pe8 = jnp.broadcast_to(pe.reshape(2, 1, CMP_BLK * HEAD_DIM), (2, 8, CMP_BLK * HEAD_DIM)).astype(BF16)
    return pl.pallas_call(
        functools.partial(_compress_kernel, n_seg=n_seg),
        grid=(b, 2, hkv),
        in_specs=[
            pl.BlockSpec((1, 1, 1, n_seg, CMP_STRIDE * HEAD_DIM), lambda i, c, h: (i, c, h, 0, 0)),
            pl.BlockSpec((1, CMP_R, CMP_STRIDE * HEAD_DIM, CMP_HID), lambda i, c, h: (c, 0, 0, 0)),
            pl.BlockSpec((1, 8, CMP_BLK * HEAD_DIM), lambda i, c, h: (c, 0, 0)),
            pl.BlockSpec((1, CMP_BLK * HEAD_DIM, CMP_HID), lambda i, c, h: (c, 0, 0)),
            pl.BlockSpec((1, 1, CMP_HID), lambda i, c, h: (c, 0, 0)),
            pl.BlockSpec((1, CMP_HID, HEAD_DIM), lambda i, c, h: (c, 0, 0)),
        ],
        out_specs=pl.BlockSpec((1, 1, 1, n_seg, HEAD_DIM), lambda i, c, h: (i, c, h, 0, 0)),
        out_shape=jax.ShapeDtypeStruct((b, 2, hkv, n_seg, HEAD_DIM), F32),
        compiler_params=_cparams("parallel", "parallel", "parallel"), name="nsa_compress",
    )(x, w1r, pe8, w1f, b1.reshape(2, 1, CMP_HID), w2.astype(BF16))


def _topk_mask_t(score_t, n_valid, k):
    r = score_t.shape[0]
    jidx = lax.broadcasted_iota(jnp.int32, (r, 1), 0)
    cnt = jnp.zeros(score_t.shape, F32)
    for i in range(n_valid):
        row = score_t[i:i + 1, :]
        beats = (row > score_t) | ((jidx > i) & (row == score_t))
        cnt = cnt + jnp.where(beats, 1.0, 0.0)
    return cnt < float(k)


def _cmp_sel_kernel(slopes_ref, q_ref, kc_ref, vc_ref, g_ref, ov_ref, oc_ref, sel_ref, *, tq, n_seg, n_blk):
    kh = pl.program_id(1)
    t0 = pl.program_id(2) * tq
    t_col = t0 + lax.broadcasted_iota(jnp.int32, (tq, 1), 0)
    cend = lax.broadcasted_iota(jnp.int32, (1, n_seg), 1) * CMP_STRIDE + (CMP_BLK - 1)
    d_c = (t_col - cend).astype(F32)
    ok = d_c >= 0.0
    kc = kc_ref[0, 0, 0].astype(BF16)
    vc = vc_ref[0, 0, 0].astype(BF16)
    imp = jnp.zeros((tq, LANES), F32)
    outs = []
    for g in range(GROUP):
        qg = (q_ref[0, :, g * HEAD_DIM:(g + 1) * HEAD_DIM] * Q_SCALE).astype(BF16)
        s = lax.dot_general(qg, kc, NT, preferred_element_type=F32) - slopes_ref[kh * GROUP + g] * d_c
        s = jnp.where(ok, s, NEG)
        m = jnp.max(s, axis=-1, keepdims=True)
        p = jnp.where(ok, jnp.exp(s - m), 0.0)
        pn = (p / jnp.maximum(jnp.sum(p, axis=-1, keepdims=True), 1e-30)).astype(BF16)
        o = jnp.dot(pn, vc, preferred_element_type=F32)
        outs.append(o * g_ref[0, 0, :, 3 * g:3 * g + 1])
        imp = imp + jnp.dot(pn, ov_ref[...], preferred_element_type=F32)
    oc_ref[0] = jnp.concatenate(outs, axis=1)
    blk = lax.broadcasted_iota(jnp.int32, (1, LANES), 1)
    tb = lax.shift_right_logical(t_col, int(math.log2(SEL_BLK)))
    forced = (blk == 0) | (blk == tb) | (blk == tb - 1)
    score = jnp.where(forced, FORCE_SCORE, jnp.where(blk * SEL_BLK <= t_col, imp, -1.0))
    score = jnp.where(blk < n_blk, score, -2.0)
    score_t = score.T
    sel_t = _topk_mask_t(score_t, n_blk, min(N_SEL, n_blk)) & (score_t > -0.5)
    sel_ref[0, 0] = jnp.where(sel_t, 1.0, 0.0).T


def cmp_select(q, kvc, gates_r, slopes, ov, *, tq):
    b, t, _ = q.shape
    n_seg = kvc.shape[3]
    n_blk = t // SEL_BLK
    grid_spec = pltpu.PrefetchScalarGridSpec(
        num_scalar_prefetch=1, grid=(b, N_KV_HEADS, t // tq),
        in_specs=[
            pl.BlockSpec((1, tq, GROUP * HEAD_DIM), lambda i, k, j, s: (i, j, k)),
            pl.BlockSpec((1, 1, 1, n_seg, HEAD_DIM), lambda i, k, j, s: (i, 0, k, 0, 0)),
            pl.BlockSpec((1, 1, 1, n_seg, HEAD_DIM), lambda i, k, j, s: (i, 1, k, 0, 0)),
            pl.BlockSpec((1, 1, tq, 3 * GROUP), lambda i, k, j, s: (i, k, j, 0)),
            pl.BlockSpec((n_seg, LANES), lambda i, k, j, s: (0, 0)),
        ],
        out_specs=[
            pl.BlockSpec((1, tq, GROUP * HEAD_DIM), lambda i, k, j, s: (i, j, k)),
            pl.BlockSpec((1, 1, tq, LANES), lambda i, k, j, s: (i, k, j, 0)),
        ],
    )
    return pl.pallas_call(
        functools.partial(_cmp_sel_kernel, tq=tq, n_seg=n_seg, n_blk=n_blk),
        grid_spec=grid_spec,
        out_shape=[jax.ShapeDtypeStruct((b, t, ATT_WIDTH), F32), jax.ShapeDtypeStruct((b, N_KV_HEADS, t, LANES), F32)],
        compiler_params=_cparams("parallel", "parallel", "parallel"), name="nsa_cmp_select",
    )(slopes, q, kvc, kvc, gates_r, ov)


def _cmp_to_sel_matrix(n_rows, n_blk):
    start = np.arange(n_rows) * CMP_STRIDE
    blk = np.arange(LANES) * SEL_BLK
    ov = (start[:, None] < blk[None, :] + SEL_BLK) & (start[:, None] + CMP_BLK > blk[None, :]) & (np.arange(LANES)[None, :] < n_blk)
    return jnp.asarray(ov, dtype=BF16)


def _stack_heads(q_blk, extra=None):
    parts = []
    for g in range(GROUP):
        qg = q_blk[:, g * HEAD_DIM:(g + 1) * HEAD_DIM] * Q_SCALE
        if extra is not None:
            qg = jnp.concatenate([qg, extra], axis=1)
        parts.append(qg)
    return jnp.concatenate(parts, axis=0).astype(BF16)


def _head_cols(fn, tq):
    return jnp.concatenate([jnp.full((tq, 1), fn(g), F32) for g in range(GROUP)], axis=0)


def _sel_attn_kernel(slopes_ref, q_ref, sel_ref, ka_ref, v_ref, g_ref, o_ref, *, tq, tk):
    kh = pl.program_id(1)
    t0 = pl.program_id(2) * tq
    mask_feat = (sel_ref[0, 0][:, 0:HEAD_DIM] - 1.0) * 1.0e30
    qa = _stack_heads(q_ref[0], mask_feat)
    slope_col = _head_cols(lambda g: slopes_ref[kh * GROUP + g], tq)
    t_col = t0 + lax.broadcasted_iota(jnp.int32, (tq, 1), 0)
    t_col4 = jnp.concatenate([t_col] * GROUP, axis=0)

    def body(kt, carry):
        m, l, acc = carry
        ks = pl.multiple_of(kt * tk, tk)
        s = lax.dot_general(qa, ka_ref[0, 0, pl.ds(ks, tk), :], NT, preferred_element_type=F32)
        pos = ks + lax.broadcasted_iota(jnp.int32, (1, tk), 1)
        s = s + slope_col * (pos - t0).astype(F32)
        s = jnp.where(pos <= t_col4, s, NEG)
        m_new = jnp.maximum(m, jnp.max(s, axis=-1, keepdims=True))
        alpha = jnp.exp(m - m_new)
        p = jnp.exp(s - m_new)
        l = alpha * l + jnp.sum(p, axis=-1, keepdims=True)
        acc = alpha * acc + jnp.dot(p.astype(BF16), v_ref[0, 0, pl.ds(ks, tk), :], preferred_element_type=F32)
        return m_new, l, acc

    n_kt = (t0 + tq + tk - 1) // tk
    init = (jnp.full((GROUP * tq, 1), M_INIT, F32), jnp.zeros((GROUP * tq, 1), F32), jnp.zeros((GROUP * tq, HEAD_DIM), F32))
    _, l, acc = lax.fori_loop(0, n_kt, body, init)
    o = acc / jnp.maximum(l, 1e-30)
    o_ref[0] = jnp.concatenate(
        [o[g * tq:(g + 1) * tq] * g_ref[0, 0, :, 3 * g + 1:3 * g + 2] for g in range(GROUP)], axis=1)


def sel_attention(q, sel, k_aug, v_sel, gates_r, slopes, *, tq, tk):
    b, t, _ = q.shape
    grid_spec = pltpu.PrefetchScalarGridSpec(
        num_scalar_prefetch=1, grid=(b, N_KV_HEADS, t // tq),
        in_specs=[
            pl.BlockSpec((1, tq, GROUP * HEAD_DIM), lambda i, k, j, s: (i, j, k)),
            pl.BlockSpec((1, 1, tq, LANES), lambda i, k, j, s: (i, k, j, 0)),
            pl.BlockSpec((1, 1, t, LANES), lambda i, k, j, s: (i, k, 0, 0)),
            pl.BlockSpec((1, 1, t, HEAD_DIM), lambda i, k, j, s: (i, k, 0, 0)),
            pl.BlockSpec((1, 1, tq, 3 * GROUP), lambda i, k, j, s: (i, k, j, 0)),
        ],
        out_specs=pl.BlockSpec((1, tq, GROUP * HEAD_DIM), lambda i, k, j, s: (i, j, k)),
    )
    return pl.pallas_call(
        functools.partial(_sel_attn_kernel, tq=tq, tk=tk), grid_spec=grid_spec,
        out_shape=jax.ShapeDtypeStruct((b, t, ATT_WIDTH), F32),
        compiler_params=_cparams("parallel", "parallel", "arbitrary"), name="nsa_sel_attn",
    )(slopes, q, sel, k_aug, v_sel, gates_r)


def _win_attn_kernel(slopes_ref, q_ref, k_ref, v_ref, g_ref, o_ref, *, tq, span):
    kh = pl.program_id(1)
    t0 = pl.program_id(2) * tq
    qs = _stack_heads(q_ref[0])
    slope_col = _head_cols(lambda g: slopes_ref[kh * GROUP + g], tq)
    t_col = t0 + lax.broadcasted_iota(jnp.int32, (tq, 1), 0)
    t_col4 = jnp.concatenate([t_col] * GROUP, axis=0)
    start = pl.multiple_of(jnp.maximum(t0 + tq - span, 0), tq)
    s = lax.dot_general(qs, k_ref[0, 0, pl.ds(start, span), :], NT, preferred_element_type=F32)
    d = t_col4 - (start + lax.broadcasted_iota(jnp.int32, (1, span), 1))
    ok = (d >= 0) & (d <= WINDOW)
    s = jnp.where(ok, s - slope_col * d.astype(F32), NEG)
    m = jnp.max(s, axis=-1, keepdims=True)
    p = jnp.where(ok, jnp.exp(s - m), 0.0)
    l = jnp.sum(p, axis=-1, keepdims=True)
    o = jnp.dot(p.astype(BF16), v_ref[0, 0, pl.ds(start, span), :], preferred_element_type=F32) / jnp.maximum(l, 1e-30)
    o_ref[0] = jnp.concatenate(
        [o[g * tq:(g + 1) * tq] * g_ref[0, 0, :, 3 * g + 2:3 * g + 3] for g in range(GROUP)], axis=1)


def win_attention(q, k_win, v_win, gates_r, slopes, *, tq):
    b, t, _ = q.shape
    span = WINDOW + tq
    assert t >= span and span % tq == 0
    grid_spec = pltpu.PrefetchScalarGridSpec(
        num_scalar_prefetch=1, grid=(b, N_KV_HEADS, t // tq),
        in_specs=[
            pl.BlockSpec((1, tq, GROUP * HEAD_DIM), lambda i, k, j, s: (i, j, k)),
            pl.BlockSpec((1, 1, t, HEAD_DIM), lambda i, k, j, s: (i, k, 0, 0)),
            pl.BlockSpec((1, 1, t, HEAD_DIM), lambda i, k, j, s: (i, k, 0, 0)),
            pl.BlockSpec((1, 1, tq, 3 * GROUP), lambda i, k, j, s: (i, k, j, 0)),
        ],
        out_specs=pl.BlockSpec((1, tq, GROUP * HEAD_DIM), lambda i, k, j, s: (i, j, k)),
    )
    return pl.pallas_call(
        functools.partial(_win_attn_kernel, tq=tq, span=span), grid_spec=grid_spec,
        out_shape=jax.ShapeDtypeStruct((b, t, ATT_WIDTH), F32),
        compiler_params=_cparams("parallel", "parallel", "arbitrary"), name="nsa_win_attn",
    )(slopes, q, k_win, v_win, gates_r)


def _nsa_out_kernel(oc_ref, os_ref, ow_ref, z_ref, x_ref, w_ref, y_ref, *, exact):
    a = (oc_ref[...] + os_ref[...] + ow_ref[...]) * _silu(z_ref[...])
    if exact:
        y_ref[...] = x_ref[...] + jnp.dot(a, w_ref[...], preferred_element_type=F32, precision=HI)
    else:
        y_ref[...] = x_ref[...] + jnp.dot(a.astype(BF16), w_ref[...], preferred_element_type=F32)


def nsa_out(o_c, o_s, o_w, z, x, w_out, *, tm, exact=False):
    m, d = x.shape
    row = pl.BlockSpec((tm, d), lambda i: (i, 0))
    return pl.pallas_call(
        functools.partial(_nsa_out_kernel, exact=exact), grid=(m // tm,),
        in_specs=[row, row, row, row, row, pl.BlockSpec(w_out.shape, lambda i: (0, 0))],
        out_specs=row, out_shape=jax.ShapeDtypeStruct((m, d), F32),
        compiler_params=_cparams("parallel"), name="nsa_out",
    )(o_c, o_s, o_w, z, x, w_out)


LOG2E = math.log2(math.e)
Q_SCALE2 = Q_SCALE * LOG2E
ALIBI_ROWS = 16
BIG = 1.0e30
ONES_ROWS = 16
TQ = 256
TK = 512
WCH = 128


def _alibi_table():
    s = _alibi_slopes() * LOG2E
    s1 = s.astype(BF16).astype(F32)
    s2 = (s - s1).astype(BF16).astype(F32)
    s3 = (s - s1 - s2).astype(BF16).astype(F32)
    tab = jnp.stack([SEL_BLK * s1, SEL_BLK * s2, SEL_BLK * s3, s1, s2, s3, jnp.full_like(s, -BIG), jnp.zeros_like(s)], axis=1)
    return tab.reshape(-1)


def _pos_features(pos, valid, width):
    lane = lax.broadcasted_iota(jnp.int32, (pos.shape[0], width), 1)
    blk = lax.shift_right_logical(pos, int(math.log2(SEL_BLK))).astype(F32)
    rem = (pos & (SEL_BLK - 1)).astype(F32)
    f = jnp.where(lane < 3, blk, jnp.where(lane < 6, rem, 0.0))
    return jnp.where(lane == 6, jnp.where(valid, 0.0, 1.0), f)


def _nsa_in_proj_kernel(x_ref, nw_ref, wt_ref, wn_ref, *rest):
    qt_ref, kvt_ref, wint_ref, gt_ref, zt_ref, kn_ref, kwn_ref = rest[-7:]
    h = _rmsnorm(x_ref[0], nw_ref[...]).astype(BF16)

    def nt(r0, r1):
        return lax.dot_general(wt_ref[r0:r1, :], h, NT, preferred_element_type=F32)

    o1, o2, o3 = ATT_WIDTH, ATT_WIDTH + 4 * KV_WIDTH, ATT_WIDTH + 6 * KV_WIDTH
    o4 = o3 + LANES
    for r0 in range(0, o1, 512):
        qt_ref[0, r0:r0 + 512, :] = (nt(r0, r0 + 512) * Q_SCALE2).astype(BF16)
    for r0 in range(o1, o2, 512):
        kvt_ref[0, 0, r0 - o1:r0 - o1 + 512, :] = nt(r0, r0 + 512)
    wint_ref[0] = nt(o2, o3)
    gt_ref[0] = jax.nn.sigmoid(nt(o3, o4))
    for r0 in range(o4, o4 + ATT_WIDTH, 512):
        zt_ref[0, r0 - o4:r0 - o4 + 512, :] = nt(r0, r0 + 512).astype(zt_ref.dtype)
    yn = jnp.dot(h, wn_ref[...], preferred_element_type=F32)
    for j in range(3):
        for hh in range(N_KV_HEADS):
            c0 = (j * N_KV_HEADS + hh) * HEAD_DIM
            kn_ref[0, j, hh] = yn[:, c0:c0 + HEAD_DIM].astype(BF16)
    for hh in range(N_KV_HEADS):
        c0 = (3 * N_KV_HEADS + hh) * HEAD_DIM
        kwn_ref[0, hh] = yn[:, c0:c0 + HEAD_DIM].astype(BF16)


def nsa_in_proj(layer, n_layers, x, nw, w_in, kvt_all, *, tm):
    b, t, d = x.shape
    o1, o2, o3 = ATT_WIDTH, ATT_WIDTH + 4 * KV_WIDTH, ATT_WIDTH + 6 * KV_WIDTH
    o4 = o3 + 3 * N_HEADS
    w_t = w_in.T
    wt = jnp.concatenate([w_t[:o4], jnp.zeros((LANES - 3 * N_HEADS, d), w_in.dtype), w_t[o4:]], axis=0).astype(BF16)
    wn = jnp.concatenate([w_in[:, o1:o1 + 3 * KV_WIDTH], w_in[:, o2:o2 + KV_WIDTH]], axis=1).astype(BF16)
    tok = lambda rows: pl.BlockSpec((1, rows, tm), lambda i, j: (i, 0, j))
    full = lambda a: pl.BlockSpec(a.shape, lambda i, j: (0,) * a.ndim)
    tshape = lambda rows, dt: jax.ShapeDtypeStruct((b, rows, t), dt)
    args = [x, nw.reshape(1, d), wt, wn]
    in_specs = [pl.BlockSpec((1, tm, d), lambda i, j: (i, j, 0)), pl.BlockSpec((1, d), lambda i, j: (0, 0)), full(wt), full(wn)]
    aliases = {}
    if kvt_all is not None:
        args.append(kvt_all)
        in_specs.append(pl.BlockSpec(memory_space=pl.ANY))
        aliases = {len(args) - 1: 1}
    return pl.pallas_call(
        _nsa_in_proj_kernel, grid=(b, t // tm), in_specs=in_specs,
        out_specs=[tok(ATT_WIDTH), pl.BlockSpec((1, 1, 4 * KV_WIDTH, tm), lambda i, j: (layer, i, 0, j)),
                   tok(2 * KV_WIDTH), tok(LANES), tok(ATT_WIDTH),
                   pl.BlockSpec((1, 3, N_KV_HEADS, tm, HEAD_DIM), lambda i, j: (i, 0, 0, j, 0)),
                   pl.BlockSpec((1, N_KV_HEADS, tm, HEAD_DIM), lambda i, j: (i, 0, j, 0))],
        out_shape=[tshape(ATT_WIDTH, BF16), jax.ShapeDtypeStruct((n_layers, b, 4 * KV_WIDTH, t), F32),
                   tshape(2 * KV_WIDTH, F32), tshape(LANES, F32),
                   tshape(ATT_WIDTH, BF16), jax.ShapeDtypeStruct((b, 3, N_KV_HEADS, t, HEAD_DIM), BF16),
                   jax.ShapeDtypeStruct((b, N_KV_HEADS, t, HEAD_DIM), BF16)],
        input_output_aliases=aliases,
        compiler_params=_cparams("parallel", "parallel"), name="nsa_in_proj",
    )(*args)


def _topk_mask_rows(score, k):
    r = score.shape[0]
    groups = [score[8 * v:8 * v + 8] for v in range(r // 8)]
    sub = lax.broadcasted_iota(jnp.int32, (8, 1), 0)
    cnt = [jnp.zeros(g.shape, F32) for g in groups]
    for i in range(r):
        row = score[i:i + 1, :]
        for v, g in enumerate(groups):
            if 8 * v > i:
                beats = row >= g
            elif 8 * v + 7 < i:
                beats = row > g
            else:
                beats = (row > g) | ((sub > i - 8 * v) & (row == g))
            cnt[v] = cnt[v] + jnp.where(beats, 1.0, 0.0)
    return jnp.concatenate(cnt, axis=0) < float(k)


def _nsa_attn_kernel(tab_ref, qt_ref, gt_ref, kc_ref, vc_ref, ksel_ref, kwin_ref, vselt_ref, vwint_ref, ovt_ref, ot_ref,
                     ksa_s, kwa_s, kca_s, vst_s, vwt_s, vct_s, sc_s, m_s, acc_s, *, t_len, n_seg, n_blk):
    kh = pl.program_id(1)
    qi = pl.program_id(2)
    t0 = qi * TQ
    cols = GROUP * TQ
    ones = lambda n: jnp.ones((ONES_ROWS, n), BF16)

    @pl.when(qi == 0)
    def _build_keys():
        pos = lax.broadcasted_iota(jnp.int32, (t_len, 1), 0)
        onehot = jnp.where(lax.shift_right_logical(pos, int(math.log2(SEL_BLK)))
                           == lax.broadcasted_iota(jnp.int32, (1, HEAD_DIM), 1), 1.0, 0.0)
        ksa_s[:, 0:LANES] = jnp.concatenate([ksel_ref[0, 0, 0].astype(F32), onehot], axis=1).astype(BF16)
        ksa_s[:, LANES:2 * LANES] = _pos_features(pos, pos >= 0, LANES).astype(BF16)
        no_key = _pos_features(jnp.zeros((WINDOW, 1), jnp.int32), jnp.zeros((WINDOW, 1), jnp.bool_), HEAD_DIM)
        kwa_s[0:WINDOW, :] = jnp.concatenate([jnp.zeros((WINDOW, HEAD_DIM), F32), no_key], axis=1).astype(BF16)
        kwa_s[WINDOW:WINDOW + t_len, :] = jnp.concatenate(
            [kwin_ref[0, 0].astype(F32), _pos_features(pos, pos >= 0, HEAD_DIM)], axis=1).astype(BF16)
        cend = lax.broadcasted_iota(jnp.int32, (n_seg, 1), 0) * CMP_STRIDE + (CMP_BLK - 1)
        kca_s[...] = jnp.concatenate([kc_ref[0, 0, 0], _pos_features(cend, cend >= 0, HEAD_DIM)], axis=1).astype(BF16)
        vc_pad = jnp.concatenate([vc_ref[0, 0, 0], jnp.zeros((n_seg, LANES - HEAD_DIM), F32)], axis=1)
        vct_s[...] = jnp.concatenate([vc_pad.T[0:HEAD_DIM].astype(BF16), ones(n_seg)], axis=0)
        for c in range(t_len // TK):
            vst_s[c] = jnp.concatenate([vselt_ref[0, 0, :, c * TK:(c + 1) * TK].astype(BF16), ones(TK)], axis=0)
        for c in range(WINDOW // WCH):
            vwt_s[c] = jnp.zeros((HEAD_DIM + ONES_ROWS, WCH), BF16)
        for c in range(t_len // WCH):
            vwt_s[WINDOW // WCH + c] = jnp.concatenate([vwint_ref[0, :, c * WCH:(c + 1) * WCH].astype(BF16), ones(WCH)], axis=0)

    qb = qt_ref[0]
    qw = jnp.concatenate([qb[g * HEAD_DIM:(g + 1) * HEAD_DIM, :] for g in range(GROUP)], axis=1)
    frow = lax.broadcasted_iota(jnp.int32, (ALIBI_ROWS, cols), 0)
    fhead = lax.broadcasted_iota(jnp.int32, (ALIBI_ROWS, cols), 1) // TQ
    feat = jnp.zeros((ALIBI_ROWS, cols), F32)
    for g in range(GROUP):
        for r in range(7):
            feat = jnp.where((frow == r) & (fhead == g), tab_ref[(kh * GROUP + g) * 8 + r], feat)
    feat = feat.astype(BF16)
    q_base = jnp.concatenate([qw, feat, jnp.zeros((LANES - HEAD_DIM - ALIBI_ROWS, cols), BF16)], axis=0)
    t_row = t0 + lax.broadcasted_iota(jnp.int32, (1, cols), 1) % TQ

    def finish(acc):
        return acc[0:HEAD_DIM] * (1.0 / jnp.maximum(acc[HEAD_DIM:HEAD_DIM + 1], 1e-30))

    cend = lax.broadcasted_iota(jnp.int32, (n_seg, 1), 0) * CMP_STRIDE + (CMP_BLK - 1)
    ok_c = cend <= t_row
    s_c = jnp.where(ok_c, jnp.dot(kca_s[...], q_base, preferred_element_type=F32), NEG)
    m_c = jnp.max(s_c, axis=0, keepdims=True)
    p_c = jnp.where(ok_c, jnp.exp2(s_c - m_c), 0.0).astype(BF16)
    acc_c = jnp.dot(vct_s[...], p_c, preferred_element_type=F32)
    inv_c = 1.0 / jnp.maximum(acc_c[HEAD_DIM:HEAD_DIM + 1], 1e-30)
    o_cmp = acc_c[0:HEAD_DIM] * inv_c
    imp_c = jnp.dot(ovt_ref[...], p_c, preferred_element_type=F32) * inv_c
    imp = imp_c[:, 0:TQ]
    for g in range(1, GROUP):
        imp = imp + imp_c[:, g * TQ:(g + 1) * TQ]
    blk = lax.broadcasted_iota(jnp.int32, (HEAD_DIM, 1), 0)
    tq_row = t_row[:, 0:TQ]
    tb = lax.shift_right_logical(tq_row, int(math.log2(SEL_BLK)))
    forced = (blk == 0) | (blk == tb) | (blk == tb - 1)
    score = jnp.where(forced, FORCE_SCORE, jnp.where(blk * SEL_BLK <= tq_row, imp, -1.0))
    score = jnp.where(blk < n_blk, score, -2.0)
    chosen = _topk_mask_rows(score, min(N_SEL, n_blk)) & (score > -0.5)

    span = WINDOW + TQ
    k_w = kwa_s[pl.ds(pl.multiple_of(t0, TQ), span), :]
    v_w = jnp.concatenate([vwt_s[qi * (TQ // WCH) + c] for c in range(span // WCH)], axis=1)
    jj = lax.broadcasted_iota(jnp.int32, (TQ, 1), 0)
    ii = lax.broadcasted_iota(jnp.int32, (1, cols), 1) % TQ
    s_w = jnp.dot(k_w, q_base, preferred_element_type=F32)
    s_w = jnp.concatenate([jnp.where(jj >= ii, s_w[0:TQ], NEG), s_w[TQ:WINDOW],
                           jnp.where(jj <= ii, s_w[WINDOW:span], NEG)], axis=0)
    m_w = jnp.maximum(jnp.max(s_w, axis=0, keepdims=True), M_INIT)
    p_w = jnp.exp2((s_w - m_w).astype(BF16))
    o_win = finish(jnp.dot(v_w, p_w, preferred_element_type=F32))

    mrow = jnp.concatenate([jnp.where(chosen, 0.0, -BIG).astype(BF16)] * GROUP, axis=1)
    q_sel = jnp.concatenate([qw, mrow, feat, jnp.zeros((LANES - ALIBI_ROWS, cols), BF16)], axis=0)
    n_pairs = (t0 + TQ - 1) // (2 * TK) + 1
    max_pairs = t_len // (2 * TK)
    key_off = lax.broadcasted_iota(jnp.int32, (TK, 1), 0)

    def issue(j):
        for u in range(2):
            ks = (2 * j + u) * TK
            s = jnp.dot(ksa_s[ks:ks + TK, :], q_sel, preferred_element_type=F32)
            sc_s[j % 2, u] = jnp.where(ks + key_off <= t_row, s, NEG)

    def absorb(j):
        s0, s1 = sc_s[j % 2, 0], sc_s[j % 2, 1]
        m = m_s[0:1, :]
        m_new = jnp.maximum(m, jnp.maximum(jnp.max(s0, axis=0, keepdims=True), jnp.max(s1, axis=0, keepdims=True)))
        acc = jnp.exp2(m - m_new) * acc_s[...]
        for u, s in enumerate((s0, s1)):
            acc = acc + jnp.dot(vst_s[2 * j + u], jnp.exp2((s - m_new).astype(BF16)), preferred_element_type=F32)
        m_s[0:1, :] = m_new
        acc_s[...] = acc

    m_s[0:1, :] = jnp.full((1, cols), M_INIT, F32)
    acc_s[...] = jnp.zeros(acc_s.shape, F32)
    for n in range(1, max_pairs + 1):
        @pl.when(n_pairs == n)
        def _pairs(n=n):
            issue(0)
            for j in range(n):
                if j + 1 < n:
                    issue(j + 1)
                absorb(j)
    o_sel = finish(acc_s[...])

    for g in range(GROUP):
        sl = slice(g * TQ, (g + 1) * TQ)
        gate = lambda j: gt_ref[0, pl.ds(kh * 3 * GROUP + 3 * g + j, 1), :]
        mix = gate(0) * o_cmp[:, sl] + gate(1) * o_sel[:, sl] + gate(2) * o_win[:, sl]
        ot_ref[0, g * HEAD_DIM:(g + 1) * HEAD_DIM, :] = mix.astype(ot_ref.dtype)


def nsa_attention(layer, qt, gt, kvc, kn, kwn, kvt, wint):
    b, _, t = qt.shape
    n_seg = kvc.shape[3]
    n_blk = t // SEL_BLK
    assert t % (2 * TK) == 0 and n_blk <= HEAD_DIM and t >= WINDOW + TQ
    start = np.arange(n_seg) * CMP_STRIDE
    blk = np.arange(HEAD_DIM) * SEL_BLK
    ovt = jnp.asarray(((start[None, :] < blk[:, None] + SEL_BLK) & (start[None, :] + CMP_BLK > blk[:, None])
                       & (np.arange(HEAD_DIM)[:, None] < n_blk)), dtype=BF16)
    vrows = HEAD_DIM + ONES_ROWS
    grid_spec = pltpu.PrefetchScalarGridSpec(
        num_scalar_prefetch=1, grid=(b, N_KV_HEADS, t // TQ),
        in_specs=[
            pl.BlockSpec((1, GROUP * HEAD_DIM, TQ), lambda i, k, j, s: (i, k, j)),
            pl.BlockSpec((1, LANES, TQ), lambda i, k, j, s: (i, 0, j)),
            pl.BlockSpec((1, 1, 1, n_seg, HEAD_DIM), lambda i, k, j, s: (i, 0, k, 0, 0)),
            pl.BlockSpec((1, 1, 1, n_seg, HEAD_DIM), lambda i, k, j, s: (i, 1, k, 0, 0)),
            pl.BlockSpec((1, 1, 1, t, HEAD_DIM), lambda i, k, j, s: (i, 2, k, 0, 0)),
            pl.BlockSpec((1, 1, t, HEAD_DIM), lambda i, k, j, s: (i, k, 0, 0)),
            pl.BlockSpec((1, 1, HEAD_DIM, t), lambda i, k, j, s: (layer, i, 3 * N_KV_HEADS + k, 0)),
            pl.BlockSpec((1, HEAD_DIM, t), lambda i, k, j, s: (i, N_KV_HEADS + k, 0)),
            pl.BlockSpec((HEAD_DIM, n_seg), lambda i, k, j, s: (0, 0)),
        ],
        out_specs=pl.BlockSpec((1, GROUP * HEAD_DIM, TQ), lambda i, k, j, s: (i, k, j)),
        scratch_shapes=[pltpu.VMEM((t, 2 * LANES), BF16), pltpu.VMEM((WINDOW + t, LANES), BF16), pltpu.VMEM((n_seg, LANES), BF16),
                        pltpu.VMEM((t // TK, vrows, TK), BF16), pltpu.VMEM(((WINDOW + t) // WCH, vrows, WCH), BF16),
                        pltpu.VMEM((vrows, n_seg), BF16),
                        pltpu.VMEM((2, 2, TK, GROUP * TQ), F32), pltpu.VMEM((8, GROUP * TQ), F32),
                        pltpu.VMEM((vrows, GROUP * TQ), F32)])
    return pl.pallas_call(
        functools.partial(_nsa_attn_kernel, t_len=t, n_seg=n_seg, n_blk=n_blk), grid_spec=grid_spec,
        out_shape=jax.ShapeDtypeStruct((b, ATT_WIDTH, t), BF16),
        compiler_params=_cparams("parallel", "parallel", "arbitrary"), name="nsa_attn",
    )(_alibi_table(), qt, gt, kvc, kvc, kn, kwn, kvt, wint, ovt)


def _nsa_out_t_kernel(ot_ref, zt_ref, x_ref, w_ref, y_ref):
    a_t = (ot_ref[0].astype(F32) * _silu(zt_ref[0].astype(F32))).astype(BF16)
    y_ref[0] = x_ref[0] + lax.dot_general(a_t, w_ref[...], TN, preferred_element_type=F32)


def nsa_out_t(ot, zt, x, w_out, *, tm):
    b, t, d = x.shape
    tok = pl.BlockSpec((1, ATT_WIDTH, tm), lambda i, j: (i, 0, j))
    row = pl.BlockSpec((1, tm, d), lambda i, j: (i, j, 0))
    return pl.pallas_call(
        _nsa_out_t_kernel, grid=(b, t // tm),
        in_specs=[tok, tok, row, pl.BlockSpec(w_out.shape, lambda i, j: (0, 0))],
        out_specs=row, out_shape=jax.ShapeDtypeStruct((b, t, d), F32),
        compiler_params=_cparams("parallel", "parallel"), name="nsa_out",
    )(ot, zt, x, w_out)


def nsa_prompt_layer_t(layer, n_layers, x, kvt_all, nw, w_in, w_out, pe, w1, b1, w2):
    b, t, d = x.shape
    qt, kvt_all, wint, gt, zt, kn, kwn = nsa_in_proj(layer, n_layers, x, nw, w_in, kvt_all, tm=256)
    n_seg = t // CMP_STRIDE
    kvc = compress_tokens(kn[:, 0:2].reshape(b, 2, N_KV_HEADS, n_seg, CMP_STRIDE * HEAD_DIM), pe, w1, b1, w2)
    ot = nsa_attention(layer, qt, gt, kvc, kn, kwn, kvt_all, wint)
    y = nsa_out_t(ot, zt, x, w_out.astype(BF16), tm=256)
    wr = min(WINDOW, t)
    win5 = wint[:, :, t - wr:].reshape(b, 2, N_KV_HEADS, HEAD_DIM, wr).transpose(0, 4, 1, 2, 3)
    return y, kvt_all, win5


def _split_nsa_w_in(w_in, dtype):
    o1 = ATT_WIDTH
    o2 = o1 + 4 * KV_WIDTH
    o3 = o2 + 2 * KV_WIDTH
    o4 = o3 + 3 * N_HEADS
    wg = jnp.pad(w_in[:, o3:o4], ((0, 0), (0, LANES - 3 * N_HEADS)))
    return [w.astype(dtype) for w in (w_in[:, :o1], w_in[:, o1:o2], w_in[:, o2:o3], wg, w_in[:, o4:])]


NSA_ACTS = (None, None, None, "sigmoid", None)


def nsa_prompt_layer(x, nw, w_in, w_out, pe, w1, b1, w2):
    b, t, d = x.shape
    n_blk = t // SEL_BLK
    assert n_blk <= HEAD_DIM
    q, kv, win, gates, z = rms_proj(x.reshape(b * t, d), nw, _split_nsa_w_in(w_in, BF16), NSA_ACTS, tm=256, name="nsa_in_proj")
    q = q.reshape(b, t, ATT_WIDTH)
    kv5 = kv.reshape(b, t, 4, N_KV_HEADS, HEAD_DIM)
    win5 = win.reshape(b, t, 2, N_KV_HEADS, HEAD_DIM)
    kvt = kv5.transpose(0, 2, 3, 1, 4).astype(BF16)
    wint = win5.transpose(0, 2, 3, 1, 4).astype(BF16)
    gates_r = gates[:, :3 * N_HEADS].reshape(b, t, N_KV_HEADS, 3 * GROUP).transpose(0, 2, 1, 3)
    slopes = _alibi_slopes()
    n_seg = t // CMP_STRIDE
    kvc = compress_tokens(kvt[:, 0:2].reshape(b, 2, N_KV_HEADS, n_seg, CMP_STRIDE * HEAD_DIM), pe, w1, b1, w2)
    o_c, sel = cmp_select(q, kvc, gates_r, slopes, _cmp_to_sel_matrix(n_seg, n_blk), tq=128)
    onehot = (jnp.arange(t)[:, None] // SEL_BLK == jnp.arange(HEAD_DIM)[None, :]).astype(BF16)
    k_aug = jnp.concatenate([kvt[:, 2], jnp.broadcast_to(onehot, (b, N_KV_HEADS, t, HEAD_DIM))], axis=-1)
    o_s = sel_attention(q, sel, k_aug, kvt[:, 3], gates_r, slopes, tq=128, tk=512)
    o_w = win_attention(q, wint[:, 0], wint[:, 1], gates_r, slopes, tq=128)
    r2 = lambda a: a.reshape(b * t, -1)
    y = nsa_out(r2(o_c), r2(o_s), r2(o_w), z, r2(x), w_out.astype(BF16), tm=256)
    wr = min(WINDOW, t)
    return y.reshape(b, t, d), kv5, win5[:, t - wr:]


def _log_sigmoid(x):
    return jnp.minimum(x, 0.0) - jnp.log(1.0 + jnp.exp(-jnp.abs(x)))


def _mconv_body(shifted, xm, cw_ref, cb_ref, wbd_ref, wg_ref, bg_ref, q_ref, k_ref, v_ref, c_ref, g_ref, exact):
    conv = cb_ref[...]
    for j in range(CONV_W):
        conv = conv + shifted[j] * cw_ref[j:j + 1, :]
    c = _silu(conv)
    c_ref[...] = c.astype(c_ref.dtype)
    cast = (lambda a: a) if exact else (lambda a: a.astype(BF16))
    kw = dict(preferred_element_type=F32, precision=HI) if exact else dict(preferred_element_type=F32)
    gpre = bg_ref[...]
    for m, (src, dst) in enumerate(((c, q_ref), (c, k_ref), (xm, v_ref))):
        parts = []
        for gi in range(D_INNER // QKV_TILE):
            sl = slice(gi * QKV_TILE, (gi + 1) * QKV_TILE)
            y = jnp.dot(cast(src[:, sl]), wbd_ref[m, gi], **kw)
            parts.append(cast(y))
            dst[:, sl] = (y * (M_HEAD_DIM ** -0.5) if m == 1 else y).astype(dst.dtype)
        gpre = gpre + jnp.dot(jnp.concatenate(parts, axis=1), wg_ref[m * D_INNER:(m + 1) * D_INNER, :], **kw)
    lane = lax.broadcasted_iota(jnp.int32, gpre.shape, 1)
    g_ref[...] = jnp.where(lane < M_HEADS, gpre, _log_sigmoid(gpre))


def _mconv_prompt_kernel(xm_ref, halo_ref, cw_ref, cb_ref, wbd_ref, wg_ref, bg_ref, q_ref, k_ref, v_ref, c_ref, g_ref, *, tm):
    xm = xm_ref[0]
    halo = jnp.where(pl.program_id(1) == 0, 0.0, halo_ref[0])
    ext = jnp.concatenate([halo, xm], axis=0)
    shifted = [ext[5 + j:5 + j + tm] for j in range(CONV_W - 1)] + [xm]
    _mconv_body(shifted, xm, cw_ref, cb_ref, wbd_ref, wg_ref, bg_ref, q_ref.at[0], k_ref.at[0], v_ref.at[0], c_ref.at[0],
                g_ref.at[0], False)


QKV_TILE = 256


def _mlstm_small_weights(w_qkv, w_gate, b_gate, dtype):
    nb = QKV_TILE // QKV_BLK
    w = w_qkv.reshape(3, D_INNER // QKV_TILE, nb, QKV_BLK, QKV_BLK)
    eye = jnp.eye(nb, dtype=w.dtype)
    wbd = jnp.einsum("mgnji,nk->mgnjki", w, eye).reshape(3, D_INNER // QKV_TILE, QKV_TILE, QKV_TILE)
    wg = jnp.pad(w_gate, ((0, 0), (0, LANES - 2 * M_HEADS)))
    bg = jnp.pad(b_gate, (0, LANES - 2 * M_HEADS)).reshape(1, LANES)
    return wbd.astype(dtype), wg.astype(dtype), bg


def mconv_prompt(xm, conv_w, conv_b, w_qkv, w_gate, b_gate, *, tm):
    b, t, _ = xm.shape
    wbd, wg, bg = _mlstm_small_weights(w_qkv, w_gate, b_gate, BF16)
    row = pl.BlockSpec((1, tm, D_INNER), lambda i, j: (i, j, 0))
    full = lambda a: pl.BlockSpec(a.shape, lambda i, j: (0,) * a.ndim)
    cb = conv_b.reshape(1, D_INNER)
    return pl.pallas_call(
        functools.partial(_mconv_prompt_kernel, tm=tm), grid=(b, t // tm),
        in_specs=[row, pl.BlockSpec((1, 8, D_INNER), lambda i, j: (i, jnp.maximum(j * (tm // 8) - 1, 0), 0)),
                  full(conv_w), full(cb), full(wbd), full(wg), full(bg)],
        out_specs=[row, row, row, row, pl.BlockSpec((1, tm, LANES), lambda i, j: (i, j, 0))],
        out_shape=[jax.ShapeDtypeStruct((b, t, D_INNER), BF16)] * 4 + [jax.ShapeDtypeStruct((b, t, LANES), F32)],
        compiler_params=_cparams("parallel", "parallel"), name="mlstm_conv_qkv",
    )(xm, xm, conv_w, cb, wbd, wg, bg)


def _mlstm_cell_kernel(q_ref, k_ref, v_ref, gc_ref, gr_ref, c0_ref, n0_ref, m0_ref, h_ref, cf_ref, nf_ref, mf_ref,
                       c_s, n_s, m_s, *, chunk, n_chunks):
    ci = pl.program_id(2)

    @pl.when(ci == 0)
    def _():
        c_s[...] = c0_ref[0, 0]
        n_s[...] = n0_ref[0, 0]
        m_s[...] = m0_ref[0, 0]

    q, k, v = q_ref[0], k_ref[0], v_ref[0]
    icol, fcol = gc_ref[0, 0, :, 0:1], gc_ref[0, 0, :, 1:2]
    irow, frow = gr_ref[0, 0, 0:1, :], gr_ref[0, 0, 1:2, :]
    ri = lax.broadcasted_iota(jnp.int32, (chunk, chunk), 0)
    cj = lax.broadcasted_iota(jnp.int32, (chunk, chunk), 1)
    causal = cj <= ri
    b_col = jnp.dot(jnp.where(causal, 1.0, 0.0), jnp.broadcast_to(fcol, (chunk, LANES)),
                    preferred_element_type=F32, precision=HI)[:, 0:1]
    b_row = jnp.dot(jnp.broadcast_to(frow, (8, chunk)), jnp.where(ri <= cj, 1.0, 0.0),
                    preferred_element_type=F32, precision=HI)[0:1, :]
    dmat = jnp.where(causal, b_col - b_row + irow, -jnp.inf)
    m_prev = m_s[0:1, 0:1]
    inter = b_col + m_prev
    mt = jnp.maximum(jnp.max(dmat, axis=-1, keepdims=True), inter)
    qb, kb, vb = q.astype(BF16), k.astype(BF16), v.astype(BF16)
    s = lax.dot_general(qb, kb, NT, preferred_element_type=F32) * jnp.exp(dmat - mt)
    decay = jnp.exp(inter - mt)
    num = (jnp.dot(s.astype(BF16), vb, preferred_element_type=F32)
           + decay * jnp.dot(qb, c_s[...].astype(BF16), preferred_element_type=F32))
    den = jnp.sum(s, axis=-1, keepdims=True) + decay * jnp.sum(q * n_s[...], axis=-1, keepdims=True)
    hc = num / jnp.maximum(jnp.abs(den), jnp.exp(-mt))
    h_ref[0] = (hc * lax.rsqrt(jnp.mean(hc * hc, axis=-1, keepdims=True) + RMS_EPS)).astype(h_ref.dtype)
    m_new = mt[chunk - 1:chunk, :]
    b_last = b_col[chunk - 1:chunk, :]
    kw = k * jnp.exp(b_last - b_col + icol - m_new)
    carry = jnp.exp(b_last + m_prev - m_new)
    c_s[...] = carry * c_s[...] + lax.dot_general(kw.astype(BF16), vb, TN, preferred_element_type=F32)
    n_s[...] = carry * n_s[...] + jnp.sum(kw, axis=0, keepdims=True)
    m_s[...] = jnp.broadcast_to(m_new, m_s.shape)

    @pl.when(ci == n_chunks - 1)
    def _():
        cf_ref[0, 0] = c_s[...]
        nf_ref[0, 0] = n_s[...]
        mf_ref[0, 0] = m_s[...]


def mlstm_cell(q, k, v, gates, c0, n0, m0, *, chunk):
    b, t, _ = q.shape
    n_chunks = t // chunk
    g_col = jnp.stack([gates[..., :M_HEADS], gates[..., M_HEADS:2 * M_HEADS]], axis=-1).transpose(0, 2, 1, 3)
    g_row = g_col.transpose(0, 1, 3, 2)
    m0b = jnp.broadcast_to(m0[:, :, None, None], (b, M_HEADS, 8, LANES))
    n0r = n0.reshape(b, M_HEADS, 1, M_HEAD_DIM)
    head = pl.BlockSpec((1, chunk, M_HEAD_DIM), lambda i, h, c: (i, c, h))
    st = lambda *blk: pl.BlockSpec((1, 1) + blk, lambda i, h, c: (i, h, 0, 0))
    hn, cf, nf, mf = pl.pallas_call(
        functools.partial(_mlstm_cell_kernel, chunk=chunk, n_chunks=n_chunks), grid=(b, M_HEADS, n_chunks),
        in_specs=[head, head, head,
                  pl.BlockSpec((1, 1, chunk, 2), lambda i, h, c: (i, h, c, 0)),
                  pl.BlockSpec((1, 1, 2, chunk), lambda i, h, c: (i, h, 0, c)),
                  st(M_HEAD_DIM, M_HEAD_DIM), st(1, M_HEAD_DIM), st(8, LANES)],
        out_specs=[head, st(M_HEAD_DIM, M_HEAD_DIM), st(1, M_HEAD_DIM), st(8, LANES)],
        out_shape=[jax.ShapeDtypeStruct((b, t, D_INNER), BF16), jax.ShapeDtypeStruct((b, M_HEADS, M_HEAD_DIM, M_HEAD_DIM), F32),
                   jax.ShapeDtypeStruct((b, M_HEADS, 1, M_HEAD_DIM), F32), jax.ShapeDtypeStruct((b, M_HEADS, 8, LANES), F32)],
        scratch_shapes=[pltpu.VMEM((M_HEAD_DIM, M_HEAD_DIM), F32), pltpu.VMEM((1, M_HEAD_DIM), F32), pltpu.VMEM((8, LANES), F32)],
        compiler_params=_cparams("parallel", "parallel", "arbitrary"), name="mlstm_cell",
    )(q, k, v, g_col, g_row, c0, n0r, m0b)
    return hn, cf, nf.reshape(b, M_HEADS, M_HEAD_DIM), mf[:, :, 0, 0]


def _mlstm_out_kernel(hn_ref, c_ref, z_ref, x_ref, nw_ref, sk_ref, w_ref, fw_ref, y_ref, *, final, exact):
    a = (hn_ref[...].astype(F32) * nw_ref[...] + sk_ref[...] * c_ref[...].astype(F32)) * _silu(z_ref[...].astype(F32))
    if exact:
        y = x_ref[...] + jnp.dot(a, w_ref[...], preferred_element_type=F32, precision=HI)
    else:
        y = x_ref[...] + jnp.dot(a.astype(BF16), w_ref[...], preferred_element_type=F32)
    y_ref[...] = _rmsnorm(y, fw_ref[...]) if final else y


def mlstm_out(hn, c, z, x, norm_w, skip, w_out, final_w, *, tm, final, exact=False):
    m, d = x.shape
    wide = pl.BlockSpec((tm, D_INNER), lambda i: (i, 0))
    row = pl.BlockSpec((tm, d), lambda i: (i, 0))
    vec = lambda n: pl.BlockSpec((1, n), lambda i: (0, 0))
    return pl.pallas_call(
        functools.partial(_mlstm_out_kernel, final=final, exact=exact), grid=(m // tm,),
        in_specs=[wide, wide, wide, row, vec(D_INNER), vec(D_INNER), pl.BlockSpec(w_out.shape, lambda i: (0, 0)), vec(d)],
        out_specs=row, out_shape=jax.ShapeDtypeStruct((m, d), F32),
        compiler_params=_cparams("parallel"), name="mlstm_out",
    )(hn, c, z, x, norm_w.reshape(1, D_INNER), skip.reshape(1, D_INNER), w_out, final_w.reshape(1, d))


def mlstm_prompt_layer(x, nw, w_in, conv_w, conv_b, w_qkv, w_gate, b_gate, norm_w, skip, w_out, final_w, *, final):
    b, t, d = x.shape
    w_in = w_in.astype(BF16)
    xm, z = rms_proj(x.reshape(b * t, d), nw, [w_in[:, :D_INNER], w_in[:, D_INNER:]], (None, None), tm=256, out_dtypes=(F32, BF16),
                     name="mlstm_in_proj")
    xm3 = xm.reshape(b, t, D_INNER)
    q, k, v, c, gates = mconv_prompt(xm3, conv_w, conv_b, w_qkv, w_gate, b_gate, tm=256)
    c0 = jnp.zeros((b, M_HEADS, M_HEAD_DIM, M_HEAD_DIM), F32)
    n0 = jnp.zeros((b, M_HEADS, M_HEAD_DIM), F32)
    m0 = jnp.full((b, M_HEADS), -jnp.inf, F32)
    hn, cf, nf, mf = mlstm_cell(q, k, v, gates, c0, n0, m0, chunk=min(256, t))
    y = mlstm_out(hn.reshape(b * t, D_INNER), c.reshape(b * t, D_INNER), z, x.reshape(b * t, d), norm_w, skip,
                  w_out.astype(BF16), final_w, tm=256, final=final)
    return y.reshape(b, t, d), cf, nf, mf, xm3[:, t - (CONV_W - 1):]


def _mconv_sample_kernel(xm_ref, hist_ref, cw_ref, cb_ref, wbd_ref, wg_ref, bg_ref, q_ref, k_ref, v_ref, c_ref, g_ref):
    xm = xm_ref[...]
    shifted = [hist_ref[j] for j in range(CONV_W - 1)] + [xm]
    _mconv_body(shifted, xm, cw_ref, cb_ref, wbd_ref, wg_ref, bg_ref, q_ref, k_ref, v_ref, c_ref, g_ref, True)


def mconv_sample(xm, hist, conv_w, conv_b, w_qkv, w_gate, b_gate):
    b = xm.shape[0]
    wbd, wg, bg = _mlstm_small_weights(w_qkv, w_gate, b_gate, F32)
    cb = conv_b.reshape(1, D_INNER)
    full = lambda a: pl.BlockSpec(a.shape, lambda i: (0,) * a.ndim)
    args = (xm, hist, conv_w, cb, wbd, wg, bg)
    row = pl.BlockSpec((b, D_INNER), lambda i: (0, 0))
    return pl.pallas_call(
        _mconv_sample_kernel, grid=(1,), in_specs=[full(a) for a in args],
        out_specs=[row, row, row, row, pl.BlockSpec((b, LANES), lambda i: (0, 0))],
        out_shape=[jax.ShapeDtypeStruct((b, D_INNER), F32)] * 4 + [jax.ShapeDtypeStruct((b, LANES), F32)],
        compiler_params=_cparams("arbitrary"), name="mlstm_conv_qkv_sample",
    )(*args)


def _mlstm_step_kernel(q_ref, k_ref, v_ref, g_ref, c0_ref, n0_ref, *rest):
    h_ref, cf_ref, nf_ref, mf_ref = rest[-4:]
    q, k, v = q_ref[0], k_ref[0], v_ref[0]
    g = g_ref[0, 0]
    ig, fl, m0 = g[:, 0:1], g[:, 1:2], g[:, 2:3]
    c0, n0 = c0_ref[0, 0, 0], n0_ref[0, 0]
    m_new = jnp.maximum(fl + m0, ig)
    decay = jnp.exp(fl + m0 - m_new)
    sw = jnp.exp(ig - m_new)
    s = jnp.sum(q * k, axis=-1, keepdims=True) * sw
    half = LANES // 2
    qk_col = jnp.concatenate([jnp.broadcast_to(q, (half, M_HEAD_DIM)), jnp.broadcast_to(k, (half, M_HEAD_DIM))], axis=0).T
    q_col, k_col = qk_col[:, 0:1], qk_col[:, half:half + 1]
    qc = jnp.sum(q_col * c0, axis=0, keepdims=True)
    num = s * v + decay * qc
    den = s + decay * jnp.sum(q * n0, axis=-1, keepdims=True)
    hc = num / jnp.maximum(jnp.abs(den), jnp.exp(-m_new))
    h_ref[0] = hc * lax.rsqrt(jnp.mean(hc * hc, axis=-1, keepdims=True) + RMS_EPS)
    cf_ref[0, 0, 0] = decay * c0 + (k_col * sw) * v
    nf_ref[0, 0] = decay * n0 + sw * k
    mf_ref[0, 0] = jnp.broadcast_to(m_new, (1, LANES))


def mlstm_step(layer, q, k, v, gates, c_all, n0, m0, c_new_all):
    b = q.shape[0]
    gsm = jnp.stack([gates[:, :M_HEADS], gates[:, M_HEADS:2 * M_HEADS], m0], axis=-1)
    gsm = jnp.pad(gsm, ((0, 0), (0, 0), (0, LANES - 3))).reshape(b, M_HEADS, 1, LANES)
    r3 = lambda a: a.reshape(b, 1, D_INNER)
    head = pl.BlockSpec((1, 1, M_HEAD_DIM), lambda i, h: (i, 0, h))
    st = lambda *blk: pl.BlockSpec((1, 1) + blk, lambda i, h: (i, h, 0, 0))
    c_blk = pl.BlockSpec((1, 1, 1, M_HEAD_DIM, M_HEAD_DIM), lambda i, h: (layer, i, h, 0, 0))
    args = [r3(q), r3(k), r3(v), gsm, c_all, n0.reshape(b, M_HEADS, 1, M_HEAD_DIM)]
    in_specs = [head, head, head, st(1, LANES), c_blk, st(1, M_HEAD_DIM)]
    aliases = {}
    if c_new_all is not None:
        args.append(c_new_all)
        in_specs.append(pl.BlockSpec(memory_space=pl.ANY))
        aliases = {len(args) - 1: 1}
    hn, cf, nf, mf = pl.pallas_call(
        _mlstm_step_kernel, grid=(b, M_HEADS), in_specs=in_specs,
        out_specs=[head, c_blk, st(1, M_HEAD_DIM), st(1, LANES)],
        out_shape=[jax.ShapeDtypeStruct((b, 1, D_INNER), F32), jax.ShapeDtypeStruct(c_all.shape, F32),
                   jax.ShapeDtypeStruct((b, M_HEADS, 1, M_HEAD_DIM), F32), jax.ShapeDtypeStruct((b, M_HEADS, 1, LANES), F32)],
        input_output_aliases=aliases,
        compiler_params=_cparams("parallel", "parallel"), name="mlstm_step",
    )(*args)
    return hn.reshape(b, D_INNER), cf, nf.reshape(b, M_HEADS, M_HEAD_DIM), mf[:, :, 0, 0]


def mlstm_sample_layer(layer, x, conv_state, c_all, c_new_all, n0, m0, nw, w_in, conv_w, conv_b, w_qkv, w_gate, b_gate, norm_w,
                       skip, w_out, final_w, *, final):
    b = x.shape[0]
    xm, z = rms_proj(x, nw, [w_in[:, :D_INNER], w_in[:, D_INNER:]], (None, None), tm=b, exact=True, name="mlstm_in_proj_sample")
    hist = conv_state.transpose(1, 0, 2)
    q, k, v, c, gates = mconv_sample(xm, hist, conv_w, conv_b, w_qkv, w_gate, b_gate)
    hn, cf, nf, mf = mlstm_step(layer, q, k, v, gates, c_all, n0, m0, c_new_all)
    y = mlstm_out(hn, c, z, x, norm_w, skip, w_out, final_w, tm=b, final=final, exact=True)
    conv_new = jnp.concatenate([hist[1:], xm[None]], axis=0).transpose(1, 0, 2)
    return y, cf, nf, mf, conv_new


SAMPLE_ROWS = 8
PAGES_PER_STEP = 16
N_BLK_PAD = 256


def _row_scalars(vals):
    row = lax.broadcasted_iota(jnp.int32, (SAMPLE_ROWS, 1), 0)
    col = jnp.zeros((SAMPLE_ROWS, 1), F32)
    for g, v in enumerate(vals):
        col = jnp.where(row == g, v, col)
    return col


def _nsa_sample_kernel(pt_ref, slopes_ref, *refs, gp, n_groups, past_len):
    cmp_pages, sel_pages = refs[:gp], refs[gp:2 * gp]
    (q_ref, qbd_ref, kvn_ref, cwin_ref, wn_ref, g_ref, wp_ref, pe_ref, w1f_ref, b1_ref, w2_ref, ov_ref,
     ex_ref) = refs[2 * gp:2 * gp + 13]
    o_ref, wout_ref, cmp_s, kc_s, vc_s, bias_s, oc_s, m_s, l_s, acc_s = refs[-10:]
    step = pl.program_id(1)
    t = past_len
    n_seg = past_len // CMP_STRIDE
    n_blk = past_len // SEL_BLK + 1
    span = gp * PAGE_SIZE
    slope_cols = [_row_scalars([slopes_ref[kh * GROUP + g] for g in range(GROUP)]) for kh in range(N_KV_HEADS)]

    @pl.when(step < n_groups)
    def _stash():
        for i in range(gp):
            row0 = pl.multiple_of((step * gp + i) * PAGE_SIZE, PAGE_SIZE)
            for c in range(2):
                for hp in range(N_KV_HEADS // 2):
                    a = cmp_pages[i][0, 0, c, 2 * hp:2 * hp + 2].reshape(2 * HEAD_DIM, PAGE_SIZE)
                    cmp_s[c * 2 + hp, pl.ds(row0, PAGE_SIZE), :] = a.T

    @pl.when(step == n_groups - 1)
    def _compress_and_select():
        for slab in range(4):
            c, hp = divmod(slab, 2)
            x = jnp.concatenate([cmp_s[slab, pl.ds(r, n_seg, stride=CMP_STRIDE), :] for r in range(CMP_STRIDE)], axis=1)
            part = jnp.dot(x.astype(BF16), wp_ref[c], preferred_element_type=F32)
            hidc = jnp.dot(pe_ref[c], w1f_ref[c], preferred_element_type=F32)[0:1] + b1_ref[c]
            for hh in range(2):
                p0 = part[:, hh * 2 * CMP_HID:hh * 2 * CMP_HID + CMP_HID]
                p1 = part[:, hh * 2 * CMP_HID + CMP_HID:(hh + 1) * 2 * CMP_HID]
                hid = hidc + p0 + pltpu.roll(p1, n_seg - 1, 0)
                tok = jnp.dot(_silu(hid).astype(BF16), w2_ref[c], preferred_element_type=F32)
                if c == 0:
                    kc_s[2 * hp + hh] = tok
                else:
                    vc_s[2 * hp + hh] = tok
        cend = lax.broadcasted_iota(jnp.int32, (1, n_seg), 1) * CMP_STRIDE + (CMP_BLK - 1)
        d_c = (t - cend).astype(F32)
        ok = d_c >= 0.0
        row = lax.broadcasted_iota(jnp.int32, (SAMPLE_ROWS, 1), 0)
        blk = lax.broadcasted_iota(jnp.int32, (1, N_BLK_PAD), 1)
        tb = t // SEL_BLK
        forced = (blk == 0) | (blk == tb) | (blk == tb - 1)
        ii = lax.broadcasted_iota(jnp.int32, (N_BLK_PAD, N_BLK_PAD), 0)
        jj = lax.broadcasted_iota(jnp.int32, (N_BLK_PAD, N_BLK_PAD), 1)
        sel_rows = []
        for kh in range(N_KV_HEADS):
            q8 = (q_ref[0, kh] * Q_SCALE).astype(BF16)
            s = lax.dot_general(q8, kc_s[kh].astype(BF16), NT, preferred_element_type=F32) - slope_cols[kh] * d_c
            s = jnp.where(ok, s, NEG)
            m = jnp.max(s, axis=-1, keepdims=True)
            p = jnp.where(ok, jnp.exp(s - m), 0.0)
            pn = (p / jnp.maximum(jnp.sum(p, axis=-1, keepdims=True), 1e-30)).astype(BF16)
            oc_s[kh] = jnp.dot(pn, vc_s[kh].astype(BF16), preferred_element_type=F32)
            pn_heads = jnp.where(row < GROUP, pn, jnp.zeros_like(pn))
            imp = jnp.sum(jnp.dot(pn_heads, ov_ref[...], preferred_element_type=F32), axis=0, keepdims=True)
            score = jnp.where(forced, FORCE_SCORE, jnp.where(blk * SEL_BLK <= t, imp, -1.0))
            score = jnp.where(blk < n_blk, score, -2.0)
            col = jnp.broadcast_to(score, (SAMPLE_ROWS, N_BLK_PAD)).T[:, 0:1]
            beats = (col > score) | ((ii < jj) & (col == score))
            cnt = jnp.sum(jnp.where(beats, 1.0, 0.0), axis=0, keepdims=True)
            sel = jnp.where((cnt < float(min(N_SEL, n_blk))) & (score > -0.5), 1.0, 0.0)
            sel_rows.append(jnp.broadcast_to(sel, (SAMPLE_ROWS, N_BLK_PAD)))
        picked = jnp.dot(jnp.concatenate(sel_rows, axis=0).astype(BF16), ex_ref[...], preferred_element_type=F32)
        key_pos = lax.broadcasted_iota(jnp.int32, (1, past_len), 1)
        bias = jnp.where(picked > 0.5, jnp.concatenate(slope_cols, axis=0) * (key_pos - t).astype(F32), NEG)
        for gi in range(n_groups):
            bias_s[gi] = bias[:, gi * span:(gi + 1) * span]
        m_s[...] = jnp.full(m_s.shape, M_INIT, F32)
        l_s[...] = jnp.zeros(l_s.shape, F32)
        acc_s[...] = jnp.zeros(acc_s.shape, F32)

    @pl.when(step >= n_groups)
    def _selected():
        gb = step - n_groups
        kk = jnp.concatenate([sel_pages[i][0, 0, 0].reshape(KV_WIDTH, PAGE_SIZE) for i in range(gp)], axis=1).astype(BF16)
        vv = jnp.concatenate([sel_pages[i][0, 0, 1].reshape(KV_WIDTH, PAGE_SIZE) for i in range(gp)], axis=1).astype(BF16)
        s = jnp.dot(qbd_ref[0], kk, preferred_element_type=F32) + bias_s[gb]
        m_old = m_s[:, 0:1]
        m_new = jnp.maximum(m_old, jnp.max(s, axis=-1, keepdims=True))
        alpha = jnp.exp(m_old - m_new)
        p = jnp.exp(s - m_new)
        l_s[...] = jnp.broadcast_to(alpha * l_s[:, 0:1] + jnp.sum(p, axis=-1, keepdims=True), l_s.shape)
        acc_s[...] = alpha * acc_s[...] + lax.dot_general(p.astype(BF16), vv, NT, preferred_element_type=F32)
        m_s[...] = jnp.broadcast_to(m_new, m_s.shape)

    @pl.when(step == 2 * n_groups - 1)
    def _finish():
        wr = cwin_ref.shape[-1]
        d_w = wr - lax.broadcasted_iota(jnp.int32, (1, wr), 1)
        ok_w = (d_w <= WINDOW) & (t - d_w >= 0)
        for kh in range(N_KV_HEADS):
            q8 = q_ref[0, kh] * Q_SCALE
            rows = slice(kh * SAMPLE_ROWS, (kh + 1) * SAMPLE_ROWS)
            s_n = jnp.sum(q8 * kvn_ref[0, 2, kh], axis=-1, keepdims=True)
            m_old = m_s[rows, 0:1]
            m_new = jnp.maximum(m_old, s_n)
            alpha = jnp.exp(m_old - m_new)
            p_n = jnp.exp(s_n - m_new)
            l = alpha * l_s[rows, 0:1] + p_n
            acc = acc_s[rows, kh * HEAD_DIM:(kh + 1) * HEAD_DIM]
            o_sel = (alpha * acc + p_n * kvn_ref[0, 3, kh]) / jnp.maximum(l, 1e-30)
            s_w = jnp.dot(q8.astype(BF16), cwin_ref[0, 0, 0, kh].astype(BF16), preferred_element_type=F32)
            s_w = jnp.where(ok_w, s_w - slope_cols[kh] * d_w.astype(F32), NEG)
            s_wn = jnp.sum(q8 * wn_ref[0, 0, kh], axis=-1, keepdims=True)
            m_w = jnp.maximum(jnp.max(s_w, axis=-1, keepdims=True), s_wn)
            p_w = jnp.where(ok_w, jnp.exp(s_w - m_w), 0.0)
            p_wn = jnp.exp(s_wn - m_w)
            l_w = jnp.sum(p_w, axis=-1, keepdims=True) + p_wn
            o_win = (lax.dot_general(p_w.astype(BF16), cwin_ref[0, 0, 1, kh].astype(BF16), NT, preferred_element_type=F32)
                     + p_wn * wn_ref[0, 1, kh]) / jnp.maximum(l_w, 1e-30)
            gts = g_ref[0, kh]
            o_ref[0, kh] = gts[:, 0:1] * oc_s[kh] + gts[:, 1:2] * o_sel + gts[:, 2:3] * o_win
        last = lax.broadcasted_iota(jnp.int32, (HEAD_DIM, wr), 1) == wr - 1
        for c in range(2):
            for kh in range(N_KV_HEADS):
                new8 = jnp.broadcast_to(wn_ref[0, c, kh], (SAMPLE_ROWS, HEAD_DIM))
                new_col = jnp.concatenate([new8, jnp.zeros_like(new8)], axis=1).T[0:HEAD_DIM, 0:1]
                wout_ref[0, 0, c, kh] = jnp.where(last, new_col, pltpu.roll(cwin_ref[0, 0, c, kh], wr - 1, 1))


def _pair_w1(w1):
    wr = w1.reshape(2, CMP_R, CMP_STRIDE, HEAD_DIM, CMP_HID).transpose(0, 2, 3, 1, 4)
    wp = jnp.einsum("crdje,hk->crhdkje", wr, jnp.eye(2, dtype=w1.dtype))
    return wp.reshape(2, CMP_STRIDE * 2 * HEAD_DIM, 2 * CMP_R * CMP_HID).astype(BF16)


def nsa_sample_attention(layer, q, kv, win, gates, cache_kv, cache_win, win_all, page_table, pe, w1, b1, w2):
    b = q.shape[0]
    n_pages = page_table.shape[1]
    past_len = n_pages * PAGE_SIZE
    gp = PAGES_PER_STEP
    assert n_pages % gp == 0 and cache_win.shape[2] == WINDOW and past_len // SEL_BLK + 1 <= N_BLK_PAD
    n_groups = n_pages // gp
    n_seg = past_len // CMP_STRIDE
    pad_rows = lambda a: jnp.pad(a, ((0, 0), (0, 0), (0, SAMPLE_ROWS - GROUP), (0, 0)))
    q4 = pad_rows(q.reshape(b, N_KV_HEADS, GROUP, HEAD_DIM))
    g4 = pad_rows(gates[:, :3 * N_HEADS].reshape(b, N_KV_HEADS, GROUP, 3))
    q_bd = jnp.einsum("bkgd,kj->bkgjd", q4 * Q_SCALE, jnp.eye(N_KV_HEADS, dtype=F32))
    q_bd = q_bd.reshape(b, N_KV_HEADS * SAMPLE_ROWS, KV_WIDTH).astype(BF16)
    kvn = kv.reshape(b, 4, N_KV_HEADS, 1, HEAD_DIM)
    wn = win.reshape(b, 2, N_KV_HEADS, 1, HEAD_DIM)
    cache_t = cache_kv.transpose(0, 1, 3, 4, 5, 2)
    cwin_t = cache_win.transpose(0, 1, 3, 4, 5, 2)
    start = np.arange(n_seg) * CMP_STRIDE
    blk = np.arange(N_BLK_PAD) * SEL_BLK
    ov = jnp.asarray((start[:, None] < blk[None, :] + SEL_BLK) & (start[:, None] + CMP_BLK > blk[None, :]), dtype=BF16)
    pe8 = jnp.broadcast_to(pe.reshape(2, 1, CMP_BLK * HEAD_DIM), (2, 8, CMP_BLK * HEAD_DIM)).astype(BF16)
    w1f = w1.reshape(2, CMP_BLK * HEAD_DIM, CMP_HID).astype(BF16)
    expand = (jnp.arange(N_BLK_PAD)[:, None] == jnp.arange(past_len)[None, :] // SEL_BLK).astype(BF16)
    consts = (_pair_w1(w1), pe8, w1f, b1.reshape(2, 1, CMP_HID), w2.astype(BF16), ov, expand)
    page_blk = (1, 1, 2, N_KV_HEADS, HEAD_DIM, PAGE_SIZE)

    def cmp_map(i):
        return lambda bi, s, pt, sl: (layer, pt[bi * n_pages + jnp.minimum(s, n_groups - 1) * gp + i], 0, 0, 0, 0)

    def sel_map(i):
        return lambda bi, s, pt, sl: (layer, pt[bi * n_pages + jnp.maximum(s - n_groups, 0) * gp + i], 1, 0, 0, 0)

    per_b = lambda a: pl.BlockSpec((1,) + a.shape[1:], lambda bi, s, pt, sl: (bi,) + (0,) * (a.ndim - 1))
    const = lambda a: pl.BlockSpec(a.shape, lambda bi, s, pt, sl: (0,) * a.ndim)
    in_specs = [pl.BlockSpec(page_blk, cmp_map(i)) for i in range(gp)] + [pl.BlockSpec(page_blk, sel_map(i)) for i in range(gp)]
    in_specs += [per_b(q4), per_b(q_bd), per_b(kvn),
                 pl.BlockSpec((1, 1) + cwin_t.shape[2:], lambda bi, s, pt, sl: (layer, bi, 0, 0, 0, 0)),
                 per_b(wn), per_b(g4)] + [const(a) for a in consts]
    all_rows = N_KV_HEADS * SAMPLE_ROWS
    small = pltpu.VMEM((all_rows, LANES), F32)
    win_blk = pl.BlockSpec((1, 1) + cwin_t.shape[2:], lambda bi, s, pt, sl: (layer, bi, 0, 0, 0, 0))
    args = [page_table.reshape(-1), _alibi_slopes(), *([cache_t] * (2 * gp)), q4, q_bd, kvn, cwin_t, wn, g4, *consts]
    aliases = {}
    if win_all is not None:
        args.append(win_all)
        in_specs.append(pl.BlockSpec(memory_space=pl.ANY))
        aliases = {len(args) - 1: 1}
    grid_spec = pltpu.PrefetchScalarGridSpec(
        num_scalar_prefetch=2, grid=(b, 2 * n_groups), in_specs=in_specs, out_specs=[per_b(q4), win_blk],
        scratch_shapes=[pltpu.VMEM((4, past_len, LANES), F32),
                        pltpu.VMEM((N_KV_HEADS, n_seg, HEAD_DIM), F32), pltpu.VMEM((N_KV_HEADS, n_seg, HEAD_DIM), F32),
                        pltpu.VMEM((n_groups, all_rows, gp * PAGE_SIZE), F32),
                        pltpu.VMEM((N_KV_HEADS, SAMPLE_ROWS, HEAD_DIM), F32), small, small,
                        pltpu.VMEM((all_rows, KV_WIDTH), F32)])
    o, win_all = pl.pallas_call(
        functools.partial(_nsa_sample_kernel, gp=gp, n_groups=n_groups, past_len=past_len), grid_spec=grid_spec,
        out_shape=[jax.ShapeDtypeStruct(q4.shape, F32), jax.ShapeDtypeStruct(cwin_t.shape, F32)],
        input_output_aliases=aliases,
        compiler_params=_cparams("parallel", "arbitrary"), name="nsa_sample_attn",
    )(*args)
    return o[:, :, :GROUP].reshape(b, ATT_WIDTH), win_all


def nsa_sample_layer(layer, x, cache_kv, cache_win, win_all, page_table, nw, w_in, w_out, pe, w1, b1, w2):
    b = x.shape[0]
    q, kv, win, gates, z = rms_proj(x, nw, _split_nsa_w_in(w_in, F32), NSA_ACTS, tm=b, exact=True, name="nsa_in_proj_sample")
    o, win_all = nsa_sample_attention(layer, q, kv, win, gates, cache_kv, cache_win, win_all, page_table, pe, w1, b1, w2)
    zero = jnp.zeros_like(o)
    y = nsa_out(o, zero, zero, z, x, w_out, tm=b, exact=True)
    return y, kv.reshape(b, 1, 4, N_KV_HEADS, HEAD_DIM), win_all


def kernel(x_prompt, x_sample, cache_kv, cache_win, state_C, state_n, state_m, state_conv, page_table, norm_w, final_norm_w,
           nsa_w_in, nsa_w_out, nsa_cmp_pe, nsa_cmp_w1, nsa_cmp_b1, nsa_cmp_w2, m_w_in, m_conv_w, m_conv_b, m_w_qkv, m_w_gate,
           m_b_gate, m_norm_w, m_skip, m_w_out):
    assert DEPTH % 2 == 0 and x_sample.shape[1] == 1
    yp, ys = x_prompt, x_sample[:, 0]
    outs = {name: [] for name in ("kv_s", "win_p", "C_p", "n_p", "n_s", "m_p", "m_s", "cv_p", "cv_s")}
    c_sample = None
    kvt_prompt = None
    win_sample_t = None
    n_nsa = (DEPTH + 1) // 2
    for i in range(DEPTH):
        l = i // 2
        if i % 2 == 0:
            prm = (nsa_w_in[l], nsa_w_out[l], nsa_cmp_pe[l], nsa_cmp_w1[l], nsa_cmp_b1[l], nsa_cmp_w2[l])
            yp, kvt_prompt, wp = nsa_prompt_layer_t(l, n_nsa, yp, kvt_prompt, norm_w[i], *prm)
            ys, kvs, win_sample_t = nsa_sample_layer(l, ys, cache_kv, cache_win, win_sample_t, page_table, norm_w[i], *prm)
            for name, val in (("kv_s", kvs), ("win_p", wp)):
                outs[name].append(val)
        else:
            prm = (m_w_in[l], m_conv_w[l], m_conv_b[l], m_w_qkv[l], m_w_gate[l], m_b_gate[l], m_norm_w[l], m_skip[l], m_w_out[l])
            final = i == DEPTH - 1
            yp, cp, np_, mp, cvp = mlstm_prompt_layer(yp, norm_w[i], *prm, final_norm_w, final=final)
            ys, c_sample, ns, ms, cvs = mlstm_sample_layer(l, ys, state_conv[l], state_C, c_sample, state_n[l], state_m[l],
                                                           norm_w[i], *prm, final_norm_w, final=final)
            for name, val in (("C_p", cp), ("n_p", np_), ("n_s", ns), ("m_p", mp), ("m_s", ms), ("cv_p", cvp), ("cv_s", cvs)):
                outs[name].append(val)
    st = {name: jnp.stack(vals) for name, vals in outs.items()}
    b, t = x_prompt.shape[:2]
    kv_prompt = kvt_prompt.reshape(n_nsa, b, 4, N_KV_HEADS, HEAD_DIM, t).transpose(0, 1, 5, 2, 3, 4)
    win_sample = win_sample_t.transpose(0, 1, 5, 2, 3, 4)
    return (yp, ys[:, None], kv_prompt, st["kv_s"], st["win_p"], win_sample, st["C_p"], c_sample, st["n_p"], st["n_s"],
            st["m_p"], st["m_s"], st["cv_p"], st["cv_s"])
```

```python
import functools
import math

import jax
import jax.numpy as jnp
import numpy as np
from jax import lax
from jax.experimental import pallas as pl
from jax.experimental.pallas import tpu as pltpu

F32 = jnp.float32
BF16 = jnp.bfloat16
HI = lax.Precision.HIGHEST

D_MODEL = 1024
DEPTH = 4
N_HEADS = 16
HEAD_DIM = 64
N_KV_HEADS = 4
GROUP = N_HEADS // N_KV_HEADS
ATT_WIDTH = N_HEADS * HEAD_DIM
KV_WIDTH = N_KV_HEADS * HEAD_DIM
CMP_BLK = 32
CMP_STRIDE = 16
CMP_R = CMP_BLK // CMP_STRIDE
CMP_HID = 2 * HEAD_DIM
SEL_BLK = 64
N_SEL = 16
WINDOW = 512
FORCE_SCORE = 1.0e4
D_INNER = 2 * D_MODEL
M_HEADS = 4
M_HEAD_DIM = D_INNER // M_HEADS
CONV_W = 4
QKV_BLK = 4
RMS_EPS = 1e-6
PAGE_SIZE = 128

LANES = 128
VMEM_LIMIT = 56 * 1024 * 1024
NEG = -1.0e30
M_INIT = -5.0e29
Q_SCALE = HEAD_DIM ** -0.5
NT = (((1,), (1,)), ((), ()))
TN = (((0,), (0,)), ((), ()))


def _cparams(*sem):
    return pltpu.CompilerParams(dimension_semantics=sem, vmem_limit_bytes=VMEM_LIMIT)


def _silu(x):
    return x * jax.nn.sigmoid(x)


def _alibi_slopes():
    return jnp.asarray(np.exp2(-8.0 * np.arange(1, N_HEADS + 1) / N_HEADS), dtype=F32)


def _rmsnorm(x, w):
    return x * lax.rsqrt(jnp.mean(x * x, axis=-1, keepdims=True) + RMS_EPS) * w


def _rms_proj_kernel(x_ref, nw_ref, *refs, acts, n_chunk, exact):
    n = len(acts)
    w_refs, o_refs = refs[:n], refs[n:]
    h = _rmsnorm(x_ref[...], nw_ref[...])
    if not exact:
        h = h.astype(BF16)
    for w_ref, o_ref, act in zip(w_refs, o_refs, acts):
        width = w_ref.shape[1]
        for n0 in range(0, width, n_chunk):
            n1 = min(width, n0 + n_chunk)
            if exact:
                y = jnp.dot(h, w_ref[:, n0:n1], preferred_element_type=F32, precision=HI)
            else:
                y = jnp.dot(h, w_ref[:, n0:n1], preferred_element_type=F32)
            if act == "sigmoid":
                y = jax.nn.sigmoid(y)
            o_ref[:, n0:n1] = y.astype(o_ref.dtype)


def rms_proj(x, nw, weights, acts, *, tm, exact=False, out_dtypes=None, name="rms_proj"):
    m, k = x.shape
    assert m % tm == 0
    out_dtypes = out_dtypes or (F32,) * len(weights)
    in_specs = [pl.BlockSpec((tm, k), lambda i: (i, 0)), pl.BlockSpec((1, k), lambda i: (0, 0))]
    in_specs += [pl.BlockSpec(w.shape, lambda i: (0, 0)) for w in weights]
    out_specs = [pl.BlockSpec((tm, w.shape[1]), lambda i: (i, 0)) for w in weights]
    out_shape = [jax.ShapeDtypeStruct((m, w.shape[1]), dt) for w, dt in zip(weights, out_dtypes)]
    return pl.pallas_call(
        functools.partial(_rms_proj_kernel, acts=tuple(acts), n_chunk=512, exact=exact),
        grid=(m // tm,), in_specs=in_specs, out_specs=out_specs, out_shape=out_shape,
        compiler_params=_cparams("parallel"), name=name,
    )(x, nw.reshape(1, k), *weights)


def _compress_kernel(x_ref, w1_ref, pe_ref, w1f_ref, b1_ref, w2_ref, o_ref, *, n_seg):
    x = x_ref[0, 0, 0]
    p0 = jnp.dot(x, w1_ref[0, 0], preferred_element_type=F32)
    p1 = jnp.dot(x, w1_ref[0, 1], preferred_element_type=F32)
    hidc = jnp.dot(pe_ref[0], w1f_ref[0], preferred_element_type=F32)[0:1] + b1_ref[0]
    hid = hidc + p0 + pltpu.roll(p1, n_seg - 1, 0)
    o_ref[0, 0, 0] = jnp.dot(_silu(hid).astype(BF16), w2_ref[0], preferred_element_type=F32)


def compress_tokens(x, pe, w1, b1, w2):
    b, _, hkv, n_seg, _ = x.shape
    w1r = w1.reshape(2, CMP_R, CMP_STRIDE * HEAD_DIM, CMP_HID).astype(BF16)
    w1f = w1.reshape(2, CMP_BLK * HEAD_DIM, CMP_HID).astype(BF16)
    pe8 = jnp.broadcast_to(pe.reshape(2, 1, CMP_BLK * HEAD_DIM), (2, 8, CMP_BLK * HEAD_DIM)).astype(BF16)
    return pl.pallas_call(
        functools.partial(_compress_kernel, n_seg=n_seg),
        grid=(b, 2, hkv),
        in_specs=[
            pl.BlockSpec((1, 1, 1, n_seg, CMP_STRIDE * HEAD_DIM), lambda i, c, h: (i, c, h, 0, 0)),
            pl.BlockSpec((1, CMP_R, CMP_STRIDE * HEAD_DIM, CMP_HID), lambda i, c, h: (c, 0, 0, 0)),
            pl.BlockSpec((1, 8, CMP_BLK * HEAD_DIM), lambda i, c, h: (c, 0, 0)),
            pl.BlockSpec((1, CMP_BLK * HEAD_DIM, CMP_HID), lambda i, c, h: (c, 0, 0)),
            pl.BlockSpec((1, 1, CMP_HID), lambda i, c, h: (c, 0, 0)),
            pl.BlockSpec((1, CMP_HID, HEAD_DIM), lambda i, c, h: (c, 0, 0)),
        ],
        out_specs=pl.BlockSpec((1, 1, 1, n_seg, HEAD_DIM), lambda i, c, h: (i, c, h, 0, 0)),
        out_shape=jax.ShapeDtypeStruct((b, 2, hkv, n_seg, HEAD_DIM), F32),
        compiler_params=_cparams("parallel", "parallel", "parallel"), name="nsa_compress",
    )(x, w1r, pe8, w1f, b1.reshape(2, 1, CMP_HID), w2.astype(BF16))


def _topk_mask_t(score_t, n_valid, k):
    r = score_t.shape[0]
    jidx = lax.broadcasted_iota(jnp.int32, (r, 1), 0)
    cnt = jnp.zeros(score_t.shape, F32)
    for i in range(n_valid):
        row = score_t[i:i + 1, :]
        beats = (row > score_t) | ((jidx > i) & (row == score_t))
        cnt = cnt + jnp.where(beats, 1.0, 0.0)
    return cnt < float(k)


def _cmp_sel_kernel(slopes_ref, q_ref, kc_ref, vc_ref, g_ref, ov_ref, oc_ref, sel_ref, *, tq, n_seg, n_blk):
    kh = pl.program_id(1)
    t0 = pl.program_id(2) * tq
    t_col = t0 + lax.broadcasted_iota(jnp.int32, (tq, 1), 0)
    cend = lax.broadcasted_iota(jnp.int32, (1, n_seg), 1) * CMP_STRIDE + (CMP_BLK - 1)
    d_c = (t_col - cend).astype(F32)
    ok = d_c >= 0.0
    kc = kc_ref[0, 0, 0].astype(BF16)
    vc = vc_ref[0, 0, 0].astype(BF16)
    imp = jnp.zeros((tq, LANES), F32)
    outs = []
    for g in range(GROUP):
        qg = (q_ref[0, :, g * HEAD_DIM:(g + 1) * HEAD_DIM] * Q_SCALE).astype(BF16)
        s = lax.dot_general(qg, kc, NT, preferred_element_type=F32) - slopes_ref[kh * GROUP + g] * d_c
        s = jnp.where(ok, s, NEG)
        m = jnp.max(s, axis=-1, keepdims=True)
        p = jnp.where(ok, jnp.exp(s - m), 0.0)
        pn = (p / jnp.maximum(jnp.sum(p, axis=-1, keepdims=True), 1e-30)).astype(BF16)
        o = jnp.dot(pn, vc, preferred_element_type=F32)
        outs.append(o * g_ref[0, 0, :, 3 * g:3 * g + 1])
        imp = imp + jnp.dot(pn, ov_ref[...], preferred_element_type=F32)
    oc_ref[0] = jnp.concatenate(outs, axis=1)
    blk = lax.broadcasted_iota(jnp.int32, (1, LANES), 1)
    tb = lax.shift_right_logical(t_col, int(math.log2(SEL_BLK)))
    forced = (blk == 0) | (blk == tb) | (blk == tb - 1)
    score = jnp.where(forced, FORCE_SCORE, jnp.where(blk * SEL_BLK <= t_col, imp, -1.0))
    score = jnp.where(blk < n_blk, score, -2.0)
    score_t = score.T
    sel_t = _topk_mask_t(score_t, n_blk, min(N_SEL, n_blk)) & (score_t > -0.5)
    sel_ref[0, 0] = jnp.where(sel_t, 1.0, 0.0).T


def cmp_select(q, kvc, gates_r, slopes, ov, *, tq):
    b, t, _ = q.shape
    n_seg = kvc.shape[3]
    n_blk = t // SEL_BLK
    grid_spec = pltpu.PrefetchScalarGridSpec(
        num_scalar_prefetch=1, grid=(b, N_KV_HEADS, t // tq),
        in_specs=[
            pl.BlockSpec((1, tq, GROUP * HEAD_DIM), lambda i, k, j, s: (i, j, k)),
            pl.BlockSpec((1, 1, 1, n_seg, HEAD_DIM), lambda i, k, j, s: (i, 0, k, 0, 0)),
            pl.BlockSpec((1, 1, 1, n_seg, HEAD_DIM), lambda i, k, j, s: (i, 1, k, 0, 0)),
            pl.BlockSpec((1, 1, tq, 3 * GROUP), lambda i, k, j, s: (i, k, j, 0)),
            pl.BlockSpec((n_seg, LANES), lambda i, k, j, s: (0, 0)),
        ],
        out_specs=[
            pl.BlockSpec((1, tq, GROUP * HEAD_DIM), lambda i, k, j, s: (i, j, k)),
            pl.BlockSpec((1, 1, tq, LANES), lambda i, k, j, s: (i, k, j, 0)),
        ],
    )
    return pl.pallas_call(
        functools.partial(_cmp_sel_kernel, tq=tq, n_seg=n_seg, n_blk=n_blk),
        grid_spec=grid_spec,
        out_shape=[jax.ShapeDtypeStruct((b, t, ATT_WIDTH), F32), jax.ShapeDtypeStruct((b, N_KV_HEADS, t, LANES), F32)],
        compiler_params=_cparams("parallel", "parallel", "parallel"), name="nsa_cmp_select",
    )(slopes, q, kvc, kvc, gates_r, ov)


def _cmp_to_sel_matrix(n_rows, n_blk):
    start = np.arange(n_rows) * CMP_STRIDE
    blk = np.arange(LANES) * SEL_BLK
    ov = (start[:, None] < blk[None, :] + SEL_BLK) & (start[:, None] + CMP_BLK > blk[None, :]) & (np.arange(LANES)[None, :] < n_blk)
    return jnp.asarray(ov, dtype=BF16)


def _stack_heads(q_blk, extra=None):
    parts = []
    for g in range(GROUP):
        qg = q_blk[:, g * HEAD_DIM:(g + 1) * HEAD_DIM] * Q_SCALE
        if extra is not None:
            qg = jnp.concatenate([qg, extra], axis=1)
        parts.append(qg)
    return jnp.concatenate(parts, axis=0).astype(BF16)


def _head_cols(fn, tq):
    return jnp.concatenate([jnp.full((tq, 1), fn(g), F32) for g in range(GROUP)], axis=0)


def _sel_attn_kernel(slopes_ref, q_ref, sel_ref, ka_ref, v_ref, g_ref, o_ref, *, tq, tk):
    kh = pl.program_id(1)
    t0 = pl.program_id(2) * tq
    mask_feat = (sel_ref[0, 0][:, 0:HEAD_DIM] - 1.0) * 1.0e30
    qa = _stack_heads(q_ref[0], mask_feat)
    slope_col = _head_cols(lambda g: slopes_ref[kh * GROUP + g], tq)
    t_col = t0 + lax.broadcasted_iota(jnp.int32, (tq, 1), 0)
    t_col4 = jnp.concatenate([t_col] * GROUP, axis=0)

    def body(kt, carry):
        m, l, acc = carry
        ks = pl.multiple_of(kt * tk, tk)
        s = lax.dot_general(qa, ka_ref[0, 0, pl.ds(ks, tk), :], NT, preferred_element_type=F32)
        pos = ks + lax.broadcasted_iota(jnp.int32, (1, tk), 1)
        s = s + slope_col * (pos - t0).astype(F32)
        s = jnp.where(pos <= t_col4, s, NEG)
        m_new = jnp.maximum(m, jnp.max(s, axis=-1, keepdims=True))
        alpha = jnp.exp(m - m_new)
        p = jnp.exp(s - m_new)
        l = alpha * l + jnp.sum(p, axis=-1, keepdims=True)
        acc = alpha * acc + jnp.dot(p.astype(BF16), v_ref[0, 0, pl.ds(ks, tk), :], preferred_element_type=F32)
        return m_new, l, acc

    n_kt = (t0 + tq + tk - 1) // tk
    init = (jnp.full((GROUP * tq, 1), M_INIT, F32), jnp.zeros((GROUP * tq, 1), F32), jnp.zeros((GROUP * tq, HEAD_DIM), F32))
    _, l, acc = lax.fori_loop(0, n_kt, body, init)
    o = acc / jnp.maximum(l, 1e-30)
    o_ref[0] = jnp.concatenate(
        [o[g * tq:(g + 1) * tq] * g_ref[0, 0, :, 3 * g + 1:3 * g + 2] for g in range(GROUP)], axis=1)


def sel_attention(q, sel, k_aug, v_sel, gates_r, slopes, *, tq, tk):
    b, t, _ = q.shape
    grid_spec = pltpu.PrefetchScalarGridSpec(
        num_scalar_prefetch=1, grid=(b, N_KV_HEADS, t // tq),
        in_specs=[
            pl.BlockSpec((1, tq, GROUP * HEAD_DIM), lambda i, k, j, s: (i, j, k)),
            pl.BlockSpec((1, 1, tq, LANES), lambda i, k, j, s: (i, k, j, 0)),
            pl.BlockSpec((1, 1, t, LANES), lambda i, k, j, s: (i, k, 0, 0)),
            pl.BlockSpec((1, 1, t, HEAD_DIM), lambda i, k, j, s: (i, k, 0, 0)),
            pl.BlockSpec((1, 1, tq, 3 * GROUP), lambda i, k, j, s: (i, k, j, 0)),
        ],
        out_specs=pl.BlockSpec((1, tq, GROUP * HEAD_DIM), lambda i, k, j, s: (i, j, k)),
    )
    return pl.pallas_call(
        functools.partial(_sel_attn_kernel, tq=tq, tk=tk), grid_spec=grid_spec,
        out_shape=jax.ShapeDtypeStruct((b, t, ATT_WIDTH), F32),
        compiler_params=_cparams("parallel", "parallel", "arbitrary"), name="nsa_sel_attn",
    )(slopes, q, sel, k_aug, v_sel, gates_r)


def _win_attn_kernel(slopes_ref, q_ref, k_ref, v_ref, g_ref, o_ref, *, tq, span):
    kh = pl.program_id(1)
    t0 = pl.program_id(2) * tq
    qs = _stack_heads(q_ref[0])
    slope_col = _head_cols(lambda g: slopes_ref[kh * GROUP + g], tq)
    t_col = t0 + lax.broadcasted_iota(jnp.int32, (tq, 1), 0)
    t_col4 = jnp.concatenate([t_col] * GROUP, axis=0)
    start = pl.multiple_of(jnp.maximum(t0 + tq - span, 0), tq)
    s = lax.dot_general(qs, k_ref[0, 0, pl.ds(start, span), :], NT, preferred_element_type=F32)
    d = t_col4 - (start + lax.broadcasted_iota(jnp.int32, (1, span), 1))
    ok = (d >= 0) & (d <= WINDOW)
    s = jnp.where(ok, s - slope_col * d.astype(F32), NEG)
    m = jnp.max(s, axis=-1, keepdims=True)
    p = jnp.where(ok, jnp.exp(s - m), 0.0)
    l = jnp.sum(p, axis=-1, keepdims=True)
    o = jnp.dot(p.astype(BF16), v_ref[0, 0, pl.ds(start, span), :], preferred_element_type=F32) / jnp.maximum(l, 1e-30)
    o_ref[0] = jnp.concatenate(
        [o[g * tq:(g + 1) * tq] * g_ref[0, 0, :, 3 * g + 2:3 * g + 3] for g in range(GROUP)], axis=1)


def win_attention(q, k_win, v_win, gates_r, slopes, *, tq):
    b, t, _ = q.shape
    span = WINDOW + tq
    assert t >= span and span % tq == 0
    grid_spec = pltpu.PrefetchScalarGridSpec(
        num_scalar_prefetch=1, grid=(b, N_KV_HEADS, t // tq),
        in_specs=[
            pl.BlockSpec((1, tq, GROUP * HEAD_DIM), lambda i, k, j, s: (i, j, k)),
            pl.BlockSpec((1, 1, t, HEAD_DIM), lambda i, k, j, s: (i, k, 0, 0)),
            pl.BlockSpec((1, 1, t, HEAD_DIM), lambda i, k, j, s: (i, k, 0, 0)),
            pl.BlockSpec((1, 1, tq, 3 * GROUP), lambda i, k, j, s: (i, k, j, 0)),
        ],
        out_specs=pl.BlockSpec((1, tq, GROUP * HEAD_DIM), lambda i, k, j, s: (i, j, k)),
    )
    return pl.pallas_call(
        functools.partial(_win_attn_kernel, tq=tq, span=span), grid_spec=grid_spec,
        out_shape=jax.ShapeDtypeStruct((b, t, ATT_WIDTH), F32),
        compiler_params=_cparams("parallel", "parallel", "arbitrary"), name="nsa_win_attn",
    )(slopes, q, k_win, v_win, gates_r)


def _nsa_out_kernel(oc_ref, os_ref, ow_ref, z_ref, x_ref, w_ref, y_ref, *, exact):
    a = (oc_ref[...] + os_ref[...] + ow_ref[...]) * _silu(z_ref[...])
    if exact:
        y_ref[...] = x_ref[...] + jnp.dot(a, w_ref[...], preferred_element_type=F32, precision=HI)
    else:
        y_ref[...] = x_ref[...] + jnp.dot(a.astype(BF16), w_ref[...], preferred_element_type=F32)


def nsa_out(o_c, o_s, o_w, z, x, w_out, *, tm, exact=False):
    m, d = x.shape
    row = pl.BlockSpec((tm, d), lambda i: (i, 0))
    return pl.pallas_call(
        functools.partial(_nsa_out_kernel, exact=exact), grid=(m // tm,),
        in_specs=[row, row, row, row, row, pl.BlockSpec(w_out.shape, lambda i: (0, 0))],
        out_specs=row, out_shape=jax.ShapeDtypeStruct((m, d), F32),
        compiler_params=_cparams("parallel"), name="nsa_out",
    )(o_c, o_s, o_w, z, x, w_out)


LOG2E = math.log2(math.e)
Q_SCALE2 = Q_SCALE * LOG2E
ALIBI_ROWS = 16
BIG = 1.0e30
ONES_ROWS = 16
TQ = 256
TK = 512
WCH = 128


def _alibi_table():
    s = _alibi_slopes() * LOG2E
    s1 = s.astype(BF16).astype(F32)
    s2 = (s - s1).astype(BF16).astype(F32)
    s3 = (s - s1 - s2).astype(BF16).astype(F32)
    tab = jnp.stack([SEL_BLK * s1, SEL_BLK * s2, SEL_BLK * s3, s1, s2, s3, jnp.full_like(s, -BIG), jnp.zeros_like(s)], axis=1)
    return tab.reshape(-1)


def _pos_features(pos, valid, width):
    lane = lax.broadcasted_iota(jnp.int32, (pos.shape[0], width), 1)
    blk = lax.shift_right_logical(pos, int(math.log2(SEL_BLK))).astype(F32)
    rem = (pos & (SEL_BLK - 1)).astype(F32)
    f = jnp.where(lane < 3, blk, jnp.where(lane < 6, rem, 0.0))
    return jnp.where(lane == 6, jnp.where(valid, 0.0, 1.0), f)


def _nsa_in_proj_kernel(x_ref, nw_ref, wt_ref, wn_ref, *rest):
    qt_ref, kvt_ref, wint_ref, gt_ref, zt_ref, kn_ref, kwn_ref = rest[-7:]
    h = _rmsnorm(x_ref[0], nw_ref[...]).astype(BF16)

    def nt(r0, r1):
        return lax.dot_general(wt_ref[r0:r1, :], h, NT, preferred_element_type=F32)

    o1, o2, o3 = ATT_WIDTH, ATT_WIDTH + 4 * KV_WIDTH, ATT_WIDTH + 6 * KV_WIDTH
    o4 = o3 + LANES
    for r0 in range(0, o1, 512):
        qt_ref[0, r0:r0 + 512, :] = (nt(r0, r0 + 512) * Q_SCALE2).astype(BF16)
    for r0 in range(o1, o2, 512):
        kvt_ref[0, 0, r0 - o1:r0 - o1 + 512, :] = nt(r0, r0 + 512)
    wint_ref[0] = nt(o2, o3)
    gt_ref[0] = jax.nn.sigmoid(nt(o3, o4))
    for r0 in range(o4, o4 + ATT_WIDTH, 512):
        zt_ref[0, r0 - o4:r0 - o4 + 512, :] = nt(r0, r0 + 512).astype(zt_ref.dtype)
    yn = jnp.dot(h, wn_ref[...], preferred_element_type=F32)
    for j in range(3):
        for hh in range(N_KV_HEADS):
            c0 = (j * N_KV_HEADS + hh) * HEAD_DIM
            kn_ref[0, j, hh] = yn[:, c0:c0 + HEAD_DIM].astype(BF16)
    for hh in range(N_KV_HEADS):
        c0 = (3 * N_KV_HEADS + hh) * HEAD_DIM
        kwn_ref[0, hh] = yn[:, c0:c0 + HEAD_DIM].astype(BF16)


def nsa_in_proj(layer, n_layers, x, nw, w_in, kvt_all, *, tm):
    b, t, d = x.shape
    o1, o2, o3 = ATT_WIDTH, ATT_WIDTH + 4 * KV_WIDTH, ATT_WIDTH + 6 * KV_WIDTH
    o4 = o3 + 3 * N_HEADS
    w_t = w_in.T
    wt = jnp.concatenate([w_t[:o4], jnp.zeros((LANES - 3 * N_HEADS, d), w_in.dtype), w_t[o4:]], axis=0).astype(BF16)
    wn = jnp.concatenate([w_in[:, o1:o1 + 3 * KV_WIDTH], w_in[:, o2:o2 + KV_WIDTH]], axis=1).astype(BF16)
    tok = lambda rows: pl.BlockSpec((1, rows, tm), lambda i, j: (i, 0, j))
    full = lambda a: pl.BlockSpec(a.shape, lambda i, j: (0,) * a.ndim)
    tshape = lambda rows, dt: jax.ShapeDtypeStruct((b, rows, t), dt)
    args = [x, nw.reshape(1, d), wt, wn]
    in_specs = [pl.BlockSpec((1, tm, d), lambda i, j: (i, j, 0)), pl.BlockSpec((1, d), lambda i, j: (0, 0)), full(wt), full(wn)]
    aliases = {}
    if kvt_all is not None:
        args.append(kvt_all)
        in_specs.append(pl.BlockSpec(memory_space=pl.ANY))
        aliases = {len(args) - 1: 1}
    return pl.pallas_call(
        _nsa_in_proj_kernel, grid=(b, t // tm), in_specs=in_specs,
        out_specs=[tok(ATT_WIDTH), pl.BlockSpec((1, 1, 4 * KV_WIDTH, tm), lambda i, j: (layer, i, 0, j)),
                   tok(2 * KV_WIDTH), tok(LANES), tok(ATT_WIDTH),
                   pl.BlockSpec((1, 3, N_KV_HEADS, tm, HEAD_DIM), lambda i, j: (i, 0, 0, j, 0)),
                   pl.BlockSpec((1, N_KV_HEADS, tm, HEAD_DIM), lambda i, j: (i, 0, j, 0))],
        out_shape=[tshape(ATT_WIDTH, BF16), jax.ShapeDtypeStruct((n_layers, b, 4 * KV_WIDTH, t), F32),
                   tshape(2 * KV_WIDTH, F32), tshape(LANES, F32),
                   tshape(ATT_WIDTH, BF16), jax.ShapeDtypeStruct((b, 3, N_KV_HEADS, t, HEAD_DIM), BF16),
                   jax.ShapeDtypeStruct((b, N_KV_HEADS, t, HEAD_DIM), BF16)],
        input_output_aliases=aliases,
        compiler_params=_cparams("parallel", "parallel"), name="nsa_in_proj",
    )(*args)


def _topk_mask_rows(score, k):
    r = score.shape[0]
    groups = [score[8 * v:8 * v + 8] for v in range(r // 8)]
    sub = lax.broadcasted_iota(jnp.int32, (8, 1), 0)
    cnt = [jnp.zeros(g.shape, F32) for g in groups]
    for i in range(r):
        row = score[i:i + 1, :]
        for v, g in enumerate(groups):
            if 8 * v > i:
                beats = row >= g
            elif 8 * v + 7 < i:
                beats = row > g
            else:
                beats = (row > g) | ((sub > i - 8 * v) & (row == g))
            cnt[v] = cnt[v] + jnp.where(beats, 1.0, 0.0)
    return jnp.concatenate(cnt, axis=0) < float(k)


def _nsa_attn_kernel(tab_ref, qt_ref, gt_ref, kc_ref, vc_ref, ksel_ref, kwin_ref, vselt_ref, vwint_ref, ovt_ref, ot_ref,
                     ksa_s, kwa_s, kca_s, vst_s, vwt_s, vct_s, sc_s, m_s, acc_s, *, t_len, n_seg, n_blk):
    kh = pl.program_id(1)
    qi = pl.program_id(2)
    t0 = qi * TQ
    cols = GROUP * TQ
    ones = lambda n: jnp.ones((ONES_ROWS, n), BF16)

    @pl.when(qi == 0)
    def _build_keys():
        pos = lax.broadcasted_iota(jnp.int32, (t_len, 1), 0)
        onehot = jnp.where(lax.shift_right_logical(pos, int(math.log2(SEL_BLK)))
                           == lax.broadcasted_iota(jnp.int32, (1, HEAD_DIM), 1), 1.0, 0.0)
        ksa_s[:, 0:LANES] = jnp.concatenate([ksel_ref[0, 0, 0].astype(F32), onehot], axis=1).astype(BF16)
        ksa_s[:, LANES:2 * LANES] = _pos_features(pos, pos >= 0, LANES).astype(BF16)
        no_key = _pos_features(jnp.zeros((WINDOW, 1), jnp.int32), jnp.zeros((WINDOW, 1), jnp.bool_), HEAD_DIM)
        kwa_s[0:WINDOW, :] = jnp.concatenate([jnp.zeros((WINDOW, HEAD_DIM), F32), no_key], axis=1).astype(BF16)
        kwa_s[WINDOW:WINDOW + t_len, :] = jnp.concatenate(
            [kwin_ref[0, 0].astype(F32), _pos_features(pos, pos >= 0, HEAD_DIM)], axis=1).astype(BF16)
        cend = lax.broadcasted_iota(jnp.int32, (n_seg, 1), 0) * CMP_STRIDE + (CMP_BLK - 1)
        kca_s[...] = jnp.concatenate([kc_ref[0, 0, 0], _pos_features(cend, cend >= 0, HEAD_DIM)], axis=1).astype(BF16)
        vc_pad = jnp.concatenate([vc_ref[0, 0, 0], jnp.zeros((n_seg, LANES - HEAD_DIM), F32)], axis=1)
        vct_s[...] = jnp.concatenate([vc_pad.T[0:HEAD_DIM].astype(BF16), ones(n_seg)], axis=0)
        for c in range(t_len // TK):
            vst_s[c] = jnp.concatenate([vselt_ref[0, 0, :, c * TK:(c + 1) * TK].astype(BF16), ones(TK)], axis=0)
        for c in range(WINDOW // WCH):
            vwt_s[c] = jnp.zeros((HEAD_DIM + ONES_ROWS, WCH), BF16)
        for c in range(t_len // WCH):
            vwt_s[WINDOW // WCH + c] = jnp.concatenate([vwint_ref[0, :, c * WCH:(c + 1) * WCH].astype(BF16), ones(WCH)], axis=0)

    qb = qt_ref[0]
    qw = jnp.concatenate([qb[g * HEAD_DIM:(g + 1) * HEAD_DIM, :] for g in range(GROUP)], axis=1)
    frow = lax.broadcasted_iota(jnp.int32, (ALIBI_ROWS, cols), 0)
    fhead = lax.broadcasted_iota(jnp.int32, (ALIBI_ROWS, cols), 1) // TQ
    feat = jnp.zeros((ALIBI_ROWS, cols), F32)
    for g in range(GROUP):
        for r in range(7):
            feat = jnp.where((frow == r) & (fhead == g), tab_ref[(kh * GROUP + g) * 8 + r], feat)
    feat = feat.astype(BF16)
    q_base = jnp.concatenate([qw, feat, jnp.zeros((LANES - HEAD_DIM - ALIBI_ROWS, cols), BF16)], axis=0)
    t_row = t0 + lax.broadcasted_iota(jnp.int32, (1, cols), 1) % TQ

    def finish(acc):
        return acc[0:HEAD_DIM] * (1.0 / jnp.maximum(acc[HEAD_DIM:HEAD_DIM + 1], 1e-30))

    cend = lax.broadcasted_iota(jnp.int32, (n_seg, 1), 0) * CMP_STRIDE + (CMP_BLK - 1)
    ok_c = cend <= t_row
    s_c = jnp.where(ok_c, jnp.dot(kca_s[...], q_base, preferred_element_type=F32), NEG)
    m_c = jnp.max(s_c, axis=0, keepdims=True)
    p_c = jnp.where(ok_c, jnp.exp2(s_c - m_c), 0.0).astype(BF16)
    acc_c = jnp.dot(vct_s[...], p_c, preferred_element_type=F32)
    inv_c = 1.0 / jnp.maximum(acc_c[HEAD_DIM:HEAD_DIM + 1], 1e-30)
    o_cmp = acc_c[0:HEAD_DIM] * inv_c
    imp_c = jnp.dot(ovt_ref[...], p_c, preferred_element_type=F32) * inv_c
    imp = imp_c[:, 0:TQ]
    for g in range(1, GROUP):
        imp = imp + imp_c[:, g * TQ:(g + 1) * TQ]
    blk = lax.broadcasted_iota(jnp.int32, (HEAD_DIM, 1), 0)
    tq_row = t_row[:, 0:TQ]
    tb = lax.shift_right_logical(tq_row, int(math.log2(SEL_BLK)))
    forced = (blk == 0) | (blk == tb) | (blk == tb - 1)
    score = jnp.where(forced, FORCE_SCORE, jnp.where(blk * SEL_BLK <= tq_row, imp, -1.0))
    score = jnp.where(blk < n_blk, score, -2.0)
    chosen = _topk_mask_rows(score, min(N_SEL, n_blk)) & (score > -0.5)

    span = WINDOW + TQ
    k_w = kwa_s[pl.ds(pl.multiple_of(t0, TQ), span), :]
    v_w = jnp.concatenate([vwt_s[qi * (TQ // WCH) + c] for c in range(span // WCH)], axis=1)
    jj = lax.broadcasted_iota(jnp.int32, (TQ, 1), 0)
    ii = lax.broadcasted_iota(jnp.int32, (1, cols), 1) % TQ
    s_w = jnp.dot(k_w, q_base, preferred_element_type=F32)
    s_w = jnp.concatenate([jnp.where(jj >= ii, s_w[0:TQ], NEG), s_w[TQ:WINDOW],
                           jnp.where(jj <= ii, s_w[WINDOW:span], NEG)], axis=0)
    m_w = jnp.maximum(jnp.max(s_w, axis=0, keepdims=True), M_INIT)
    p_w = jnp.exp2((s_w - m_w).astype(BF16))
    o_win = finish(jnp.dot(v_w, p_w, preferred_element_type=F32))

    mrow = jnp.concatenate([jnp.where(chosen, 0.0, -BIG).astype(BF16)] * GROUP, axis=1)
    q_sel = jnp.concatenate([qw, mrow, feat, jnp.zeros((LANES - ALIBI_ROWS, cols), BF16)], axis=0)
    n_pairs = (t0 + TQ - 1) // (2 * TK) + 1
    max_pairs = t_len // (2 * TK)
    key_off = lax.broadcasted_iota(jnp.int32, (TK, 1), 0)

    def issue(j):
        for u in range(2):
            ks = (2 * j + u) * TK
            s = jnp.dot(ksa_s[ks:ks + TK, :], q_sel, preferred_element_type=F32)
            sc_s[j % 2, u] = jnp.where(ks + key_off <= t_row, s, NEG)

    def absorb(j):
        s0, s1 = sc_s[j % 2, 0], sc_s[j % 2, 1]
        m = m_s[0:1, :]
        m_new = jnp.maximum(m, jnp.maximum(jnp.max(s0, axis=0, keepdims=True), jnp.max(s1, axis=0, keepdims=True)))
        acc = jnp.exp2(m - m_new) * acc_s[...]
        for u, s in enumerate((s0, s1)):
            acc = acc + jnp.dot(vst_s[2 * j + u], jnp.exp2((s - m_new).astype(BF16)), preferred_element_type=F32)
        m_s[0:1, :] = m_new
        acc_s[...] = acc

    m_s[0:1, :] = jnp.full((1, cols), M_INIT, F32)
    acc_s[...] = jnp.zeros(acc_s.shape, F32)
    for n in range(1, max_pairs + 1):
        @pl.when(n_pairs == n)
        def _pairs(n=n):
            issue(0)
            for j in range(n):
                if j + 1 < n:
                    issue(j + 1)
                absorb(j)
    o_sel = finish(acc_s[...])

    for g in range(GROUP):
        sl = slice(g * TQ, (g + 1) * TQ)
        gate = lambda j: gt_ref[0, pl.ds(kh * 3 * GROUP + 3 * g + j, 1), :]
        mix = gate(0) * o_cmp[:, sl] + gate(1) * o_sel[:, sl] + gate(2) * o_win[:, sl]
        ot_ref[0, g * HEAD_DIM:(g + 1) * HEAD_DIM, :] = mix.astype(ot_ref.dtype)


def nsa_attention(layer, qt, gt, kvc, kn, kwn, kvt, wint):
    b, _, t = qt.shape
    n_seg = kvc.shape[3]
    n_blk = t // SEL_BLK
    assert t % (2 * TK) == 0 and n_blk <= HEAD_DIM and t >= WINDOW + TQ
    start = np.arange(n_seg) * CMP_STRIDE
    blk = np.arange(HEAD_DIM) * SEL_BLK
    ovt = jnp.asarray(((start[None, :] < blk[:, None] + SEL_BLK) & (start[None, :] + CMP_BLK > blk[:, None])
                       & (np.arange(HEAD_DIM)[:, None] < n_blk)), dtype=BF16)
    vrows = HEAD_DIM + ONES_ROWS
    grid_spec = pltpu.PrefetchScalarGridSpec(
        num_scalar_prefetch=1, grid=(b, N_KV_HEADS, t // TQ),
        in_specs=[
            pl.BlockSpec((1, GROUP * HEAD_DIM, TQ), lambda i, k, j, s: (i, k, j)),
            pl.BlockSpec((1, LANES, TQ), lambda i, k, j, s: (i, 0, j)),
            pl.BlockSpec((1, 1, 1, n_seg, HEAD_DIM), lambda i, k, j, s: (i, 0, k, 0, 0)),
            pl.BlockSpec((1, 1, 1, n_seg, HEAD_DIM), lambda i, k, j, s: (i, 1, k, 0, 0)),
            pl.BlockSpec((1, 1, 1, t, HEAD_DIM), lambda i, k, j, s: (i, 2, k, 0, 0)),
            pl.BlockSpec((1, 1, t, HEAD_DIM), lambda i, k, j, s: (i, k, 0, 0)),
            pl.BlockSpec((1, 1, HEAD_DIM, t), lambda i, k, j, s: (layer, i, 3 * N_KV_HEADS + k, 0)),
            pl.BlockSpec((1, HEAD_DIM, t), lambda i, k, j, s: (i, N_KV_HEADS + k, 0)),
            pl.BlockSpec((HEAD_DIM, n_seg), lambda i, k, j, s: (0, 0)),
        ],
        out_specs=pl.BlockSpec((1, GROUP * HEAD_DIM, TQ), lambda i, k, j, s: (i, k, j)),
        scratch_shapes=[pltpu.VMEM((t, 2 * LANES), BF16), pltpu.VMEM((WINDOW + t, LANES), BF16), pltpu.VMEM((n_seg, LANES), BF16),
                        pltpu.VMEM((t // TK, vrows, TK), BF16), pltpu.VMEM(((WINDOW + t) // WCH, vrows, WCH), BF16),
                        pltpu.VMEM((vrows, n_seg), BF16),
                        pltpu.VMEM((2, 2, TK, GROUP * TQ), F32), pltpu.VMEM((8, GROUP * TQ), F32),
                        pltpu.VMEM((vrows, GROUP * TQ), F32)])
    return pl.pallas_call(
        functools.partial(_nsa_attn_kernel, t_len=t, n_seg=n_seg, n_blk=n_blk), grid_spec=grid_spec,
        out_shape=jax.ShapeDtypeStruct((b, ATT_WIDTH, t), BF16),
        compiler_params=_cparams("parallel", "parallel", "arbitrary"), name="nsa_attn",
    )(_alibi_table(), qt, gt, kvc, kvc, kn, kwn, kvt, wint, ovt)


def _nsa_out_t_kernel(ot_ref, zt_ref, x_ref, w_ref, y_ref):
    a_t = (ot_ref[0].astype(F32) * _silu(zt_ref[0].astype(F32))).astype(BF16)
    y_ref[0] = x_ref[0] + lax.dot_general(a_t, w_ref[...], TN, preferred_element_type=F32)


def nsa_out_t(ot, zt, x, w_out, *, tm):
    b, t, d = x.shape
    tok = pl.BlockSpec((1, ATT_WIDTH, tm), lambda i, j: (i, 0, j))
    row = pl.BlockSpec((1, tm, d), lambda i, j: (i, j, 0))
    return pl.pallas_call(
        _nsa_out_t_kernel, grid=(b, t // tm),
        in_specs=[tok, tok, row, pl.BlockSpec(w_out.shape, lambda i, j: (0, 0))],
        out_specs=row, out_shape=jax.ShapeDtypeStruct((b, t, d), F32),
        compiler_params=_cparams("parallel", "parallel"), name="nsa_out",
    )(ot, zt, x, w_out)


def nsa_prompt_layer_t(layer, n_layers, x, kvt_all, nw, w_in, w_out, pe, w1, b1, w2):
    b, t, d = x.shape
    qt, kvt_all, wint, gt, zt, kn, kwn = nsa_in_proj(layer, n_layers, x, nw, w_in, kvt_all, tm=256)
    n_seg = t // CMP_STRIDE
    kvc = compress_tokens(kn[:, 0:2].reshape(b, 2, N_KV_HEADS, n_seg, CMP_STRIDE * HEAD_DIM), pe, w1, b1, w2)
    ot = nsa_attention(layer, qt, gt, kvc, kn, kwn, kvt_all, wint)
    y = nsa_out_t(ot, zt, x, w_out.astype(BF16), tm=256)
    wr = min(WINDOW, t)
    win5 = wint[:, :, t - wr:].reshape(b, 2, N_KV_HEADS, HEAD_DIM, wr).transpose(0, 4, 1, 2, 3)
    return y, kvt_all, win5


def _split_nsa_w_in(w_in, dtype):
    o1 = ATT_WIDTH
    o2 = o1 + 4 * KV_WIDTH
    o3 = o2 + 2 * KV_WIDTH
    o4 = o3 + 3 * N_HEADS
    wg = jnp.pad(w_in[:, o3:o4], ((0, 0), (0, LANES - 3 * N_HEADS)))
    return [w.astype(dtype) for w in (w_in[:, :o1], w_in[:, o1:o2], w_in[:, o2:o3], wg, w_in[:, o4:])]


NSA_ACTS = (None, None, None, "sigmoid", None)


def nsa_prompt_layer(x, nw, w_in, w_out, pe, w1, b1, w2):
    b, t, d = x.shape
    n_blk = t // SEL_BLK
    assert n_blk <= HEAD_DIM
    q, kv, win, gates, z = rms_proj(x.reshape(b * t, d), nw, _split_nsa_w_in(w_in, BF16), NSA_ACTS, tm=256, name="nsa_in_proj")
    q = q.reshape(b, t, ATT_WIDTH)
    kv5 = kv.reshape(b, t, 4, N_KV_HEADS, HEAD_DIM)
    win5 = win.reshape(b, t, 2, N_KV_HEADS, HEAD_DIM)
    kvt = kv5.transpose(0, 2, 3, 1, 4).astype(BF16)
    wint = win5.transpose(0, 2, 3, 1, 4).astype(BF16)
    gates_r = gates[:, :3 * N_HEADS].reshape(b, t, N_KV_HEADS, 3 * GROUP).transpose(0, 2, 1, 3)
    slopes = _alibi_slopes()
    n_seg = t // CMP_STRIDE
    kvc = compress_tokens(kvt[:, 0:2].reshape(b, 2, N_KV_HEADS, n_seg, CMP_STRIDE * HEAD_DIM), pe, w1, b1, w2)
    o_c, sel = cmp_select(q, kvc, gates_r, slopes, _cmp_to_sel_matrix(n_seg, n_blk), tq=128)
    onehot = (jnp.arange(t)[:, None] // SEL_BLK == jnp.arange(HEAD_DIM)[None, :]).astype(BF16)
    k_aug = jnp.concatenate([kvt[:, 2], jnp.broadcast_to(onehot, (b, N_KV_HEADS, t, HEAD_DIM))], axis=-1)
    o_s = sel_attention(q, sel, k_aug, kvt[:, 3], gates_r, slopes, tq=128, tk=512)
    o_w = win_attention(q, wint[:, 0], wint[:, 1], gates_r, slopes, tq=128)
    r2 = lambda a: a.reshape(b * t, -1)
    y = nsa_out(r2(o_c), r2(o_s), r2(o_w), z, r2(x), w_out.astype(BF16), tm=256)
    wr = min(WINDOW, t)
    return y.reshape(b, t, d), kv5, win5[:, t - wr:]


def _log_sigmoid(x):
    return jnp.minimum(x, 0.0) - jnp.log(1.0 + jnp.exp(-jnp.abs(x)))


def _mconv_body(shifted, xm, cw_ref, cb_ref, wbd_ref, wg_ref, bg_ref, q_ref, k_ref, v_ref, c_ref, g_ref, exact):
    conv = cb_ref[...]
    for j in range(CONV_W):
        conv = conv + shifted[j] * cw_ref[j:j + 1, :]
    c = _silu(conv)
    c_ref[...] = c.astype(c_ref.dtype)
    cast = (lambda a: a) if exact else (lambda a: a.astype(BF16))
    kw = dict(preferred_element_type=F32, precision=HI) if exact else dict(preferred_element_type=F32)
    gpre = bg_ref[...]
    for m, (src, dst) in enumerate(((c, q_ref), (c, k_ref), (xm, v_ref))):
        parts = []
        for gi in range(D_INNER // QKV_TILE):
            sl = slice(gi * QKV_TILE, (gi + 1) * QKV_TILE)
            y = jnp.dot(cast(src[:, sl]), wbd_ref[m, gi], **kw)
            parts.append(cast(y))
            dst[:, sl] = (y * (M_HEAD_DIM ** -0.5) if m == 1 else y).astype(dst.dtype)
        gpre = gpre + jnp.dot(jnp.concatenate(parts, axis=1), wg_ref[m * D_INNER:(m + 1) * D_INNER, :], **kw)
    lane = lax.broadcasted_iota(jnp.int32, gpre.shape, 1)
    g_ref[...] = jnp.where(lane < M_HEADS, gpre, _log_sigmoid(gpre))


def _mconv_prompt_kernel(xm_ref, halo_ref, cw_ref, cb_ref, wbd_ref, wg_ref, bg_ref, q_ref, k_ref, v_ref, c_ref, g_ref, *, tm):
    xm = xm_ref[0]
    halo = jnp.where(pl.program_id(1) == 0, 0.0, halo_ref[0])
    ext = jnp.concatenate([halo, xm], axis=0)
    shifted = [ext[5 + j:5 + j + tm] for j in range(CONV_W - 1)] + [xm]
    _mconv_body(shifted, xm, cw_ref, cb_ref, wbd_ref, wg_ref, bg_ref, q_ref.at[0], k_ref.at[0], v_ref.at[0], c_ref.at[0],
                g_ref.at[0], False)


QKV_TILE = LANES


def _mlstm_small_weights(w_qkv, w_gate, b_gate, dtype):
    nb = QKV_TILE // QKV_BLK
    w = w_qkv.reshape(3, D_INNER // QKV_TILE, nb, QKV_BLK, QKV_BLK)
    eye = jnp.eye(nb, dtype=w.dtype)
    wbd = jnp.einsum("mgnji,nk->mgnjki", w, eye).reshape(3, D_INNER // QKV_TILE, QKV_TILE, QKV_TILE)
    wg = jnp.pad(w_gate, ((0, 0), (0, LANES - 2 * M_HEADS)))
    bg = jnp.pad(b_gate, (0, LANES - 2 * M_HEADS)).reshape(1, LANES)
    return wbd.astype(dtype), wg.astype(dtype), bg


def mconv_prompt(xm, conv_w, conv_b, w_qkv, w_gate, b_gate, *, tm):
    b, t, _ = xm.shape
    wbd, wg, bg = _mlstm_small_weights(w_qkv, w_gate, b_gate, BF16)
    row = pl.BlockSpec((1, tm, D_INNER), lambda i, j: (i, j, 0))
    full = lambda a: pl.BlockSpec(a.shape, lambda i, j: (0,) * a.ndim)
    cb = conv_b.reshape(1, D_INNER)
    return pl.pallas_call(
        functools.partial(_mconv_prompt_kernel, tm=tm), grid=(b, t // tm),
        in_specs=[row, pl.BlockSpec((1, 8, D_INNER), lambda i, j: (i, jnp.maximum(j * (tm // 8) - 1, 0), 0)),
                  full(conv_w), full(cb), full(wbd), full(wg), full(bg)],
        out_specs=[row, row, row, row, pl.BlockSpec((1, tm, LANES), lambda i, j: (i, j, 0))],
        out_shape=[jax.ShapeDtypeStruct((b, t, D_INNER), BF16)] * 4 + [jax.ShapeDtypeStruct((b, t, LANES), F32)],
        compiler_params=_cparams("parallel", "parallel"), name="mlstm_conv_qkv",
    )(xm, xm, conv_w, cb, wbd, wg, bg)


def _mlstm_cell_kernel(q_ref, k_ref, v_ref, gc_ref, gr_ref, c0_ref, n0_ref, m0_ref, h_ref, cf_ref, nf_ref, mf_ref,
                       c_s, n_s, m_s, *, chunk, n_chunks):
    ci = pl.program_id(2)

    @pl.when(ci == 0)
    def _():
        c_s[...] = c0_ref[0, 0]
        n_s[...] = n0_ref[0, 0]
        m_s[...] = m0_ref[0, 0]

    q, k, v = q_ref[0], k_ref[0], v_ref[0]
    icol, fcol = gc_ref[0, 0, :, 0:1], gc_ref[0, 0, :, 1:2]
    irow, frow = gr_ref[0, 0, 0:1, :], gr_ref[0, 0, 1:2, :]
    ri = lax.broadcasted_iota(jnp.int32, (chunk, chunk), 0)
    cj = lax.broadcasted_iota(jnp.int32, (chunk, chunk), 1)
    causal = cj <= ri
    b_col = jnp.dot(jnp.where(causal, 1.0, 0.0), jnp.broadcast_to(fcol, (chunk, LANES)),
                    preferred_element_type=F32, precision=HI)[:, 0:1]
    b_row = jnp.dot(jnp.broadcast_to(frow, (8, chunk)), jnp.where(ri <= cj, 1.0, 0.0),
                    preferred_element_type=F32, precision=HI)[0:1, :]
    dmat = jnp.where(causal, b_col - b_row + irow, -jnp.inf)
    m_prev = m_s[0:1, 0:1]
    inter = b_col + m_prev
    mt = jnp.maximum(jnp.max(dmat, axis=-1, keepdims=True), inter)
    qb, kb, vb = q.astype(BF16), k.astype(BF16), v.astype(BF16)
    s = lax.dot_general(qb, kb, NT, preferred_element_type=F32) * jnp.exp(dmat - mt)
    decay = jnp.exp(inter - mt)
    num = (jnp.dot(s.astype(BF16), vb, preferred_element_type=F32)
           + decay * jnp.dot(qb, c_s[...].astype(BF16), preferred_element_type=F32))
    den = jnp.sum(s, axis=-1, keepdims=True) + decay * jnp.sum(q * n_s[...], axis=-1, keepdims=True)
    hc = num / jnp.maximum(jnp.abs(den), jnp.exp(-mt))
    h_ref[0] = (hc * lax.rsqrt(jnp.mean(hc * hc, axis=-1, keepdims=True) + RMS_EPS)).astype(h_ref.dtype)
    m_new = mt[chunk - 1:chunk, :]
    b_last = b_col[chunk - 1:chunk, :]
    kw = k * jnp.exp(b_last - b_col + icol - m_new)
    carry = jnp.exp(b_last + m_prev - m_new)
    c_s[...] = carry * c_s[...] + lax.dot_general(kw.astype(BF16), vb, TN, preferred_element_type=F32)
    n_s[...] = carry * n_s[...] + jnp.sum(kw, axis=0, keepdims=True)
    m_s[...] = jnp.broadcast_to(m_new, m_s.shape)

    @pl.when(ci == n_chunks - 1)
    def _():
        cf_ref[0, 0] = c_s[...]
        nf_ref[0, 0] = n_s[...]
        mf_ref[0, 0] = m_s[...]


def mlstm_cell(q, k, v, gates, c0, n0, m0, *, chunk):
    b, t, _ = q.shape
    n_chunks = t // chunk
    g_col = jnp.stack([gates[..., :M_HEADS], gates[..., M_HEADS:2 * M_HEADS]], axis=-1).transpose(0, 2, 1, 3)
    g_row = g_col.transpose(0, 1, 3, 2)
    m0b = jnp.broadcast_to(m0[:, :, None, None], (b, M_HEADS, 8, LANES))
    n0r = n0.reshape(b, M_HEADS, 1, M_HEAD_DIM)
    head = pl.BlockSpec((1, chunk, M_HEAD_DIM), lambda i, h, c: (i, c, h))
    st = lambda *blk: pl.BlockSpec((1, 1) + blk, lambda i, h, c: (i, h, 0, 0))
    hn, cf, nf, mf = pl.pallas_call(
        functools.partial(_mlstm_cell_kernel, chunk=chunk, n_chunks=n_chunks), grid=(b, M_HEADS, n_chunks),
        in_specs=[head, head, head,
                  pl.BlockSpec((1, 1, chunk, 2), lambda i, h, c: (i, h, c, 0)),
                  pl.BlockSpec((1, 1, 2, chunk), lambda i, h, c: (i, h, 0, c)),
                  st(M_HEAD_DIM, M_HEAD_DIM), st(1, M_HEAD_DIM), st(8, LANES)],
        out_specs=[head, st(M_HEAD_DIM, M_HEAD_DIM), st(1, M_HEAD_DIM), st(8, LANES)],
        out_shape=[jax.ShapeDtypeStruct((b, t, D_INNER), BF16), jax.ShapeDtypeStruct((b, M_HEADS, M_HEAD_DIM, M_HEAD_DIM), F32),
                   jax.ShapeDtypeStruct((b, M_HEADS, 1, M_HEAD_DIM), F32), jax.ShapeDtypeStruct((b, M_HEADS, 8, LANES), F32)],
        scratch_shapes=[pltpu.VMEM((M_HEAD_DIM, M_HEAD_DIM), F32), pltpu.VMEM((1, M_HEAD_DIM), F32), pltpu.VMEM((8, LANES), F32)],
        compiler_params=_cparams("parallel", "parallel", "arbitrary"), name="mlstm_cell",
    )(q, k, v, g_col, g_row, c0, n0r, m0b)
    return hn, cf, nf.reshape(b, M_HEADS, M_HEAD_DIM), mf[:, :, 0, 0]


def _mlstm_out_kernel(hn_ref, c_ref, z_ref, x_ref, nw_ref, sk_ref, w_ref, fw_ref, y_ref, *, final, exact):
    a = (hn_ref[...].astype(F32) * nw_ref[...] + sk_ref[...] * c_ref[...].astype(F32)) * _silu(z_ref[...].astype(F32))
    if exact:
        y = x_ref[...] + jnp.dot(a, w_ref[...], preferred_element_type=F32, precision=HI)
    else:
        y = x_ref[...] + jnp.dot(a.astype(BF16), w_ref[...], preferred_element_type=F32)
    y_ref[...] = _rmsnorm(y, fw_ref[...]) if final else y


def mlstm_out(hn, c, z, x, norm_w, skip, w_out, final_w, *, tm, final, exact=False):
    m, d = x.shape
    wide = pl.BlockSpec((tm, D_INNER), lambda i: (i, 0))
    row = pl.BlockSpec((tm, d), lambda i: (i, 0))
    vec = lambda n: pl.BlockSpec((1, n), lambda i: (0, 0))
    return pl.pallas_call(
        functools.partial(_mlstm_out_kernel, final=final, exact=exact), grid=(m // tm,),
        in_specs=[wide, wide, wide, row, vec(D_INNER), vec(D_INNER), pl.BlockSpec(w_out.shape, lambda i: (0, 0)), vec(d)],
        out_specs=row, out_shape=jax.ShapeDtypeStruct((m, d), F32),
        compiler_params=_cparams("parallel"), name="mlstm_out",
    )(hn, c, z, x, norm_w.reshape(1, D_INNER), skip.reshape(1, D_INNER), w_out, final_w.reshape(1, d))


def mlstm_prompt_layer(x, nw, w_in, conv_w, conv_b, w_qkv, w_gate, b_gate, norm_w, skip, w_out, final_w, *, final):
    b, t, d = x.shape
    w_in = w_in.astype(BF16)
    xm, z = rms_proj(x.reshape(b * t, d), nw, [w_in[:, :D_INNER], w_in[:, D_INNER:]], (None, None), tm=256, out_dtypes=(F32, BF16),
                     name="mlstm_in_proj")
    xm3 = xm.reshape(b, t, D_INNER)
    q, k, v, c, gates = mconv_prompt(xm3, conv_w, conv_b, w_qkv, w_gate, b_gate, tm=256)
    c0 = jnp.zeros((b, M_HEADS, M_HEAD_DIM, M_HEAD_DIM), F32)
    n0 = jnp.zeros((b, M_HEADS, M_HEAD_DIM), F32)
    m0 = jnp.full((b, M_HEADS), -jnp.inf, F32)
    hn, cf, nf, mf = mlstm_cell(q, k, v, gates, c0, n0, m0, chunk=min(256, t))
    y = mlstm_out(hn.reshape(b * t, D_INNER), c.reshape(b * t, D_INNER), z, x.reshape(b * t, d), norm_w, skip,
                  w_out.astype(BF16), final_w, tm=256, final=final)
    return y.reshape(b, t, d), cf, nf, mf, xm3[:, t - (CONV_W - 1):]


def _mconv_sample_kernel(xm_ref, hist_ref, cw_ref, cb_ref, wbd_ref, wg_ref, bg_ref, q_ref, k_ref, v_ref, c_ref, g_ref):
    xm = xm_ref[...]
    shifted = [hist_ref[j] for j in range(CONV_W - 1)] + [xm]
    _mconv_body(shifted, xm, cw_ref, cb_ref, wbd_ref, wg_ref, bg_ref, q_ref, k_ref, v_ref, c_ref, g_ref, True)


def mconv_sample(xm, hist, conv_w, conv_b, w_qkv, w_gate, b_gate):
    b = xm.shape[0]
    wbd, wg, bg = _mlstm_small_weights(w_qkv, w_gate, b_gate, F32)
    cb = conv_b.reshape(1, D_INNER)
    full = lambda a: pl.BlockSpec(a.shape, lambda i: (0,) * a.ndim)
    args = (xm, hist, conv_w, cb, wbd, wg, bg)
    row = pl.BlockSpec((b, D_INNER), lambda i: (0, 0))
    return pl.pallas_call(
        _mconv_sample_kernel, grid=(1,), in_specs=[full(a) for a in args],
        out_specs=[row, row, row, row, pl.BlockSpec((b, LANES), lambda i: (0, 0))],
        out_shape=[jax.ShapeDtypeStruct((b, D_INNER), F32)] * 4 + [jax.ShapeDtypeStruct((b, LANES), F32)],
        compiler_params=_cparams("arbitrary"), name="mlstm_conv_qkv_sample",
    )(*args)


def _mlstm_step_kernel(q_ref, k_ref, v_ref, g_ref, c0_ref, n0_ref, *rest):
    h_ref, cf_ref, nf_ref, mf_ref = rest[-4:]
    q, k, v = q_ref[0], k_ref[0], v_ref[0]
    g = g_ref[0, 0]
    ig, fl, m0 = g[:, 0:1], g[:, 1:2], g[:, 2:3]
    c0, n0 = c0_ref[0, 0, 0], n0_ref[0, 0]
    m_new = jnp.maximum(fl + m0, ig)
    decay = jnp.exp(fl + m0 - m_new)
    sw = jnp.exp(ig - m_new)
    s = jnp.sum(q * k, axis=-1, keepdims=True) * sw
    half = LANES // 2
    qk_col = jnp.concatenate([jnp.broadcast_to(q, (half, M_HEAD_DIM)), jnp.broadcast_to(k, (half, M_HEAD_DIM))], axis=0).T
    q_col, k_col = qk_col[:, 0:1], qk_col[:, half:half + 1]
    qc = jnp.sum(q_col * c0, axis=0, keepdims=True)
    num = s * v + decay * qc
    den = s + decay * jnp.sum(q * n0, axis=-1, keepdims=True)
    hc = num / jnp.maximum(jnp.abs(den), jnp.exp(-m_new))
    h_ref[0] = hc * lax.rsqrt(jnp.mean(hc * hc, axis=-1, keepdims=True) + RMS_EPS)
    cf_ref[0, 0, 0] = decay * c0 + (k_col * sw) * v
    nf_ref[0, 0] = decay * n0 + sw * k
    mf_ref[0, 0] = jnp.broadcast_to(m_new, (1, LANES))


def mlstm_step(layer, q, k, v, gates, c_all, n0, m0, c_new_all):
    b = q.shape[0]
    gsm = jnp.stack([gates[:, :M_HEADS], gates[:, M_HEADS:2 * M_HEADS], m0], axis=-1)
    gsm = jnp.pad(gsm, ((0, 0), (0, 0), (0, LANES - 3))).reshape(b, M_HEADS, 1, LANES)
    r3 = lambda a: a.reshape(b, 1, D_INNER)
    head = pl.BlockSpec((1, 1, M_HEAD_DIM), lambda i, h: (i, 0, h))
    st = lambda *blk: pl.BlockSpec((1, 1) + blk, lambda i, h: (i, h, 0, 0))
    c_blk = pl.BlockSpec((1, 1, 1, M_HEAD_DIM, M_HEAD_DIM), lambda i, h: (layer, i, h, 0, 0))
    args = [r3(q), r3(k), r3(v), gsm, c_all, n0.reshape(b, M_HEADS, 1, M_HEAD_DIM)]
    in_specs = [head, head, head, st(1, LANES), c_blk, st(1, M_HEAD_DIM)]
    aliases = {}
    if c_new_all is not None:
        args.append(c_new_all)
        in_specs.append(pl.BlockSpec(memory_space=pl.ANY))
        aliases = {len(args) - 1: 1}
    hn, cf, nf, mf = pl.pallas_call(
        _mlstm_step_kernel, grid=(b, M_HEADS), in_specs=in_specs,
        out_specs=[head, c_blk, st(1, M_HEAD_DIM), st(1, LANES)],
        out_shape=[jax.ShapeDtypeStruct((b, 1, D_INNER), F32), jax.ShapeDtypeStruct(c_all.shape, F32),
                   jax.ShapeDtypeStruct((b, M_HEADS, 1, M_HEAD_DIM), F32), jax.ShapeDtypeStruct((b, M_HEADS, 1, LANES), F32)],
        input_output_aliases=aliases,
        compiler_params=_cparams("parallel", "parallel"), name="mlstm_step",
    )(*args)
    return hn.reshape(b, D_INNER), cf, nf.reshape(b, M_HEADS, M_HEAD_DIM), mf[:, :, 0, 0]


def mlstm_sample_layer(layer, x, conv_state, c_all, c_new_all, n0, m0, nw, w_in, conv_w, conv_b, w_qkv, w_gate, b_gate, norm_w,
                       skip, w_out, final_w, *, final):
    b = x.shape[0]
    xm, z = rms_proj(x, nw, [w_in[:, :D_INNER], w_in[:, D_INNER:]], (None, None), tm=b, exact=True, name="mlstm_in_proj_sample")
    hist = conv_state.transpose(1, 0, 2)
    q, k, v, c, gates = mconv_sample(xm, hist, conv_w, conv_b, w_qkv, w_gate, b_gate)
    hn, cf, nf, mf = mlstm_step(layer, q, k, v, gates, c_all, n0, m0, c_new_all)
    y = mlstm_out(hn, c, z, x, norm_w, skip, w_out, final_w, tm=b, final=final, exact=True)
    conv_new = jnp.concatenate([hist[1:], xm[None]], axis=0).transpose(1, 0, 2)
    return y, cf, nf, mf, conv_new


SAMPLE_ROWS = 8
PAGES_PER_STEP = 16
N_BLK_PAD = 256


def _row_scalars(vals):
    row = lax.broadcasted_iota(jnp.int32, (SAMPLE_ROWS, 1), 0)
    col = jnp.zeros((SAMPLE_ROWS, 1), F32)
    for g, v in enumerate(vals):
        col = jnp.where(row == g, v, col)
    return col


def _nsa_sample_kernel(pt_ref, slopes_ref, *refs, gp, n_groups, past_len):
    cmp_pages, sel_pages = refs[:gp], refs[gp:2 * gp]
    (q_ref, qbd_ref, kvn_ref, cwin_ref, wn_ref, g_ref, wp_ref, pe_ref, w1f_ref, b1_ref, w2_ref, ov_ref,
     ex_ref) = refs[2 * gp:2 * gp + 13]
    o_ref, wout_ref, cmp_s, kc_s, vc_s, bias_s, oc_s, m_s, l_s, acc_s = refs[-10:]
    step = pl.program_id(1)
    t = past_len
    n_seg = past_len // CMP_STRIDE
    n_blk = past_len // SEL_BLK + 1
    span = gp * PAGE_SIZE
    slope_cols = [_row_scalars([slopes_ref[kh * GROUP + g] for g in range(GROUP)]) for kh in range(N_KV_HEADS)]

    @pl.when(step < n_groups)
    def _stash():
        for i in range(gp):
            row0 = pl.multiple_of((step * gp + i) * PAGE_SIZE, PAGE_SIZE)
            for c in range(2):
                for hp in range(N_KV_HEADS // 2):
                    a = cmp_pages[i][0, 0, c, 2 * hp:2 * hp + 2].reshape(2 * HEAD_DIM, PAGE_SIZE)
                    cmp_s[c * 2 + hp, pl.ds(row0, PAGE_SIZE), :] = a.T

    @pl.when(step == n_groups - 1)
    def _compress_and_select():
        for slab in range(4):
            c, hp = divmod(slab, 2)
            x = jnp.concatenate([cmp_s[slab, pl.ds(r, n_seg, stride=CMP_STRIDE), :] for r in range(CMP_STRIDE)], axis=1)
            part = jnp.dot(x.astype(BF16), wp_ref[c], preferred_element_type=F32)
            hidc = jnp.dot(pe_ref[c], w1f_ref[c], preferred_element_type=F32)[0:1] + b1_ref[c]
            for hh in range(2):
                p0 = part[:, hh * 2 * CMP_HID:hh * 2 * CMP_HID + CMP_HID]
                p1 = part[:, hh * 2 * CMP_HID + CMP_HID:(hh + 1) * 2 * CMP_HID]
                hid = hidc + p0 + pltpu.roll(p1, n_seg - 1, 0)
                tok = jnp.dot(_silu(hid).astype(BF16), w2_ref[c], preferred_element_type=F32)
                if c == 0:
                    kc_s[2 * hp + hh] = tok
                else:
                    vc_s[2 * hp + hh] = tok
        cend = lax.broadcasted_iota(jnp.int32, (1, n_seg), 1) * CMP_STRIDE + (CMP_BLK - 1)
        d_c = (t - cend).astype(F32)
        ok = d_c >= 0.0
        row = lax.broadcasted_iota(jnp.int32, (SAMPLE_ROWS, 1), 0)
        blk = lax.broadcasted_iota(jnp.int32, (1, N_BLK_PAD), 1)
        tb = t // SEL_BLK
        forced = (blk == 0) | (blk == tb) | (blk == tb - 1)
        ii = lax.broadcasted_iota(jnp.int32, (N_BLK_PAD, N_BLK_PAD), 0)
        jj = lax.broadcasted_iota(jnp.int32, (N_BLK_PAD, N_BLK_PAD), 1)
        sel_rows = []
        for kh in range(N_KV_HEADS):
            q8 = (q_ref[0, kh] * Q_SCALE).astype(BF16)
            s = lax.dot_general(q8, kc_s[kh].astype(BF16), NT, preferred_element_type=F32) - slope_cols[kh] * d_c
            s = jnp.where(ok, s, NEG)
            m = jnp.max(s, axis=-1, keepdims=True)
            p = jnp.where(ok, jnp.exp(s - m), 0.0)
            pn = (p / jnp.maximum(jnp.sum(p, axis=-1, keepdims=True), 1e-30)).astype(BF16)
            oc_s[kh] = jnp.dot(pn, vc_s[kh].astype(BF16), preferred_element_type=F32)
            pn_heads = jnp.where(row < GROUP, pn, jnp.zeros_like(pn))
            imp = jnp.sum(jnp.dot(pn_heads, ov_ref[...], preferred_element_type=F32), axis=0, keepdims=True)
            score = jnp.where(forced, FORCE_SCORE, jnp.where(blk * SEL_BLK <= t, imp, -1.0))
            score = jnp.where(blk < n_blk, score, -2.0)
            col = jnp.broadcast_to(score, (SAMPLE_ROWS, N_BLK_PAD)).T[:, 0:1]
            beats = (col > score) | ((ii < jj) & (col == score))
            cnt = jnp.sum(jnp.where(beats, 1.0, 0.0), axis=0, keepdims=True)
            sel = jnp.where((cnt < float(min(N_SEL, n_blk))) & (score > -0.5), 1.0, 0.0)
            sel_rows.append(jnp.broadcast_to(sel, (SAMPLE_ROWS, N_BLK_PAD)))
        picked = jnp.dot(jnp.concatenate(sel_rows, axis=0).astype(BF16), ex_ref[...], preferred_element_type=F32)
        key_pos = lax.broadcasted_iota(jnp.int32, (1, past_len), 1)
        bias = jnp.where(picked > 0.5, jnp.concatenate(slope_cols, axis=0) * (key_pos - t).astype(F32), NEG)
        for gi in range(n_groups):
            bias_s[gi] = bias[:, gi * span:(gi + 1) * span]
        m_s[...] = jnp.full(m_s.shape, M_INIT, F32)
        l_s[...] = jnp.zeros(l_s.shape, F32)
        acc_s[...] = jnp.zeros(acc_s.shape, F32)

    @pl.when(step >= n_groups)
    def _selected():
        gb = step - n_groups
        kk = jnp.concatenate([sel_pages[i][0, 0, 0].reshape(KV_WIDTH, PAGE_SIZE) for i in range(gp)], axis=1).astype(BF16)
        vv = jnp.concatenate([sel_pages[i][0, 0, 1].reshape(KV_WIDTH, PAGE_SIZE) for i in range(gp)], axis=1).astype(BF16)
        s = jnp.dot(qbd_ref[0], kk, preferred_element_type=F32) + bias_s[gb]
        m_old = m_s[:, 0:1]
        m_new = jnp.maximum(m_old, jnp.max(s, axis=-1, keepdims=True))
        alpha = jnp.exp(m_old - m_new)
        p = jnp.exp(s - m_new)
        l_s[...] = jnp.broadcast_to(alpha * l_s[:, 0:1] + jnp.sum(p, axis=-1, keepdims=True), l_s.shape)
        acc_s[...] = alpha * acc_s[...] + lax.dot_general(p.astype(BF16), vv, NT, preferred_element_type=F32)
        m_s[...] = jnp.broadcast_to(m_new, m_s.shape)

    @pl.when(step == 2 * n_groups - 1)
    def _finish():
        wr = cwin_ref.shape[-1]
        d_w = wr - lax.broadcasted_iota(jnp.int32, (1, wr), 1)
        ok_w = (d_w <= WINDOW) & (t - d_w >= 0)
        for kh in range(N_KV_HEADS):
            q8 = q_ref[0, kh] * Q_SCALE
            rows = slice(kh * SAMPLE_ROWS, (kh + 1) * SAMPLE_ROWS)
            s_n = jnp.sum(q8 * kvn_ref[0, 2, kh], axis=-1, keepdims=True)
            m_old = m_s[rows, 0:1]
            m_new = jnp.maximum(m_old, s_n)
            alpha = jnp.exp(m_old - m_new)
            p_n = jnp.exp(s_n - m_new)
            l = alpha * l_s[rows, 0:1] + p_n
            acc = acc_s[rows, kh * HEAD_DIM:(kh + 1) * HEAD_DIM]
            o_sel = (alpha * acc + p_n * kvn_ref[0, 3, kh]) / jnp.maximum(l, 1e-30)
            s_w = jnp.dot(q8.astype(BF16), cwin_ref[0, 0, 0, kh].astype(BF16), preferred_element_type=F32)
            s_w = jnp.where(ok_w, s_w - slope_cols[kh] * d_w.astype(F32), NEG)
            s_wn = jnp.sum(q8 * wn_ref[0, 0, kh], axis=-1, keepdims=True)
            m_w = jnp.maximum(jnp.max(s_w, axis=-1, keepdims=True), s_wn)
            p_w = jnp.where(ok_w, jnp.exp(s_w - m_w), 0.0)
            p_wn = jnp.exp(s_wn - m_w)
            l_w = jnp.sum(p_w, axis=-1, keepdims=True) + p_wn
            o_win = (lax.dot_general(p_w.astype(BF16), cwin_ref[0, 0, 1, kh].astype(BF16), NT, preferred_element_type=F32)
                     + p_wn * wn_ref[0, 1, kh]) / jnp.maximum(l_w, 1e-30)
            gts = g_ref[0, kh]
            o_ref[0, kh] = gts[:, 0:1] * oc_s[kh] + gts[:, 1:2] * o_sel + gts[:, 2:3] * o_win
        last = lax.broadcasted_iota(jnp.int32, (HEAD_DIM, wr), 1) == wr - 1
        for c in range(2):
            for kh in range(N_KV_HEADS):
                new8 = jnp.broadcast_to(wn_ref[0, c, kh], (SAMPLE_ROWS, HEAD_DIM))
                new_col = jnp.concatenate([new8, jnp.zeros_like(new8)], axis=1).T[0:HEAD_DIM, 0:1]
                wout_ref[0, 0, c, kh] = jnp.where(last, new_col, pltpu.roll(cwin_ref[0, 0, c, kh], wr - 1, 1))


def _pair_w1(w1):
    wr = w1.reshape(2, CMP_R, CMP_STRIDE, HEAD_DIM, CMP_HID).transpose(0, 2, 3, 1, 4)
    wp = jnp.einsum("crdje,hk->crhdkje", wr, jnp.eye(2, dtype=w1.dtype))
    return wp.reshape(2, CMP_STRIDE * 2 * HEAD_DIM, 2 * CMP_R * CMP_HID).astype(BF16)


def nsa_sample_attention(layer, q, kv, win, gates, cache_kv, cache_win, win_all, page_table, pe, w1, b1, w2):
    b = q.shape[0]
    n_pages = page_table.shape[1]
    past_len = n_pages * PAGE_SIZE
    gp = PAGES_PER_STEP
    assert n_pages % gp == 0 and cache_win.shape[2] == WINDOW and past_len // SEL_BLK + 1 <= N_BLK_PAD
    n_groups = n_pages // gp
    n_seg = past_len // CMP_STRIDE
    pad_rows = lambda a: jnp.pad(a, ((0, 0), (0, 0), (0, SAMPLE_ROWS - GROUP), (0, 0)))
    q4 = pad_rows(q.reshape(b, N_KV_HEADS, GROUP, HEAD_DIM))
    g4 = pad_rows(gates[:, :3 * N_HEADS].reshape(b, N_KV_HEADS, GROUP, 3))
    q_bd = jnp.einsum("bkgd,kj->bkgjd", q4 * Q_SCALE, jnp.eye(N_KV_HEADS, dtype=F32))
    q_bd = q_bd.reshape(b, N_KV_HEADS * SAMPLE_ROWS, KV_WIDTH).astype(BF16)
    kvn = kv.reshape(b, 4, N_KV_HEADS, 1, HEAD_DIM)
    wn = win.reshape(b, 2, N_KV_HEADS, 1, HEAD_DIM)
    cache_t = cache_kv.transpose(0, 1, 3, 4, 5, 2)
    cwin_t = cache_win.transpose(0, 1, 3, 4, 5, 2)
    start = np.arange(n_seg) * CMP_STRIDE
    blk = np.arange(N_BLK_PAD) * SEL_BLK
    ov = jnp.asarray((start[:, None] < blk[None, :] + SEL_BLK) & (start[:, None] + CMP_BLK > blk[None, :]), dtype=BF16)
    pe8 = jnp.broadcast_to(pe.reshape(2, 1, CMP_BLK * HEAD_DIM), (2, 8, CMP_BLK * HEAD_DIM)).astype(BF16)
    w1f = w1.reshape(2, CMP_BLK * HEAD_DIM, CMP_HID).astype(BF16)
    expand = (jnp.arange(N_BLK_PAD)[:, None] == jnp.arange(past_len)[None, :] // SEL_BLK).astype(BF16)
    consts = (_pair_w1(w1), pe8, w1f, b1.reshape(2, 1, CMP_HID), w2.astype(BF16), ov, expand)
    page_blk = (1, 1, 2, N_KV_HEADS, HEAD_DIM, PAGE_SIZE)

    def cmp_map(i):
        return lambda bi, s, pt, sl: (layer, pt[bi * n_pages + jnp.minimum(s, n_groups - 1) * gp + i], 0, 0, 0, 0)

    def sel_map(i):
        return lambda bi, s, pt, sl: (layer, pt[bi * n_pages + jnp.maximum(s - n_groups, 0) * gp + i], 1, 0, 0, 0)

    per_b = lambda a: pl.BlockSpec((1,) + a.shape[1:], lambda bi, s, pt, sl: (bi,) + (0,) * (a.ndim - 1))
    const = lambda a: pl.BlockSpec(a.shape, lambda bi, s, pt, sl: (0,) * a.ndim)
    in_specs = [pl.BlockSpec(page_blk, cmp_map(i)) for i in range(gp)] + [pl.BlockSpec(page_blk, sel_map(i)) for i in range(gp)]
    in_specs += [per_b(q4), per_b(q_bd), per_b(kvn),
                 pl.BlockSpec((1, 1) + cwin_t.shape[2:], lambda bi, s, pt, sl: (layer, bi, 0, 0, 0, 0)),
                 per_b(wn), per_b(g4)] + [const(a) for a in consts]
    all_rows = N_KV_HEADS * SAMPLE_ROWS
    small = pltpu.VMEM((all_rows, LANES), F32)
    win_blk = pl.BlockSpec((1, 1) + cwin_t.shape[2:], lambda bi, s, pt, sl: (layer, bi, 0, 0, 0, 0))
    args = [page_table.reshape(-1), _alibi_slopes(), *([cache_t] * (2 * gp)), q4, q_bd, kvn, cwin_t, wn, g4, *consts]
    aliases = {}
    if win_all is not None:
        args.append(win_all)
        in_specs.append(pl.BlockSpec(memory_space=pl.ANY))
        aliases = {len(args) - 1: 1}
    grid_spec = pltpu.PrefetchScalarGridSpec(
        num_scalar_prefetch=2, grid=(b, 2 * n_groups), in_specs=in_specs, out_specs=[per_b(q4), win_blk],
        scratch_shapes=[pltpu.VMEM((4, past_len, LANES), F32),
                        pltpu.VMEM((N_KV_HEADS, n_seg, HEAD_DIM), F32), pltpu.VMEM((N_KV_HEADS, n_seg, HEAD_DIM), F32),
                        pltpu.VMEM((n_groups, all_rows, gp * PAGE_SIZE), F32),
                        pltpu.VMEM((N_KV_HEADS, SAMPLE_ROWS, HEAD_DIM), F32), small, small,
                        pltpu.VMEM((all_rows, KV_WIDTH), F32)])
    o, win_all = pl.pallas_call(
        functools.partial(_nsa_sample_kernel, gp=gp, n_groups=n_groups, past_len=past_len), grid_spec=grid_spec,
        out_shape=[jax.ShapeDtypeStruct(q4.shape, F32), jax.ShapeDtypeStruct(cwin_t.shape, F32)],
        input_output_aliases=aliases,
        compiler_params=_cparams("parallel", "arbitrary"), name="nsa_sample_attn",
    )(*args)
    return o[:, :, :GROUP].reshape(b, ATT_WIDTH), win_all


def nsa_sample_layer(layer, x, cache_kv, cache_win, win_all, page_table, nw, w_in, w_out, pe, w1, b1, w2):
    b = x.shape[0]
    q, kv, win, gates, z = rms_proj(x, nw, _split_nsa_w_in(w_in, F32), NSA_ACTS, tm=b, exact=True, name="nsa_in_proj_sample")
    o, win_all = nsa_sample_attention(layer, q, kv, win, gates, cache_kv, cache_win, win_all, page_table, pe, w1, b1, w2)
    zero = jnp.zeros_like(o)
    y = nsa_out(o, zero, zero, z, x, w_out, tm=b, exact=True)
    return y, kv.reshape(b, 1, 4, N_KV_HEADS, HEAD_DIM), win_all


def kernel(x_prompt, x_sample, cache_kv, cache_win, state_C, state_n, state_m, state_conv, page_table, norm_w, final_norm_w,
           nsa_w_in, nsa_w_out, nsa_cmp_pe, nsa_cmp_w1, nsa_cmp_b1, nsa_cmp_w2, m_w_in, m_conv_w, m_conv_b, m_w_qkv, m_w_gate,
           m_b_gate, m_norm_w, m_skip, m_w_out):
    assert DEPTH % 2 == 0 and x_sample.shape[1] == 1
    yp, ys = x_prompt, x_sample[:, 0]
    outs = {name: [] for name in ("kv_s", "win_p", "C_p", "n_p", "n_s", "m_p", "m_s", "cv_p", "cv_s")}
    c_sample = None
    kvt_prompt = None
    win_sample_t = None
    n_nsa = (DEPTH + 1) // 2
    for i in range(DEPTH):
        l = i // 2
        if i % 2 == 0:
            prm = (nsa_w_in[l], nsa_w_out[l], nsa_cmp_pe[l], nsa_cmp_w1[l], nsa_cmp_b1[l], nsa_cmp_w2[l])
            yp, kvt_prompt, wp = nsa_prompt_layer_t(l, n_nsa, yp, kvt_prompt, norm_w[i], *prm)
            ys, kvs, win_sample_t = nsa_sample_layer(l, ys, cache_kv, cache_win, win_sample_t, page_table, norm_w[i], *prm)
            for name, val in (("kv_s", kvs), ("win_p", wp)):
                outs[name].append(val)
        else:
            prm = (m_w_in[l], m_conv_w[l], m_conv_b[l], m_w_qkv[l], m_w_gate[l], m_b_gate[l], m_norm_w[l], m_skip[l], m_w_out[l])
            final = i == DEPTH - 1
            yp, cp, np_, mp, cvp = mlstm_prompt_layer(yp, norm_w[i], *prm, final_norm_w, final=final)
            ys, c_sample, ns, ms, cvs = mlstm_sample_layer(l, ys, state_conv[l], state_C, c_sample, state_n[l], state_m[l],
                                                           norm_w[i], *prm, final_norm_w, final=final)
            for name, val in (("C_p", cp), ("n_p", np_), ("n_s", ns), ("m_p", mp), ("m_s", ms), ("cv_p", cvp), ("cv_s", cvs)):
                outs[name].append(val)
    st = {name: jnp.stack(vals) for name, vals in outs.items()}
    b, t = x_prompt.shape[:2]
    kv_prompt = kvt_prompt.reshape(n_nsa, b, 4, N_KV_HEADS, HEAD_DIM, t).transpose(0, 1, 5, 2, 3, 4)
    win_sample = win_sample_t.transpose(0, 1, 5, 2, 3, 4)
    return (yp, ys[:, None], kv_prompt, st["kv_s"], st["win_p"], win_sample, st["C_p"], c_sample, st["n_p"], st["n_s"],
            st["m_p"], st["m_s"], st["cv_p"], st["cv_s"])
```

```python
import functools
import math

import jax
import jax.numpy as jnp
import numpy as np
from jax import lax
from jax.experimental import pallas as pl
from jax.experimental.pallas import tpu as pltpu

F32 = jnp.float32
BF16 = jnp.bfloat16
HI = lax.Precision.HIGHEST

D_MODEL = 1024
DEPTH = 4
N_HEADS = 16
HEAD_DIM = 64
N_KV_HEADS = 4
GROUP = N_HEADS // N_KV_HEADS
ATT_WIDTH = N_HEADS * HEAD_DIM
KV_WIDTH = N_KV_HEADS * HEAD_DIM
CMP_BLK = 32
CMP_STRIDE = 16
CMP_R = CMP_BLK // CMP_STRIDE
CMP_HID = 2 * HEAD_DIM
SEL_BLK = 64
N_SEL = 16
WINDOW = 512
FORCE_SCORE = 1.0e4
D_INNER = 2 * D_MODEL
M_HEADS = 4
M_HEAD_DIM = D_INNER // M_HEADS
CONV_W = 4
QKV_BLK = 4
RMS_EPS = 1e-6
PAGE_SIZE = 128

LANES = 128
VMEM_LIMIT = 56 * 1024 * 1024
NEG = -1.0e30
M_INIT = -5.0e29
Q_SCALE = HEAD_DIM ** -0.5
NT = (((1,), (1,)), ((), ()))
TN = (((0,), (0,)), ((), ()))


def _cparams(*sem):
    return pltpu.CompilerParams(dimension_semantics=sem, vmem_limit_bytes=VMEM_LIMIT)


def _silu(x):
    return x * jax.nn.sigmoid(x)


def _alibi_slopes():
    return jnp.asarray(np.exp2(-8.0 * np.arange(1, N_HEADS + 1) / N_HEADS), dtype=F32)


def _rmsnorm(x, w):
    return x * lax.rsqrt(jnp.mean(x * x, axis=-1, keepdims=True) + RMS_EPS) * w


def _rms_proj_kernel(x_ref, nw_ref, *refs, acts, n_chunk, exact):
    n = len(acts)
    w_refs, o_refs = refs[:n], refs[n:]
    h = _rmsnorm(x_ref[...], nw_ref[...])
    if not exact:
        h = h.astype(BF16)
    for w_ref, o_ref, act in zip(w_refs, o_refs, acts):
        width = w_ref.shape[1]
        for n0 in range(0, width, n_chunk):
            n1 = min(width, n0 + n_chunk)
            if exact:
                y = jnp.dot(h, w_ref[:, n0:n1], preferred_element_type=F32, precision=HI)
            else:
                y = jnp.dot(h, w_ref[:, n0:n1], preferred_element_type=F32)
            if act == "sigmoid":
                y = jax.nn.sigmoid(y)
            o_ref[:, n0:n1] = y.astype(o_ref.dtype)


def rms_proj(x, nw, weights, acts, *, tm, exact=False, out_dtypes=None, name="rms_proj"):
    m, k = x.shape
    assert m % tm == 0
    out_dtypes = out_dtypes or (F32,) * len(weights)
    in_specs = [pl.BlockSpec((tm, k), lambda i: (i, 0)), pl.BlockSpec((1, k), lambda i: (0, 0))]
    in_specs += [pl.BlockSpec(w.shape, lambda i: (0, 0)) for w in weights]
    out_specs = [pl.BlockSpec((tm, w.shape[1]), lambda i: (i, 0)) for w in weights]
    out_shape = [jax.ShapeDtypeStruct((m, w.shape[1]), dt) for w, dt in zip(weights, out_dtypes)]
    return pl.pallas_call(
        functools.partial(_rms_proj_kernel, acts=tuple(acts), n_chunk=512, exact=exact),
        grid=(m // tm,), in_specs=in_specs, out_specs=out_specs, out_shape=out_shape,
        compiler_params=_cparams("parallel"), name=name,
    )(x, nw.reshape(1, k), *weights)


def _compress_kernel(x_ref, w1_ref, pe_ref, w1f_ref, b1_ref, w2_ref, o_ref, *, n_seg):
    x = x_ref[0, 0, 0]
    p0 = jnp.dot(x, w1_ref[0, 0], preferred_element_type=F32)
    p1 = jnp.dot(x, w1_ref[0, 1], preferred_element_type=F32)
    hidc = jnp.dot(pe_ref[0], w1f_ref[0], preferred_element_type=F32)[0:1] + b1_ref[0]
    hid = hidc + p0 + pltpu.roll(p1, n_seg - 1, 0)
    o_ref[0, 0, 0] = jnp.dot(_silu(hid).astype(BF16), w2_ref[0], preferred_element_type=F32)


def compress_tokens(x, pe, w1, b1, w2):
    b, _, hkv, n_seg, _ = x.shape
    w1r = w1.reshape(2, CMP_R, CMP_STRIDE * HEAD_DIM, CMP_HID).astype(BF16)
    w1f = w1.reshape(2, CMP_BLK * HEAD_DIM, CMP_HID).astype(BF16)
    pe8 = jnp.broadcast_to(pe.reshape(2, 1, CMP_BLK * HEAD_DIM), (2, 8, CMP_BLK * HEAD_DIM)).astype(BF16)
    return pl.pallas_call(
        functools.partial(_compress_kernel, n_seg=n_seg),
        grid=(b, 2, hkv),
        in_specs=[
            pl.BlockSpec((1, 1, 1, n_seg, CMP_STRIDE * HEAD_DIM), lambda i, c, h: (i, c, h, 0, 0)),
            pl.BlockSpec((1, CMP_R, CMP_STRIDE * HEAD_DIM, CMP_HID), lambda i, c, h: (c, 0, 0, 0)),
            pl.BlockSpec((1, 8, CMP_BLK * HEAD_DIM), lambda i, c, h: (c, 0, 0)),
            pl.BlockSpec((1, CMP_BLK * HEAD_DIM, CMP_HID), lambda i, c, h: (c, 0, 0)),
            pl.BlockSpec((1, 1, CMP_HID), lambda i, c, h: (c, 0, 0)),
            pl.BlockSpec((1, CMP_HID, HEAD_DIM), lambda i, c, h: (c, 0, 0)),
        ],
        out_specs=pl.BlockSpec((1, 1, 1, n_seg, HEAD_DIM), lambda i, c, h: (i, c, h, 0, 0)),
        out_shape=jax.ShapeDtypeStruct((b, 2, hkv, n_seg, HEAD_DIM), F32),
        compiler_params=_cparams("parallel", "parallel", "parallel"), name="nsa_compress",
    )(x, w1r, pe8, w1f, b1.reshape(2, 1, CMP_HID), w2.astype(BF16))


def _topk_mask_t(score_t, n_valid, k):
    r = score_t.shape[0]
    jidx = lax.broadcasted_iota(jnp.int32, (r, 1), 0)
    cnt = jnp.zeros(score_t.shape, F32)
    for i in range(n_valid):
        row = score_t[i:i + 1, :]
        beats = (row > score_t) | ((jidx > i) & (row == score_t))
        cnt = cnt + jnp.where(beats, 1.0, 0.0)
    return cnt < float(k)


def _cmp_sel_kernel(slopes_ref, q_ref, kc_ref, vc_ref, g_ref, ov_ref, oc_ref, sel_ref, *, tq, n_seg, n_blk):
    kh = pl.program_id(1)
    t0 = pl.program_id(2) * tq
    t_col = t0 + lax.broadcasted_iota(jnp.int32, (tq, 1), 0)
    cend = lax.broadcasted_iota(jnp.int32, (1, n_seg), 1) * CMP_STRIDE + (CMP_BLK - 1)
    d_c = (t_col - cend).astype(F32)
    ok = d_c >= 0.0
    kc = kc_ref[0, 0, 0].astype(BF16)
    vc = vc_ref[0, 0, 0].astype(BF16)
    imp = jnp.zeros((tq, LANES), F32)
    outs = []
    for g in range(GROUP):
        qg = (q_ref[0, :, g * HEAD_DIM:(g + 1) * HEAD_DIM] * Q_SCALE).astype(BF16)
        s = lax.dot_general(qg, kc, NT, preferred_element_type=F32) - slopes_ref[kh * GROUP + g] * d_c
        s = jnp.where(ok, s, NEG)
        m = jnp.max(s, axis=-1, keepdims=True)
        p = jnp.where(ok, jnp.exp(s - m), 0.0)
        pn = (p / jnp.maximum(jnp.sum(p, axis=-1, keepdims=True), 1e-30)).astype(BF16)
        o = jnp.dot(pn, vc, preferred_element_type=F32)
        outs.append(o * g_ref[0, 0, :, 3 * g:3 * g + 1])
        imp = imp + jnp.dot(pn, ov_ref[...], preferred_element_type=F32)
    oc_ref[0] = jnp.concatenate(outs, axis=1)
    blk = lax.broadcasted_iota(jnp.int32, (1, LANES), 1)
    tb = lax.shift_right_logical(t_col, int(math.log2(SEL_BLK)))
    forced = (blk == 0) | (blk == tb) | (blk == tb - 1)
    score = jnp.where(forced, FORCE_SCORE, jnp.where(blk * SEL_BLK <= t_col, imp, -1.0))
    score = jnp.where(blk < n_blk, score, -2.0)
    score_t = score.T
    sel_t = _topk_mask_t(score_t, n_blk, min(N_SEL, n_blk)) & (score_t > -0.5)
    sel_ref[0, 0] = jnp.where(sel_t, 1.0, 0.0).T


def cmp_select(q, kvc, gates_r, slopes, ov, *, tq):
    b, t, _ = q.shape
    n_seg = kvc.shape[3]
    n_blk = t // SEL_BLK
    grid_spec = pltpu.PrefetchScalarGridSpec(
        num_scalar_prefetch=1, grid=(b, N_KV_HEADS, t // tq),
        in_specs=[
            pl.BlockSpec((1, tq, GROUP * HEAD_DIM), lambda i, k, j, s: (i, j, k)),
            pl.BlockSpec((1, 1, 1, n_seg, HEAD_DIM), lambda i, k, j, s: (i, 0, k, 0, 0)),
            pl.BlockSpec((1, 1, 1, n_seg, HEAD_DIM), lambda i, k, j, s: (i, 1, k, 0, 0)),
            pl.BlockSpec((1, 1, tq, 3 * GROUP), lambda i, k, j, s: (i, k, j, 0)),
            pl.BlockSpec((n_seg, LANES), lambda i, k, j, s: (0, 0)),
        ],
        out_specs=[
            pl.BlockSpec((1, tq, GROUP * HEAD_DIM), lambda i, k, j, s: (i, j, k)),
            pl.BlockSpec((1, 1, tq, LANES), lambda i, k, j, s: (i, k, j, 0)),
        ],
    )
    return pl.pallas_call(
        functools.partial(_cmp_sel_kernel, tq=tq, n_seg=n_seg, n_blk=n_blk),
        grid_spec=grid_spec,
        out_shape=[jax.ShapeDtypeStruct((b, t, ATT_WIDTH), F32), jax.ShapeDtypeStruct((b, N_KV_HEADS, t, LANES), F32)],
        compiler_params=_cparams("parallel", "parallel", "parallel"), name="nsa_cmp_select",
    )(slopes, q, kvc, kvc, gates_r, ov)


def _cmp_to_sel_matrix(n_rows, n_blk):
    start = np.arange(n_rows) * CMP_STRIDE
    blk = np.arange(LANES) * SEL_BLK
    ov = (start[:, None] < blk[None, :] + SEL_BLK) & (start[:, None] + CMP_BLK > blk[None, :]) & (np.arange(LANES)[None, :] < n_blk)
    return jnp.asarray(ov, dtype=BF16)


def _stack_heads(q_blk, extra=None):
    parts = []
    for g in range(GROUP):
        qg = q_blk[:, g * HEAD_DIM:(g + 1) * HEAD_DIM] * Q_SCALE
        if extra is not None:
            qg = jnp.concatenate([qg, extra], axis=1)
        parts.append(qg)
    return jnp.concatenate(parts, axis=0).astype(BF16)


def _head_cols(fn, tq):
    return jnp.concatenate([jnp.full((tq, 1), fn(g), F32) for g in range(GROUP)], axis=0)


def _sel_attn_kernel(slopes_ref, q_ref, sel_ref, ka_ref, v_ref, g_ref, o_ref, *, tq, tk):
    kh = pl.program_id(1)
    t0 = pl.program_id(2) * tq
    mask_feat = (sel_ref[0, 0][:, 0:HEAD_DIM] - 1.0) * 1.0e30
    qa = _stack_heads(q_ref[0], mask_feat)
    slope_col = _head_cols(lambda g: slopes_ref[kh * GROUP + g], tq)
    t_col = t0 + lax.broadcasted_iota(jnp.int32, (tq, 1), 0)
    t_col4 = jnp.concatenate([t_col] * GROUP, axis=0)

    def body(kt, carry):
        m, l, acc = carry
        ks = pl.multiple_of(kt * tk, tk)
        s = lax.dot_general(qa, ka_ref[0, 0, pl.ds(ks, tk), :], NT, preferred_element_type=F32)
        pos = ks + lax.broadcasted_iota(jnp.int32, (1, tk), 1)
        s = s + slope_col * (pos - t0).astype(F32)
        s = jnp.where(pos <= t_col4, s, NEG)
        m_new = jnp.maximum(m, jnp.max(s, axis=-1, keepdims=True))
        alpha = jnp.exp(m - m_new)
        p = jnp.exp(s - m_new)
        l = alpha * l + jnp.sum(p, axis=-1, keepdims=True)
        acc = alpha * acc + jnp.dot(p.astype(BF16), v_ref[0, 0, pl.ds(ks, tk), :], preferred_element_type=F32)
        return m_new, l, acc

    n_kt = (t0 + tq + tk - 1) // tk
    init = (jnp.full((GROUP * tq, 1), M_INIT, F32), jnp.zeros((GROUP * tq, 1), F32), jnp.zeros((GROUP * tq, HEAD_DIM), F32))
    _, l, acc = lax.fori_loop(0, n_kt, body, init)
    o = acc / jnp.maximum(l, 1e-30)
    o_ref[0] = jnp.concatenate(
        [o[g * tq:(g + 1) * tq] * g_ref[0, 0, :, 3 * g + 1:3 * g + 2] for g in range(GROUP)], axis=1)


def sel_attention(q, sel, k_aug, v_sel, gates_r, slopes, *, tq, tk):
    b, t, _ = q.shape
    grid_spec = pltpu.PrefetchScalarGridSpec(
        num_scalar_prefetch=1, grid=(b, N_KV_HEADS, t // tq),
        in_specs=[
            pl.BlockSpec((1, tq, GROUP * HEAD_DIM), lambda i, k, j, s: (i, j, k)),
            pl.BlockSpec((1, 1, tq, LANES), lambda i, k, j, s: (i, k, j, 0)),
            pl.BlockSpec((1, 1, t, LANES), lambda i, k, j, s: (i, k, 0, 0)),
            pl.BlockSpec((1, 1, t, HEAD_DIM), lambda i, k, j, s: (i, k, 0, 0)),
            pl.BlockSpec((1, 1, tq, 3 * GROUP), lambda i, k, j, s: (i, k, j, 0)),
        ],
        out_specs=pl.BlockSpec((1, tq, GROUP * HEAD_DIM), lambda i, k, j, s: (i, j, k)),
    )
    return pl.pallas_call(
        functools.partial(_sel_attn_kernel, tq=tq, tk=tk), grid_spec=grid_spec,
        out_shape=jax.ShapeDtypeStruct((b, t, ATT_WIDTH), F32),
        compiler_params=_cparams("parallel", "parallel", "arbitrary"), name="nsa_sel_attn",
    )(slopes, q, sel, k_aug, v_sel, gates_r)


def _win_attn_kernel(slopes_ref, q_ref, k_ref, v_ref, g_ref, o_ref, *, tq, span):
    kh = pl.program_id(1)
    t0 = pl.program_id(2) * tq
    qs = _stack_heads(q_ref[0])
    slope_col = _head_cols(lambda g: slopes_ref[kh * GROUP + g], tq)
    t_col = t0 + lax.broadcasted_iota(jnp.int32, (tq, 1), 0)
    t_col4 = jnp.concatenate([t_col] * GROUP, axis=0)
    start = pl.multiple_of(jnp.maximum(t0 + tq - span, 0), tq)
    s = lax.dot_general(qs, k_ref[0, 0, pl.ds(start, span), :], NT, preferred_element_type=F32)
    d = t_col4 - (start + lax.broadcasted_iota(jnp.int32, (1, span), 1))
    ok = (d >= 0) & (d <= WINDOW)
    s = jnp.where(ok, s - slope_col * d.astype(F32), NEG)
    m = jnp.max(s, axis=-1, keepdims=True)
    p = jnp.where(ok, jnp.exp(s - m), 0.0)
    l = jnp.sum(p, axis=-1, keepdims=True)
    o = jnp.dot(p.astype(BF16), v_ref[0, 0, pl.ds(start, span), :], preferred_element_type=F32) / jnp.maximum(l, 1e-30)
    o_ref[0] = jnp.concatenate(
        [o[g * tq:(g + 1) * tq] * g_ref[0, 0, :, 3 * g + 2:3 * g + 3] for g in range(GROUP)], axis=1)


def win_attention(q, k_win, v_win, gates_r, slopes, *, tq):
    b, t, _ = q.shape
    span = WINDOW + tq
    assert t >= span and span % tq == 0
    grid_spec = pltpu.PrefetchScalarGridSpec(
        num_scalar_prefetch=1, grid=(b, N_KV_HEADS, t // tq),
        in_specs=[
            pl.BlockSpec((1, tq, GROUP * HEAD_DIM), lambda i, k, j, s: (i, j, k)),
            pl.BlockSpec((1, 1, t, HEAD_DIM), lambda i, k, j, s: (i, k, 0, 0)),
            pl.BlockSpec((1, 1, t, HEAD_DIM), lambda i, k, j, s: (i, k, 0, 0)),
            pl.BlockSpec((1, 1, tq, 3 * GROUP), lambda i, k, j, s: (i, k, j, 0)),
        ],
        out_specs=pl.BlockSpec((1, tq, GROUP * HEAD_DIM), lambda i, k, j, s: (i, j, k)),
    )
    return pl.pallas_call(
        functools.partial(_win_attn_kernel, tq=tq, span=span), grid_spec=grid_spec,
        out_shape=jax.ShapeDtypeStruct((b, t, ATT_WIDTH), F32),
        compiler_params=_cparams("parallel", "parallel", "arbitrary"), name="nsa_win_attn",
    )(slopes, q, k_win, v_win, gates_r)


def _nsa_out_kernel(oc_ref, os_ref, ow_ref, z_ref, x_ref, w_ref, y_ref, *, exact):
    a = (oc_ref[...] + os_ref[...] + ow_ref[...]) * _silu(z_ref[...])
    if exact:
        y_ref[...] = x_ref[...] + jnp.dot(a, w_ref[...], preferred_element_type=F32, precision=HI)
    else:
        y_ref[...] = x_ref[...] + jnp.dot(a.astype(BF16), w_ref[...], preferred_element_type=F32)


def nsa_out(o_c, o_s, o_w, z, x, w_out, *, tm, exact=False):
    m, d = x.shape
    row = pl.BlockSpec((tm, d), lambda i: (i, 0))
    return pl.pallas_call(
        functools.partial(_nsa_out_kernel, exact=exact), grid=(m // tm,),
        in_specs=[row, row, row, row, row, pl.BlockSpec(w_out.shape, lambda i: (0, 0))],
        out_specs=row, out_shape=jax.ShapeDtypeStruct((m, d), F32),
        compiler_params=_cparams("parallel"), name="nsa_out",
    )(o_c, o_s, o_w, z, x, w_out)


LOG2E = math.log2(math.e)
Q_SCALE2 = Q_SCALE * LOG2E
ALIBI_ROWS = 16
BIG = 1.0e30
ONES_ROWS = 16
TQ = 256
TK = 512
WCH = 128


def _alibi_table():
    s = _alibi_slopes() * LOG2E
    s1 = s.astype(BF16).astype(F32)
    s2 = (s - s1).astype(BF16).astype(F32)
    s3 = (s - s1 - s2).astype(BF16).astype(F32)
    tab = jnp.stack([SEL_BLK * s1, SEL_BLK * s2, SEL_BLK * s3, s1, s2, s3, jnp.full_like(s, -BIG), jnp.zeros_like(s)], axis=1)
    return tab.reshape(-1)


def _pos_features(pos, valid, width):
    lane = lax.broadcasted_iota(jnp.int32, (pos.shape[0], width), 1)
    blk = lax.shift_right_logical(pos, int(math.log2(SEL_BLK))).astype(F32)
    rem = (pos & (SEL_BLK - 1)).astype(F32)
    f = jnp.where(lane < 3, blk, jnp.where(lane < 6, rem, 0.0))
    return jnp.where(lane == 6, jnp.where(valid, 0.0, 1.0), f)


def _nsa_in_proj_kernel(x_ref, nw_ref, wt_ref, wn_ref, *rest):
    qt_ref, kvt_ref, wint_ref, gt_ref, zt_ref, kn_ref, kwn_ref = rest[-7:]
    h = _rmsnorm(x_ref[0], nw_ref[...]).astype(BF16)

    def nt(r0, r1):
        return lax.dot_general(wt_ref[r0:r1, :], h, NT, preferred_element_type=F32)

    o1, o2, o3 = ATT_WIDTH, ATT_WIDTH + 4 * KV_WIDTH, ATT_WIDTH + 6 * KV_WIDTH
    o4 = o3 + LANES
    for r0 in range(0, o1, 512):
        qt_ref[0, r0:r0 + 512, :] = (nt(r0, r0 + 512) * Q_SCALE2).astype(BF16)
    for r0 in range(o1, o2, 512):
        kvt_ref[0, 0, r0 - o1:r0 - o1 + 512, :] = nt(r0, r0 + 512)
    wint_ref[0] = nt(o2, o3)
    gt_ref[0] = jax.nn.sigmoid(nt(o3, o4))
    for r0 in range(o4, o4 + ATT_WIDTH, 512):
        zt_ref[0, r0 - o4:r0 - o4 + 512, :] = nt(r0, r0 + 512).astype(zt_ref.dtype)
    yn = jnp.dot(h, wn_ref[...], preferred_element_type=F32)
    for j in range(3):
        for hh in range(N_KV_HEADS):
            c0 = (j * N_KV_HEADS + hh) * HEAD_DIM
            kn_ref[0, j, hh] = yn[:, c0:c0 + HEAD_DIM].astype(BF16)
    for hh in range(N_KV_HEADS):
        c0 = (3 * N_KV_HEADS + hh) * HEAD_DIM
        kwn_ref[0, hh] = yn[:, c0:c0 + HEAD_DIM].astype(BF16)


def nsa_in_proj(layer, n_layers, x, nw, w_in, kvt_all, *, tm):
    b, t, d = x.shape
    o1, o2, o3 = ATT_WIDTH, ATT_WIDTH + 4 * KV_WIDTH, ATT_WIDTH + 6 * KV_WIDTH
    o4 = o3 + 3 * N_HEADS
    w_t = w_in.T
    wt = jnp.concatenate([w_t[:o4], jnp.zeros((LANES - 3 * N_HEADS, d), w_in.dtype), w_t[o4:]], axis=0).astype(BF16)
    wn = jnp.concatenate([w_in[:, o1:o1 + 3 * KV_WIDTH], w_in[:, o2:o2 + KV_WIDTH]], axis=1).astype(BF16)
    tok = lambda rows: pl.BlockSpec((1, rows, tm), lambda i, j: (i, 0, j))
    full = lambda a: pl.BlockSpec(a.shape, lambda i, j: (0,) * a.ndim)
    tshape = lambda rows, dt: jax.ShapeDtypeStruct((b, rows, t), dt)
    args = [x, nw.reshape(1, d), wt, wn]
    in_specs = [pl.BlockSpec((1, tm, d), lambda i, j: (i, j, 0)), pl.BlockSpec((1, d), lambda i, j: (0, 0)), full(wt), full(wn)]
    aliases = {}
    if kvt_all is not None:
        args.append(kvt_all)
        in_specs.append(pl.BlockSpec(memory_space=pl.ANY))
        aliases = {len(args) - 1: 1}
    return pl.pallas_call(
        _nsa_in_proj_kernel, grid=(b, t // tm), in_specs=in_specs,
        out_specs=[tok(ATT_WIDTH), pl.BlockSpec((1, 1, 4 * KV_WIDTH, tm), lambda i, j: (layer, i, 0, j)),
                   tok(2 * KV_WIDTH), tok(LANES), tok(ATT_WIDTH),
                   pl.BlockSpec((1, 3, N_KV_HEADS, tm, HEAD_DIM), lambda i, j: (i, 0, 0, j, 0)),
                   pl.BlockSpec((1, N_KV_HEADS, tm, HEAD_DIM), lambda i, j: (i, 0, j, 0))],
        out_shape=[tshape(ATT_WIDTH, BF16), jax.ShapeDtypeStruct((n_layers, b, 4 * KV_WIDTH, t), F32),
                   tshape(2 * KV_WIDTH, F32), tshape(LANES, F32),
                   tshape(ATT_WIDTH, BF16), jax.ShapeDtypeStruct((b, 3, N_KV_HEADS, t, HEAD_DIM), BF16),
                   jax.ShapeDtypeStruct((b, N_KV_HEADS, t, HEAD_DIM), BF16)],
        input_output_aliases=aliases,
        compiler_params=_cparams("parallel", "parallel"), name="nsa_in_proj",
    )(*args)


def _topk_mask_rows(score, k):
    r = score.shape[0]
    groups = [score[8 * v:8 * v + 8] for v in range(r // 8)]
    sub = lax.broadcasted_iota(jnp.int32, (8, 1), 0)
    cnt = [jnp.zeros(g.shape, F32) for g in groups]
    for i in range(r):
        row = score[i:i + 1, :]
        for v, g in enumerate(groups):
            if 8 * v > i:
                beats = row >= g
            elif 8 * v + 7 < i:
                beats = row > g
            else:
                beats = (row > g) | ((sub > i - 8 * v) & (row == g))
            cnt[v] = cnt[v] + jnp.where(beats, 1.0, 0.0)
    return jnp.concatenate(cnt, axis=0) < float(k)


def _nsa_attn_kernel(tab_ref, qt_ref, gt_ref, kc_ref, vc_ref, ksel_ref, kwin_ref, vselt_ref, vwint_ref, ovt_ref, ot_ref,
                     ksa_s, kwa_s, kca_s, vst_s, vwt_s, vct_s, sc_s, m_s, acc_s, *, t_len, n_seg, n_blk):
    kh = pl.program_id(1)
    qi = pl.program_id(2)
    t0 = qi * TQ
    cols = GROUP * TQ
    ones = lambda n: jnp.ones((ONES_ROWS, n), BF16)

    @pl.when(qi == 0)
    def _build_keys():
        pos = lax.broadcasted_iota(jnp.int32, (t_len, 1), 0)
        onehot = jnp.where(lax.shift_right_logical(pos, int(math.log2(SEL_BLK)))
                           == lax.broadcasted_iota(jnp.int32, (1, HEAD_DIM), 1), 1.0, 0.0)
        ksa_s[:, 0:LANES] = jnp.concatenate([ksel_ref[0, 0, 0].astype(F32), onehot], axis=1).astype(BF16)
        ksa_s[:, LANES:2 * LANES] = _pos_features(pos, pos >= 0, LANES).astype(BF16)
        no_key = _pos_features(jnp.zeros((WINDOW, 1), jnp.int32), jnp.zeros((WINDOW, 1), jnp.bool_), HEAD_DIM)
        kwa_s[0:WINDOW, :] = jnp.concatenate([jnp.zeros((WINDOW, HEAD_DIM), F32), no_key], axis=1).astype(BF16)
        kwa_s[WINDOW:WINDOW + t_len, :] = jnp.concatenate(
            [kwin_ref[0, 0].astype(F32), _pos_features(pos, pos >= 0, HEAD_DIM)], axis=1).astype(BF16)
        cend = lax.broadcasted_iota(jnp.int32, (n_seg, 1), 0) * CMP_STRIDE + (CMP_BLK - 1)
        kca_s[...] = jnp.concatenate([kc_ref[0, 0, 0], _pos_features(cend, cend >= 0, HEAD_DIM)], axis=1).astype(BF16)
        vc_pad = jnp.concatenate([vc_ref[0, 0, 0], jnp.zeros((n_seg, LANES - HEAD_DIM), F32)], axis=1)
        vct_s[...] = jnp.concatenate([vc_pad.T[0:HEAD_DIM].astype(BF16), ones(n_seg)], axis=0)
        for c in range(t_len // TK):
            vst_s[c] = jnp.concatenate([vselt_ref[0, 0, :, c * TK:(c + 1) * TK].astype(BF16), ones(TK)], axis=0)
        for c in range(WINDOW // WCH):
            vwt_s[c] = jnp.zeros((HEAD_DIM + ONES_ROWS, WCH), BF16)
        for c in range(t_len // WCH):
            vwt_s[WINDOW // WCH + c] = jnp.concatenate([vwint_ref[0, :, c * WCH:(c + 1) * WCH].astype(BF16), ones(WCH)], axis=0)

    qb = qt_ref[0]
    qw = jnp.concatenate([qb[g * HEAD_DIM:(g + 1) * HEAD_DIM, :] for g in range(GROUP)], axis=1)
    frow = lax.broadcasted_iota(jnp.int32, (ALIBI_ROWS, cols), 0)
    fhead = lax.broadcasted_iota(jnp.int32, (ALIBI_ROWS, cols), 1) // TQ
    feat = jnp.zeros((ALIBI_ROWS, cols), F32)
    for g in range(GROUP):
        for r in range(7):
            feat = jnp.where((frow == r) & (fhead == g), tab_ref[(kh * GROUP + g) * 8 + r], feat)
    feat = feat.astype(BF16)
    q_base = jnp.concatenate([qw, feat, jnp.zeros((LANES - HEAD_DIM - ALIBI_ROWS, cols), BF16)], axis=0)
    t_row = t0 + lax.broadcasted_iota(jnp.int32, (1, cols), 1) % TQ

    def finish(acc):
        return acc[0:HEAD_DIM] * (1.0 / jnp.maximum(acc[HEAD_DIM:HEAD_DIM + 1], 1e-30))

    cend = lax.broadcasted_iota(jnp.int32, (n_seg, 1), 0) * CMP_STRIDE + (CMP_BLK - 1)
    ok_c = cend <= t_row
    s_c = jnp.where(ok_c, jnp.dot(kca_s[...], q_base, preferred_element_type=F32), NEG)
    m_c = jnp.max(s_c, axis=0, keepdims=True)
    p_c = jnp.where(ok_c, jnp.exp2(s_c - m_c), 0.0).astype(BF16)
    acc_c = jnp.dot(vct_s[...], p_c, preferred_element_type=F32)
    inv_c = 1.0 / jnp.maximum(acc_c[HEAD_DIM:HEAD_DIM + 1], 1e-30)
    o_cmp = acc_c[0:HEAD_DIM] * inv_c
    imp_c = jnp.dot(ovt_ref[...], p_c, preferred_element_type=F32) * inv_c
    imp = imp_c[:, 0:TQ]
    for g in range(1, GROUP):
        imp = imp + imp_c[:, g * TQ:(g + 1) * TQ]
    blk = lax.broadcasted_iota(jnp.int32, (HEAD_DIM, 1), 0)
    tq_row = t_row[:, 0:TQ]
    tb = lax.shift_right_logical(tq_row, int(math.log2(SEL_BLK)))
    forced = (blk == 0) | (blk == tb) | (blk == tb - 1)
    score = jnp.where(forced, FORCE_SCORE, jnp.where(blk * SEL_BLK <= tq_row, imp, -1.0))
    score = jnp.where(blk < n_blk, score, -2.0)
    chosen = _topk_mask_rows(score, min(N_SEL, n_blk)) & (score > -0.5)

    span = WINDOW + TQ
    k_w = kwa_s[pl.ds(pl.multiple_of(t0, TQ), span), :]
    v_w = jnp.concatenate([vwt_s[qi * (TQ // WCH) + c] for c in range(span // WCH)], axis=1)
    jj = lax.broadcasted_iota(jnp.int32, (TQ, 1), 0)
    ii = lax.broadcasted_iota(jnp.int32, (1, cols), 1) % TQ
    s_w = jnp.dot(k_w, q_base, preferred_element_type=F32)
    s_w = jnp.concatenate([jnp.where(jj >= ii, s_w[0:TQ], NEG), s_w[TQ:WINDOW],
                           jnp.where(jj <= ii, s_w[WINDOW:span], NEG)], axis=0)
    m_w = jnp.maximum(jnp.max(s_w, axis=0, keepdims=True), M_INIT)
    p_w = jnp.exp2((s_w - m_w).astype(BF16))
    o_win = finish(jnp.dot(v_w, p_w, preferred_element_type=F32))

    mrow = jnp.concatenate([jnp.where(chosen, 0.0, -BIG).astype(BF16)] * GROUP, axis=1)
    q_sel = jnp.concatenate([qw, mrow, feat, jnp.zeros((LANES - ALIBI_ROWS, cols), BF16)], axis=0)
    n_pairs = (t0 + TQ - 1) // (2 * TK) + 1
    max_pairs = t_len // (2 * TK)
    key_off = lax.broadcasted_iota(jnp.int32, (TK, 1), 0)

    def issue(j):
        for u in range(2):
            ks = (2 * j + u) * TK
            s = jnp.dot(ksa_s[ks:ks + TK, :], q_sel, preferred_element_type=F32)
            sc_s[j % 2, u] = jnp.where(ks + key_off <= t_row, s, NEG)

    def absorb(j):
        s0, s1 = sc_s[j % 2, 0], sc_s[j % 2, 1]
        m = m_s[0:1, :]
        m_new = jnp.maximum(m, jnp.maximum(jnp.max(s0, axis=0, keepdims=True), jnp.max(s1, axis=0, keepdims=True)))
        acc = jnp.exp2(m - m_new) * acc_s[...]
        for u, s in enumerate((s0, s1)):
            acc = acc + jnp.dot(vst_s[2 * j + u], jnp.exp2((s - m_new).astype(BF16)), preferred_element_type=F32)
        m_s[0:1, :] = m_new
        acc_s[...] = acc

    m_s[0:1, :] = jnp.full((1, cols), M_INIT, F32)
    acc_s[...] = jnp.zeros(acc_s.shape, F32)
    for n in range(1, max_pairs + 1):
        @pl.when(n_pairs == n)
        def _pairs(n=n):
            issue(0)
            for j in range(n):
                if j + 1 < n:
                    issue(j + 1)
                absorb(j)
    o_sel = finish(acc_s[...])

    for g in range(GROUP):
        sl = slice(g * TQ, (g + 1) * TQ)
        gate = lambda j: gt_ref[0, pl.ds(kh * 3 * GROUP + 3 * g + j, 1), :]
        mix = gate(0) * o_cmp[:, sl] + gate(1) * o_sel[:, sl] + gate(2) * o_win[:, sl]
        ot_ref[0, g * HEAD_DIM:(g + 1) * HEAD_DIM, :] = mix.astype(ot_ref.dtype)


def nsa_attention(layer, qt, gt, kvc, kn, kwn, kvt, wint):
    b, _, t = qt.shape
    n_seg = kvc.shape[3]
    n_blk = t // SEL_BLK
    assert t % (2 * TK) == 0 and n_blk <= HEAD_DIM and t >= WINDOW + TQ
    start = np.arange(n_seg) * CMP_STRIDE
    blk = np.arange(HEAD_DIM) * SEL_BLK
    ovt = jnp.asarray(((start[None, :] < blk[:, None] + SEL_BLK) & (start[None, :] + CMP_BLK > blk[:, None])
                       & (np.arange(HEAD_DIM)[:, None] < n_blk)), dtype=BF16)
    vrows = HEAD_DIM + ONES_ROWS
    grid_spec = pltpu.PrefetchScalarGridSpec(
        num_scalar_prefetch=1, grid=(b, N_KV_HEADS, t // TQ),
        in_specs=[
            pl.BlockSpec((1, GROUP * HEAD_DIM, TQ), lambda i, k, j, s: (i, k, j)),
            pl.BlockSpec((1, LANES, TQ), lambda i, k, j, s: (i, 0, j)),
            pl.BlockSpec((1, 1, 1, n_seg, HEAD_DIM), lambda i, k, j, s: (i, 0, k, 0, 0)),
            pl.BlockSpec((1, 1, 1, n_seg, HEAD_DIM), lambda i, k, j, s: (i, 1, k, 0, 0)),
            pl.BlockSpec((1, 1, 1, t, HEAD_DIM), lambda i, k, j, s: (i, 2, k, 0, 0)),
            pl.BlockSpec((1, 1, t, HEAD_DIM), lambda i, k, j, s: (i, k, 0, 0)),
            pl.BlockSpec((1, 1, HEAD_DIM, t), lambda i, k, j, s: (layer, i, 3 * N_KV_HEADS + k, 0)),
            pl.BlockSpec((1, HEAD_DIM, t), lambda i, k, j, s: (i, N_KV_HEADS + k, 0)),
            pl.BlockSpec((HEAD_DIM, n_seg), lambda i, k, j, s: (0, 0)),
        ],
        out_specs=pl.BlockSpec((1, GROUP * HEAD_DIM, TQ), lambda i, k, j, s: (i, k, j)),
        scratch_shapes=[pltpu.VMEM((t, 2 * LANES), BF16), pltpu.VMEM((WINDOW + t, LANES), BF16), pltpu.VMEM((n_seg, LANES), BF16),
                        pltpu.VMEM((t // TK, vrows, TK), BF16), pltpu.VMEM(((WINDOW + t) // WCH, vrows, WCH), BF16),
                        pltpu.VMEM((vrows, n_seg), BF16),
                        pltpu.VMEM((2, 2, TK, GROUP * TQ), F32), pltpu.VMEM((8, GROUP * TQ), F32),
                        pltpu.VMEM((vrows, GROUP * TQ), F32)])
    return pl.pallas_call(
        functools.partial(_nsa_attn_kernel, t_len=t, n_seg=n_seg, n_blk=n_blk), grid_spec=grid_spec,
        out_shape=jax.ShapeDtypeStruct((b, ATT_WIDTH, t), BF16),
        compiler_params=_cparams("parallel", "parallel", "arbitrary"), name="nsa_attn",
    )(_alibi_table(), qt, gt, kvc, kvc, kn, kwn, kvt, wint, ovt)


def _nsa_out_t_kernel(ot_ref, zt_ref, x_ref, w_ref, y_ref):
    a_t = (ot_ref[0].astype(F32) * _silu(zt_ref[0].astype(F32))).astype(BF16)
    y_ref[0] = x_ref[0] + lax.dot_general(a_t, w_ref[...], TN, preferred_element_type=F32)


def nsa_out_t(ot, zt, x, w_out, *, tm):
    b, t, d = x.shape
    tok = pl.BlockSpec((1, ATT_WIDTH, tm), lambda i, j: (i, 0, j))
    row = pl.BlockSpec((1, tm, d), lambda i, j: (i, j, 0))
    return pl.pallas_call(
        _nsa_out_t_kernel, grid=(b, t // tm),
        in_specs=[tok, tok, row, pl.BlockSpec(w_out.shape, lambda i, j: (0, 0))],
        out_specs=row, out_shape=jax.ShapeDtypeStruct((b, t, d), F32),
        compiler_params=_cparams("parallel", "parallel"), name="nsa_out",
    )(ot, zt, x, w_out)


def nsa_prompt_layer_t(layer, n_layers, x, kvt_all, nw, w_in, w_out, pe, w1, b1, w2):
    b, t, d = x.shape
    qt, kvt_all, wint, gt, zt, kn, kwn = nsa_in_proj(layer, n_layers, x, nw, w_in, kvt_all, tm=256)
    n_seg = t // CMP_STRIDE
    kvc = compress_tokens(kn[:, 0:2].reshape(b, 2, N_KV_HEADS, n_seg, CMP_STRIDE * HEAD_DIM), pe, w1, b1, w2)
    ot = nsa_attention(layer, qt, gt, kvc, kn, kwn, kvt_all, wint)
    y = nsa_out_t(ot, zt, x, w_out.astype(BF16), tm=256)
    wr = min(WINDOW, t)
    win5 = wint[:, :, t - wr:].reshape(b, 2, N_KV_HEADS, HEAD_DIM, wr).transpose(0, 4, 1, 2, 3)
    return y, kvt_all, win5


def _split_nsa_w_in(w_in, dtype):
    o1 = ATT_WIDTH
    o2 = o1 + 4 * KV_WIDTH
    o3 = o2 + 2 * KV_WIDTH
    o4 = o3 + 3 * N_HEADS
    wg = jnp.pad(w_in[:, o3:o4], ((0, 0), (0, LANES - 3 * N_HEADS)))
    return [w.astype(dtype) for w in (w_in[:, :o1], w_in[:, o1:o2], w_in[:, o2:o3], wg, w_in[:, o4:])]


NSA_ACTS = (None, None, None, "sigmoid", None)


def nsa_prompt_layer(x, nw, w_in, w_out, pe, w1, b1, w2):
    b, t, d = x.shape
    n_blk = t // SEL_BLK
    assert n_blk <= HEAD_DIM
    q, kv, win, gates, z = rms_proj(x.reshape(b * t, d), nw, _split_nsa_w_in(w_in, BF16), NSA_ACTS, tm=256, name="nsa_in_proj")
    q = q.reshape(b, t, ATT_WIDTH)
    kv5 = kv.reshape(b, t, 4, N_KV_HEADS, HEAD_DIM)
    win5 = win.reshape(b, t, 2, N_KV_HEADS, HEAD_DIM)
    kvt = kv5.transpose(0, 2, 3, 1, 4).astype(BF16)
    wint = win5.transpose(0, 2, 3, 1, 4).astype(BF16)
    gates_r = gates[:, :3 * N_HEADS].reshape(b, t, N_KV_HEADS, 3 * GROUP).transpose(0, 2, 1, 3)
    slopes = _alibi_slopes()
    n_seg = t // CMP_STRIDE
    kvc = compress_tokens(kvt[:, 0:2].reshape(b, 2, N_KV_HEADS, n_seg, CMP_STRIDE * HEAD_DIM), pe, w1, b1, w2)
    o_c, sel = cmp_select(q, kvc, gates_r, slopes, _cmp_to_sel_matrix(n_seg, n_blk), tq=128)
    onehot = (jnp.arange(t)[:, None] // SEL_BLK == jnp.arange(HEAD_DIM)[None, :]).astype(BF16)
    k_aug = jnp.concatenate([kvt[:, 2], jnp.broadcast_to(onehot, (b, N_KV_HEADS, t, HEAD_DIM))], axis=-1)
    o_s = sel_attention(q, sel, k_aug, kvt[:, 3], gates_r, slopes, tq=128, tk=512)
    o_w = win_attention(q, wint[:, 0], wint[:, 1], gates_r, slopes, tq=128)
    r2 = lambda a: a.reshape(b * t, -1)
    y = nsa_out(r2(o_c), r2(o_s), r2(o_w), z, r2(x), w_out.astype(BF16), tm=256)
    wr = min(WINDOW, t)
    return y.reshape(b, t, d), kv5, win5[:, t - wr:]


def _log_sigmoid(x):
    return jnp.minimum(x, 0.0) - jnp.log(1.0 + jnp.exp(-jnp.abs(x)))


def _mconv_body(shifted, xm, cw_ref, cb_ref, wbd_ref, wg_ref, bg_ref, q_ref, k_ref, v_ref, c_ref, g_ref, exact):
    conv = cb_ref[...]
    for j in range(CONV_W):
        conv = conv + shifted[j] * cw_ref[j:j + 1, :]
    c = _silu(conv)
    c_ref[...] = c.astype(c_ref.dtype)
    cast = (lambda a: a) if exact else (lambda a: a.astype(BF16))
    kw = dict(preferred_element_type=F32, precision=HI) if exact else dict(preferred_element_type=F32)
    gpre = bg_ref[...]
    for m, (src, dst) in enumerate(((c, q_ref), (c, k_ref), (xm, v_ref))):
        parts = []
        for gi in range(D_INNER // QKV_TILE):
            sl = slice(gi * QKV_TILE, (gi + 1) * QKV_TILE)
            y = jnp.dot(cast(src[:, sl]), wbd_ref[m, gi], **kw)
            parts.append(cast(y))
            dst[:, sl] = (y * (M_HEAD_DIM ** -0.5) if m == 1 else y).astype(dst.dtype)
        gpre = gpre + jnp.dot(jnp.concatenate(parts, axis=1), wg_ref[m * D_INNER:(m + 1) * D_INNER, :], **kw)
    lane = lax.broadcasted_iota(jnp.int32, gpre.shape, 1)
    g_ref[...] = jnp.where(lane < M_HEADS, gpre, _log_sigmoid(gpre))


def _mconv_prompt_kernel(xm_ref, halo_ref, cw_ref, cb_ref, wbd_ref, wg_ref, bg_ref, q_ref, k_ref, v_ref, c_ref, g_ref, *, tm):
    xm = xm_ref[0]
    halo = jnp.where(pl.program_id(1) == 0, 0.0, halo_ref[0])
    ext = jnp.concatenate([halo, xm], axis=0)
    shifted = [ext[5 + j:5 + j + tm] for j in range(CONV_W - 1)] + [xm]
    _mconv_body(shifted, xm, cw_ref, cb_ref, wbd_ref, wg_ref, bg_ref, q_ref.at[0], k_ref.at[0], v_ref.at[0], c_ref.at[0],
                g_ref.at[0], False)


QKV_TILE = LANES


def _mlstm_small_weights(w_qkv, w_gate, b_gate, dtype):
    nb = QKV_TILE // QKV_BLK
    w = w_qkv.reshape(3, D_INNER // QKV_TILE, nb, QKV_BLK, QKV_BLK)
    eye = jnp.eye(nb, dtype=w.dtype)
    wbd = jnp.einsum("mgnji,nk->mgnjki", w, eye).reshape(3, D_INNER // QKV_TILE, QKV_TILE, QKV_TILE)
    wg = jnp.pad(w_gate, ((0, 0), (0, LANES - 2 * M_HEADS)))
    bg = jnp.pad(b_gate, (0, LANES - 2 * M_HEADS)).reshape(1, LANES)
    return wbd.astype(dtype), wg.astype(dtype), bg


def mconv_prompt(xm, conv_w, conv_b, w_qkv, w_gate, b_gate, *, tm):
    b, t, _ = xm.shape
    wbd, wg, bg = _mlstm_small_weights(w_qkv, w_gate, b_gate, BF16)
    row = pl.BlockSpec((1, tm, D_INNER), lambda i, j: (i, j, 0))
    full = lambda a: pl.BlockSpec(a.shape, lambda i, j: (0,) * a.ndim)
    cb = conv_b.reshape(1, D_INNER)
    return pl.pallas_call(
        functools.partial(_mconv_prompt_kernel, tm=tm), grid=(b, t // tm),
        in_specs=[row, pl.BlockSpec((1, 8, D_INNER), lambda i, j: (i, jnp.maximum(j * (tm // 8) - 1, 0), 0)),
                  full(conv_w), full(cb), full(wbd), full(wg), full(bg)],
        out_specs=[row, row, row, row, pl.BlockSpec((1, tm, LANES), lambda i, j: (i, j, 0))],
        out_shape=[jax.ShapeDtypeStruct((b, t, D_INNER), BF16)] * 4 + [jax.ShapeDtypeStruct((b, t, LANES), F32)],
        compiler_params=_cparams("parallel", "parallel"), name="mlstm_conv_qkv",
    )(xm, xm, conv_w, cb, wbd, wg, bg)


CELL_HEADS = 4


def _mlstm_cell_kernel(q_ref, k_ref, v_ref, gc_ref, gr_ref, c0_ref, n0_ref, m0_ref, h_ref, cf_ref, nf_ref, mf_ref,
                       c_s, n_s, m_s, *, chunk, n_chunks):
    ci = pl.program_id(2)

    @pl.when(ci == 0)
    def _():
        c_s[...] = c0_ref[0]
        n_s[...] = n0_ref[0]
        m_s[...] = m0_ref[0]

    ri = lax.broadcasted_iota(jnp.int32, (chunk, chunk), 0)
    cj = lax.broadcasted_iota(jnp.int32, (chunk, chunk), 1)
    causal = cj <= ri
    lower = jnp.where(causal, 1.0, 0.0)
    upper = jnp.where(ri <= cj, 1.0, 0.0)
    for hh in range(CELL_HEADS):
        cols = slice(hh * M_HEAD_DIM, (hh + 1) * M_HEAD_DIM)
        q, k, v = q_ref[0, :, cols], k_ref[0, :, cols], v_ref[0, :, cols]
        icol, fcol = gc_ref[0, hh, :, 0:1], gc_ref[0, hh, :, 1:2]
        irow, frow = gr_ref[0, hh, 0:1, :], gr_ref[0, hh, 1:2, :]
        b_col = jnp.dot(lower, jnp.broadcast_to(fcol, (chunk, LANES)),
                        preferred_element_type=F32, precision=HI)[:, 0:1]
        b_row = jnp.dot(jnp.broadcast_to(frow, (8, chunk)), upper,
                        preferred_element_type=F32, precision=HI)[0:1, :]
        dmat = jnp.where(causal, b_col - b_row + irow, -jnp.inf)
        m_prev = m_s[hh, 0:1, 0:1]
        inter = b_col + m_prev
        mt = jnp.maximum(jnp.max(dmat, axis=-1, keepdims=True), inter)
        qb, kb, vb = q.astype(BF16), k.astype(BF16), v.astype(BF16)
        s = lax.dot_general(qb, kb, NT, preferred_element_type=F32) * jnp.exp(dmat - mt)
        decay = jnp.exp(inter - mt)
        num = (jnp.dot(s.astype(BF16), vb, preferred_element_type=F32)
               + decay * jnp.dot(qb, c_s[hh].astype(BF16), preferred_element_type=F32))
        den = jnp.sum(s, axis=-1, keepdims=True) + decay * jnp.sum(q * n_s[hh], axis=-1, keepdims=True)
        hc = num / jnp.maximum(jnp.abs(den), jnp.exp(-mt))
        h_ref[0, :, cols] = (hc * lax.rsqrt(jnp.mean(hc * hc, axis=-1, keepdims=True) + RMS_EPS)).astype(h_ref.dtype)
        m_new = mt[chunk - 1:chunk, :]
        b_last = b_col[chunk - 1:chunk, :]
        kw = k * jnp.exp(b_last - b_col + icol - m_new)
        carry = jnp.exp(b_last + m_prev - m_new)
        c_s[hh] = carry * c_s[hh] + lax.dot_general(kw.astype(BF16), vb, TN, preferred_element_type=F32)
        n_s[hh] = carry * n_s[hh] + jnp.sum(kw, axis=0, keepdims=True)
        m_s[hh] = jnp.broadcast_to(m_new, m_s.shape[1:])

    @pl.when(ci == n_chunks - 1)
    def _():
        cf_ref[0] = c_s[...]
        nf_ref[0] = n_s[...]
        mf_ref[0] = m_s[...]


def mlstm_cell(q, k, v, gates, c0, n0, m0, *, chunk):
    b, t, _ = q.shape
    n_chunks = t // chunk
    g_col = jnp.stack([gates[..., :M_HEADS], gates[..., M_HEADS:2 * M_HEADS]], axis=-1).transpose(0, 2, 1, 3)
    g_row = g_col.transpose(0, 1, 3, 2)
    m0b = jnp.broadcast_to(m0[:, :, None, None], (b, M_HEADS, 8, LANES))
    n0r = n0.reshape(b, M_HEADS, 1, M_HEAD_DIM)
    nh = CELL_HEADS
    head = pl.BlockSpec((1, chunk, nh * M_HEAD_DIM), lambda i, h, c: (i, c, h))
    st = lambda *blk: pl.BlockSpec((1, nh) + blk, lambda i, h, c: (i, h, 0, 0))
    hn, cf, nf, mf = pl.pallas_call(
        functools.partial(_mlstm_cell_kernel, chunk=chunk, n_chunks=n_chunks), grid=(b, M_HEADS // nh, n_chunks),
        in_specs=[head, head, head,
                  pl.BlockSpec((1, nh, chunk, 2), lambda i, h, c: (i, h, c, 0)),
                  pl.BlockSpec((1, nh, 2, chunk), lambda i, h, c: (i, h, 0, c)),
                  st(M_HEAD_DIM, M_HEAD_DIM), st(1, M_HEAD_DIM), st(8, LANES)],
        out_specs=[head, st(M_HEAD_DIM, M_HEAD_DIM), st(1, M_HEAD_DIM), st(8, LANES)],
        out_shape=[jax.ShapeDtypeStruct((b, t, D_INNER), BF16), jax.ShapeDtypeStruct((b, M_HEADS, M_HEAD_DIM, M_HEAD_DIM), F32),
                   jax.ShapeDtypeStruct((b, M_HEADS, 1, M_HEAD_DIM), F32), jax.ShapeDtypeStruct((b, M_HEADS, 8, LANES), F32)],
        scratch_shapes=[pltpu.VMEM((nh, M_HEAD_DIM, M_HEAD_DIM), F32), pltpu.VMEM((nh, 1, M_HEAD_DIM), F32),
                        pltpu.VMEM((nh, 8, LANES), F32)],
        compiler_params=_cparams("parallel", "parallel", "arbitrary"), name="mlstm_cell",
    )(q, k, v, g_col, g_row, c0, n0r, m0b)
    return hn, cf, nf.reshape(b, M_HEADS, M_HEAD_DIM), mf[:, :, 0, 0]


def _mlstm_out_kernel(hn_ref, c_ref, z_ref, x_ref, nw_ref, sk_ref, w_ref, fw_ref, y_ref, *, final, exact):
    a = (hn_ref[...].astype(F32) * nw_ref[...] + sk_ref[...] * c_ref[...].astype(F32)) * _silu(z_ref[...].astype(F32))
    if exact:
        y = x_ref[...] + jnp.dot(a, w_ref[...], preferred_element_type=F32, precision=HI)
    else:
        y = x_ref[...] + jnp.dot(a.astype(BF16), w_ref[...], preferred_element_type=F32)
    y_ref[...] = _rmsnorm(y, fw_ref[...]) if final else y


def mlstm_out(hn, c, z, x, norm_w, skip, w_out, final_w, *, tm, final, exact=False):
    m, d = x.shape
    wide = pl.BlockSpec((tm, D_INNER), lambda i: (i, 0))
    row = pl.BlockSpec((tm, d), lambda i: (i, 0))
    vec = lambda n: pl.BlockSpec((1, n), lambda i: (0, 0))
    return pl.pallas_call(
        functools.partial(_mlstm_out_kernel, final=final, exact=exact), grid=(m // tm,),
        in_specs=[wide, wide, wide, row, vec(D_INNER), vec(D_INNER), pl.BlockSpec(w_out.shape, lambda i: (0, 0)), vec(d)],
        out_specs=row, out_shape=jax.ShapeDtypeStruct((m, d), F32),
        compiler_params=_cparams("parallel"), name="mlstm_out",
    )(hn, c, z, x, norm_w.reshape(1, D_INNER), skip.reshape(1, D_INNER), w_out, final_w.reshape(1, d))


def mlstm_prompt_layer(x, nw, w_in, conv_w, conv_b, w_qkv, w_gate, b_gate, norm_w, skip, w_out, final_w, *, final):
    b, t, d = x.shape
    w_in = w_in.astype(BF16)
    xm, z = rms_proj(x.reshape(b * t, d), nw, [w_in[:, :D_INNER], w_in[:, D_INNER:]], (None, None), tm=256, out_dtypes=(F32, BF16),
                     name="mlstm_in_proj")
    xm3 = xm.reshape(b, t, D_INNER)
    q, k, v, c, gates = mconv_prompt(xm3, conv_w, conv_b, w_qkv, w_gate, b_gate, tm=256)
    c0 = jnp.zeros((b, M_HEADS, M_HEAD_DIM, M_HEAD_DIM), F32)
    n0 = jnp.zeros((b, M_HEADS, M_HEAD_DIM), F32)
    m0 = jnp.full((b, M_HEADS), -jnp.inf, F32)
    hn, cf, nf, mf = mlstm_cell(q, k, v, gates, c0, n0, m0, chunk=min(256, t))
    y = mlstm_out(hn.reshape(b * t, D_INNER), c.reshape(b * t, D_INNER), z, x.reshape(b * t, d), norm_w, skip,
                  w_out.astype(BF16), final_w, tm=256, final=final)
    return y.reshape(b, t, d), cf, nf, mf, xm3[:, t - (CONV_W - 1):]


def _mconv_sample_kernel(xm_ref, hist_ref, cw_ref, cb_ref, wbd_ref, wg_ref, bg_ref, q_ref, k_ref, v_ref, c_ref, g_ref):
    xm = xm_ref[...]
    shifted = [hist_ref[j] for j in range(CONV_W - 1)] + [xm]
    _mconv_body(shifted, xm, cw_ref, cb_ref, wbd_ref, wg_ref, bg_ref, q_ref, k_ref, v_ref, c_ref, g_ref, True)


def mconv_sample(xm, hist, conv_w, conv_b, w_qkv, w_gate, b_gate):
    b = xm.shape[0]
    wbd, wg, bg = _mlstm_small_weights(w_qkv, w_gate, b_gate, F32)
    cb = conv_b.reshape(1, D_INNER)
    full = lambda a: pl.BlockSpec(a.shape, lambda i: (0,) * a.ndim)
    args = (xm, hist, conv_w, cb, wbd, wg, bg)
    row = pl.BlockSpec((b, D_INNER), lambda i: (0, 0))
    return pl.pallas_call(
        _mconv_sample_kernel, grid=(1,), in_specs=[full(a) for a in args],
        out_specs=[row, row, row, row, pl.BlockSpec((b, LANES), lambda i: (0, 0))],
        out_shape=[jax.ShapeDtypeStruct((b, D_INNER), F32)] * 4 + [jax.ShapeDtypeStruct((b, LANES), F32)],
        compiler_params=_cparams("arbitrary"), name="mlstm_conv_qkv_sample",
    )(*args)


def _mlstm_step_kernel(q_ref, k_ref, v_ref, g_ref, c0_ref, n0_ref, *rest):
    h_ref, cf_ref, nf_ref, mf_ref = rest[-4:]
    q, k, v = q_ref[0], k_ref[0], v_ref[0]
    g = g_ref[0, 0]
    ig, fl, m0 = g[:, 0:1], g[:, 1:2], g[:, 2:3]
    c0, n0 = c0_ref[0, 0, 0], n0_ref[0, 0]
    m_new = jnp.maximum(fl + m0, ig)
    decay = jnp.exp(fl + m0 - m_new)
    sw = jnp.exp(ig - m_new)
    s = jnp.sum(q * k, axis=-1, keepdims=True) * sw
    half = LANES // 2
    qk_col = jnp.concatenate([jnp.broadcast_to(q, (half, M_HEAD_DIM)), jnp.broadcast_to(k, (half, M_HEAD_DIM))], axis=0).T
    q_col, k_col = qk_col[:, 0:1], qk_col[:, half:half + 1]
    qc = jnp.sum(q_col * c0, axis=0, keepdims=True)
    num = s * v + decay * qc
    den = s + decay * jnp.sum(q * n0, axis=-1, keepdims=True)
    hc = num / jnp.maximum(jnp.abs(den), jnp.exp(-m_new))
    h_ref[0] = hc * lax.rsqrt(jnp.mean(hc * hc, axis=-1, keepdims=True) + RMS_EPS)
    cf_ref[0, 0, 0] = decay * c0 + (k_col * sw) * v
    nf_ref[0, 0] = decay * n0 + sw * k
    mf_ref[0, 0] = jnp.broadcast_to(m_new, (1, LANES))


def mlstm_step(layer, q, k, v, gates, c_all, n0, m0, c_new_all):
    b = q.shape[0]
    gsm = jnp.stack([gates[:, :M_HEADS], gates[:, M_HEADS:2 * M_HEADS], m0], axis=-1)
    gsm = jnp.pad(gsm, ((0, 0), (0, 0), (0, LANES - 3))).reshape(b, M_HEADS, 1, LANES)
    r3 = lambda a: a.reshape(b, 1, D_INNER)
    head = pl.BlockSpec((1, 1, M_HEAD_DIM), lambda i, h: (i, 0, h))
    st = lambda *blk: pl.BlockSpec((1, 1) + blk, lambda i, h: (i, h, 0, 0))
    c_blk = pl.BlockSpec((1, 1, 1, M_HEAD_DIM, M_HEAD_DIM), lambda i, h: (layer, i, h, 0, 0))
    args = [r3(q), r3(k), r3(v), gsm, c_all, n0.reshape(b, M_HEADS, 1, M_HEAD_DIM)]
    in_specs = [head, head, head, st(1, LANES), c_blk, st(1, M_HEAD_DIM)]
    aliases = {}
    if c_new_all is not None:
        args.append(c_new_all)
        in_specs.append(pl.BlockSpec(memory_space=pl.ANY))
        aliases = {len(args) - 1: 1}
    hn, cf, nf, mf = pl.pallas_call(
        _mlstm_step_kernel, grid=(b, M_HEADS), in_specs=in_specs,
        out_specs=[head, c_blk, st(1, M_HEAD_DIM), st(1, LANES)],
        out_shape=[jax.ShapeDtypeStruct((b, 1, D_INNER), F32), jax.ShapeDtypeStruct(c_all.shape, F32),
                   jax.ShapeDtypeStruct((b, M_HEADS, 1, M_HEAD_DIM), F32), jax.ShapeDtypeStruct((b, M_HEADS, 1, LANES), F32)],
        input_output_aliases=aliases,
        compiler_params=_cparams("parallel", "parallel"), name="mlstm_step",
    )(*args)
    return hn.reshape(b, D_INNER), cf, nf.reshape(b, M_HEADS, M_HEAD_DIM), mf[:, :, 0, 0]


def mlstm_sample_layer(layer, x, conv_state, c_all, c_new_all, n0, m0, nw, w_in, conv_w, conv_b, w_qkv, w_gate, b_gate, norm_w,
                       skip, w_out, final_w, *, final):
    b = x.shape[0]
    xm, z = rms_proj(x, nw, [w_in[:, :D_INNER], w_in[:, D_INNER:]], (None, None), tm=b, exact=True, name="mlstm_in_proj_sample")
    hist = conv_state.transpose(1, 0, 2)
    q, k, v, c, gates = mconv_sample(xm, hist, conv_w, conv_b, w_qkv, w_gate, b_gate)
    hn, cf, nf, mf = mlstm_step(layer, q, k, v, gates, c_all, n0, m0, c_new_all)
    y = mlstm_out(hn, c, z, x, norm_w, skip, w_out, final_w, tm=b, final=final, exact=True)
    conv_new = jnp.concatenate([hist[1:], xm[None]], axis=0).transpose(1, 0, 2)
    return y, cf, nf, mf, conv_new


SAMPLE_ROWS = 8
PAGES_PER_STEP = 16
N_BLK_PAD = 256


def _row_scalars(vals):
    row = lax.broadcasted_iota(jnp.int32, (SAMPLE_ROWS, 1), 0)
    col = jnp.zeros((SAMPLE_ROWS, 1), F32)
    for g, v in enumerate(vals):
        col = jnp.where(row == g, v, col)
    return col


def _nsa_sample_kernel(pt_ref, slopes_ref, *refs, gp, n_groups, past_len):
    cmp_pages, sel_pages = refs[:gp], refs[gp:2 * gp]
    (q_ref, qbd_ref, kvn_ref, cwin_ref, wn_ref, g_ref, wp_ref, pe_ref, w1f_ref, b1_ref, w2_ref, ov_ref,
     ex_ref) = refs[2 * gp:2 * gp + 13]
    o_ref, wout_ref, cmp_s, kc_s, vc_s, bias_s, oc_s, m_s, l_s, acc_s = refs[-10:]
    step = pl.program_id(1)
    t = past_len
    n_seg = past_len // CMP_STRIDE
    n_blk = past_len // SEL_BLK + 1
    span = gp * PAGE_SIZE
    slope_cols = [_row_scalars([slopes_ref[kh * GROUP + g] for g in range(GROUP)]) for kh in range(N_KV_HEADS)]

    @pl.when(step < n_groups)
    def _stash():
        for i in range(gp):
            row0 = pl.multiple_of((step * gp + i) * PAGE_SIZE, PAGE_SIZE)
            for c in range(2):
                for hp in range(N_KV_HEADS // 2):
                    a = cmp_pages[i][0, 0, c, 2 * hp:2 * hp + 2].reshape(2 * HEAD_DIM, PAGE_SIZE)
                    cmp_s[c * 2 + hp, pl.ds(row0, PAGE_SIZE), :] = a.T

    @pl.when(step == n_groups - 1)
    def _compress_and_select():
        for slab in range(4):
            c, hp = divmod(slab, 2)
            x = jnp.concatenate([cmp_s[slab, pl.ds(r, n_seg, stride=CMP_STRIDE), :] for r in range(CMP_STRIDE)], axis=1)
            part = jnp.dot(x.astype(BF16), wp_ref[c], preferred_element_type=F32)
            hidc = jnp.dot(pe_ref[c], w1f_ref[c], preferred_element_type=F32)[0:1] + b1_ref[c]
            for hh in range(2):
                p0 = part[:, hh * 2 * CMP_HID:hh * 2 * CMP_HID + CMP_HID]
                p1 = part[:, hh * 2 * CMP_HID + CMP_HID:(hh + 1) * 2 * CMP_HID]
                hid = hidc + p0 + pltpu.roll(p1, n_seg - 1, 0)
                tok = jnp.dot(_silu(hid).astype(BF16), w2_ref[c], preferred_element_type=F32)
                if c == 0:
                    kc_s[2 * hp + hh] = tok
                else:
                    vc_s[2 * hp + hh] = tok
        cend = lax.broadcasted_iota(jnp.int32, (1, n_seg), 1) * CMP_STRIDE + (CMP_BLK - 1)
        d_c = (t - cend).astype(F32)
        ok = d_c >= 0.0
        row = lax.broadcasted_iota(jnp.int32, (SAMPLE_ROWS, 1), 0)
        blk = lax.broadcasted_iota(jnp.int32, (1, N_BLK_PAD), 1)
        tb = t // SEL_BLK
        forced = (blk == 0) | (blk == tb) | (blk == tb - 1)
        ii = lax.broadcasted_iota(jnp.int32, (N_BLK_PAD, N_BLK_PAD), 0)
        jj = lax.broadcasted_iota(jnp.int32, (N_BLK_PAD, N_BLK_PAD), 1)
        sel_rows = []
        for kh in range(N_KV_HEADS):
            q8 = (q_ref[0, kh] * Q_SCALE).astype(BF16)
            s = lax.dot_general(q8, kc_s[kh].astype(BF16), NT, preferred_element_type=F32) - slope_cols[kh] * d_c
            s = jnp.where(ok, s, NEG)
            m = jnp.max(s, axis=-1, keepdims=True)
            p = jnp.where(ok, jnp.exp(s - m), 0.0)
            pn = (p / jnp.maximum(jnp.sum(p, axis=-1, keepdims=True), 1e-30)).astype(BF16)
            oc_s[kh] = jnp.dot(pn, vc_s[kh].astype(BF16), preferred_element_type=F32)
            pn_heads = jnp.where(row < GROUP, pn, jnp.zeros_like(pn))
            imp = jnp.sum(jnp.dot(pn_heads, ov_ref[...], preferred_element_type=F32), axis=0, keepdims=True)
            score = jnp.where(forced, FORCE_SCORE, jnp.where(blk * SEL_BLK <= t, imp, -1.0))
            score = jnp.where(blk < n_blk, score, -2.0)
            col = jnp.broadcast_to(score, (SAMPLE_ROWS, N_BLK_PAD)).T[:, 0:1]
            beats = (col > score) | ((ii < jj) & (col == score))
            cnt = jnp.sum(jnp.where(beats, 1.0, 0.0), axis=0, keepdims=True)
            sel = jnp.where((cnt < float(min(N_SEL, n_blk))) & (score > -0.5), 1.0, 0.0)
            sel_rows.append(jnp.broadcast_to(sel, (SAMPLE_ROWS, N_BLK_PAD)))
        picked = jnp.dot(jnp.concatenate(sel_rows, axis=0).astype(BF16), ex_ref[...], preferred_element_type=F32)
        key_pos = lax.broadcasted_iota(jnp.int32, (1, past_len), 1)
        bias = jnp.where(picked > 0.5, jnp.concatenate(slope_cols, axis=0) * (key_pos - t).astype(F32), NEG)
        for gi in range(n_groups):
            bias_s[gi] = bias[:, gi * span:(gi + 1) * span]
        m_s[...] = jnp.full(m_s.shape, M_INIT, F32)
        l_s[...] = jnp.zeros(l_s.shape, F32)
        acc_s[...] = jnp.zeros(acc_s.shape, F32)

    @pl.when(step >= n_groups)
    def _selected():
        gb = step - n_groups
        kk = jnp.concatenate([sel_pages[i][0, 0, 0].reshape(KV_WIDTH, PAGE_SIZE) for i in range(gp)], axis=1).astype(BF16)
        vv = jnp.concatenate([sel_pages[i][0, 0, 1].reshape(KV_WIDTH, PAGE_SIZE) for i in range(gp)], axis=1).astype(BF16)
        s = jnp.dot(qbd_ref[0], kk, preferred_element_type=F32) + bias_s[gb]
        m_old = m_s[:, 0:1]
        m_new = jnp.maximum(m_old, jnp.max(s, axis=-1, keepdims=True))
        alpha = jnp.exp(m_old - m_new)
        p = jnp.exp(s - m_new)
        l_s[...] = jnp.broadcast_to(alpha * l_s[:, 0:1] + jnp.sum(p, axis=-1, keepdims=True), l_s.shape)
        acc_s[...] = alpha * acc_s[...] + lax.dot_general(p.astype(BF16), vv, NT, preferred_element_type=F32)
        m_s[...] = jnp.broadcast_to(m_new, m_s.shape)

    @pl.when(step == 2 * n_groups - 1)
    def _finish():
        wr = cwin_ref.shape[-1]
        d_w = wr - lax.broadcasted_iota(jnp.int32, (1, wr), 1)
        ok_w = (d_w <= WINDOW) & (t - d_w >= 0)
        for kh in range(N_KV_HEADS):
            q8 = q_ref[0, kh] * Q_SCALE
            rows = slice(kh * SAMPLE_ROWS, (kh + 1) * SAMPLE_ROWS)
            s_n = jnp.sum(q8 * kvn_ref[0, 2, kh], axis=-1, keepdims=True)
            m_old = m_s[rows, 0:1]
            m_new = jnp.maximum(m_old, s_n)
            alpha = jnp.exp(m_old - m_new)
            p_n = jnp.exp(s_n - m_new)
            l = alpha * l_s[rows, 0:1] + p_n
            acc = acc_s[rows, kh * HEAD_DIM:(kh + 1) * HEAD_DIM]
            o_sel = (alpha * acc + p_n * kvn_ref[0, 3, kh]) / jnp.maximum(l, 1e-30)
            s_w = jnp.dot(q8.astype(BF16), cwin_ref[0, 0, 0, kh].astype(BF16), preferred_element_type=F32)
            s_w = jnp.where(ok_w, s_w - slope_cols[kh] * d_w.astype(F32), NEG)
            s_wn = jnp.sum(q8 * wn_ref[0, 0, kh], axis=-1, keepdims=True)
            m_w = jnp.maximum(jnp.max(s_w, axis=-1, keepdims=True), s_wn)
            p_w = jnp.where(ok_w, jnp.exp(s_w - m_w), 0.0)
            p_wn = jnp.exp(s_wn - m_w)
            l_w = jnp.sum(p_w, axis=-1, keepdims=True) + p_wn
            o_win = (lax.dot_general(p_w.astype(BF16), cwin_ref[0, 0, 1, kh].astype(BF16), NT, preferred_element_type=F32)
                     + p_wn * wn_ref[0, 1, kh]) / jnp.maximum(l_w, 1e-30)
            gts = g_ref[0, kh]
            o_ref[0, kh] = gts[:, 0:1] * oc_s[kh] + gts[:, 1:2] * o_sel + gts[:, 2:3] * o_win
        last = lax.broadcasted_iota(jnp.int32, (HEAD_DIM, wr), 1) == wr - 1
        for c in range(2):
            for kh in range(N_KV_HEADS):
                new8 = jnp.broadcast_to(wn_ref[0, c, kh], (SAMPLE_ROWS, HEAD_DIM))
                new_col = jnp.concatenate([new8, jnp.zeros_like(new8)], axis=1).T[0:HEAD_DIM, 0:1]
                wout_ref[0, 0, c, kh] = jnp.where(last, new_col, pltpu.roll(cwin_ref[0, 0, c, kh], wr - 1, 1))


def _pair_w1(w1):
    wr = w1.reshape(2, CMP_R, CMP_STRIDE, HEAD_DIM, CMP_HID).transpose(0, 2, 3, 1, 4)
    wp = jnp.einsum("crdje,hk->crhdkje", wr, jnp.eye(2, dtype=w1.dtype))
    return wp.reshape(2, CMP_STRIDE * 2 * HEAD_DIM, 2 * CMP_R * CMP_HID).astype(BF16)


def nsa_sample_attention(layer, q, kv, win, gates, cache_kv, cache_win, win_all, page_table, pe, w1, b1, w2):
    b = q.shape[0]
    n_pages = page_table.shape[1]
    past_len = n_pages * PAGE_SIZE
    gp = PAGES_PER_STEP
    assert n_pages % gp == 0 and cache_win.shape[2] == WINDOW and past_len // SEL_BLK + 1 <= N_BLK_PAD
    n_groups = n_pages // gp
    n_seg = past_len // CMP_STRIDE
    pad_rows = lambda a: jnp.pad(a, ((0, 0), (0, 0), (0, SAMPLE_ROWS - GROUP), (0, 0)))
    q4 = pad_rows(q.reshape(b, N_KV_HEADS, GROUP, HEAD_DIM))
    g4 = pad_rows(gates[:, :3 * N_HEADS].reshape(b, N_KV_HEADS, GROUP, 3))
    q_bd = jnp.einsum("bkgd,kj->bkgjd", q4 * Q_SCALE, jnp.eye(N_KV_HEADS, dtype=F32))
    q_bd = q_bd.reshape(b, N_KV_HEADS * SAMPLE_ROWS, KV_WIDTH).astype(BF16)
    kvn = kv.reshape(b, 4, N_KV_HEADS, 1, HEAD_DIM)
    wn = win.reshape(b, 2, N_KV_HEADS, 1, HEAD_DIM)
    cache_t = cache_kv.transpose(0, 1, 3, 4, 5, 2)
    cwin_t = cache_win.transpose(0, 1, 3, 4, 5, 2)
    start = np.arange(n_seg) * CMP_STRIDE
    blk = np.arange(N_BLK_PAD) * SEL_BLK
    ov = jnp.asarray((start[:, None] < blk[None, :] + SEL_BLK) & (start[:, None] + CMP_BLK > blk[None, :]), dtype=BF16)
    pe8 = jnp.broadcast_to(pe.reshape(2, 1, CMP_BLK * HEAD_DIM), (2, 8, CMP_BLK * HEAD_DIM)).astype(BF16)
    w1f = w1.reshape(2, CMP_BLK * HEAD_DIM, CMP_HID).astype(BF16)
    expand = (jnp.arange(N_BLK_PAD)[:, None] == jnp.arange(past_len)[None, :] // SEL_BLK).astype(BF16)
    consts = (_pair_w1(w1), pe8, w1f, b1.reshape(2, 1, CMP_HID), w2.astype(BF16), ov, expand)
    page_blk = (1, 1, 2, N_KV_HEADS, HEAD_DIM, PAGE_SIZE)

    def cmp_map(i):
        return lambda bi, s, pt, sl: (layer, pt[bi * n_pages + jnp.minimum(s, n_groups - 1) * gp + i], 0, 0, 0, 0)

    def sel_map(i):
        return lambda bi, s, pt, sl: (layer, pt[bi * n_pages + jnp.maximum(s - n_groups, 0) * gp + i], 1, 0, 0, 0)

    per_b = lambda a: pl.BlockSpec((1,) + a.shape[1:], lambda bi, s, pt, sl: (bi,) + (0,) * (a.ndim - 1))
    const = lambda a: pl.BlockSpec(a.shape, lambda bi, s, pt, sl: (0,) * a.ndim)
    in_specs = [pl.BlockSpec(page_blk, cmp_map(i)) for i in range(gp)] + [pl.BlockSpec(page_blk, sel_map(i)) for i in range(gp)]
    in_specs += [per_b(q4), per_b(q_bd), per_b(kvn),
                 pl.BlockSpec((1, 1) + cwin_t.shape[2:], lambda bi, s, pt, sl: (layer, bi, 0, 0, 0, 0)),
                 per_b(wn), per_b(g4)] + [const(a) for a in consts]
    all_rows = N_KV_HEADS * SAMPLE_ROWS
    small = pltpu.VMEM((all_rows, LANES), F32)
    win_blk = pl.BlockSpec((1, 1) + cwin_t.shape[2:], lambda bi, s, pt, sl: (layer, bi, 0, 0, 0, 0))
    args = [page_table.reshape(-1), _alibi_slopes(), *([cache_t] * (2 * gp)), q4, q_bd, kvn, cwin_t, wn, g4, *consts]
    aliases = {}
    if win_all is not None:
        args.append(win_all)
        in_specs.append(pl.BlockSpec(memory_space=pl.ANY))
        aliases = {len(args) - 1: 1}
    grid_spec = pltpu.PrefetchScalarGridSpec(
        num_scalar_prefetch=2, grid=(b, 2 * n_groups), in_specs=in_specs, out_specs=[per_b(q4), win_blk],
        scratch_shapes=[pltpu.VMEM((4, past_len, LANES), F32),
                        pltpu.VMEM((N_KV_HEADS, n_seg, HEAD_DIM), F32), pltpu.VMEM((N_KV_HEADS, n_seg, HEAD_DIM), F32),
                        pltpu.VMEM((n_groups, all_rows, gp * PAGE_SIZE), F32),
                        pltpu.VMEM((N_KV_HEADS, SAMPLE_ROWS, HEAD_DIM), F32), small, small,
                        pltpu.VMEM((all_rows, KV_WIDTH), F32)])
    o, win_all = pl.pallas_call(
        functools.partial(_nsa_sample_kernel, gp=gp, n_groups=n_groups, past_len=past_len), grid_spec=grid_spec,
        out_shape=[jax.ShapeDtypeStruct(q4.shape, F32), jax.ShapeDtypeStruct(cwin_t.shape, F32)],
        input_output_aliases=aliases,
        compiler_params=_cparams("parallel", "arbitrary"), name="nsa_sample_attn",
    )(*args)
    return o[:, :, :GROUP].reshape(b, ATT_WIDTH), win_all


def nsa_sample_layer(layer, x, cache_kv, cache_win, win_all, page_table, nw, w_in, w_out, pe, w1, b1, w2):
    b = x.shape[0]
    q, kv, win, gates, z = rms_proj(x, nw, _split_nsa_w_in(w_in, F32), NSA_ACTS, tm=b, exact=True, name="nsa_in_proj_sample")
    o, win_all = nsa_sample_attention(layer, q, kv, win, gates, cache_kv, cache_win, win_all, page_table, pe, w1, b1, w2)
    zero = jnp.zeros_like(o)
    y = nsa_out(o, zero, zero, z, x, w_out, tm=b, exact=True)
    return y, kv.reshape(b, 1, 4, N_KV_HEADS, HEAD_DIM), win_all


def kernel(x_prompt, x_sample, cache_kv, cache_win, state_C, state_n, state_m, state_conv, page_table, norm_w, final_norm_w,
           nsa_w_in, nsa_w_out, nsa_cmp_pe, nsa_cmp_w1, nsa_cmp_b1, nsa_cmp_w2, m_w_in, m_conv_w, m_conv_b, m_w_qkv, m_w_gate,
           m_b_gate, m_norm_w, m_skip, m_w_out):
    assert DEPTH % 2 == 0 and x_sample.shape[1] == 1
    yp, ys = x_prompt, x_sample[:, 0]
    outs = {name: [] for name in ("kv_s", "win_p", "C_p", "n_p", "n_s", "m_p", "m_s", "cv_p", "cv_s")}
    c_sample = None
    kvt_prompt = None
    win_sample_t = None
    n_nsa = (DEPTH + 1) // 2
    for i in range(DEPTH):
        l = i // 2
        if i % 2 == 0:
            prm = (nsa_w_in[l], nsa_w_out[l], nsa_cmp_pe[l], nsa_cmp_w1[l], nsa_cmp_b1[l], nsa_cmp_w2[l])
            yp, kvt_prompt, wp = nsa_prompt_layer_t(l, n_nsa, yp, kvt_prompt, norm_w[i], *prm)
            ys, kvs, win_sample_t = nsa_sample_layer(l, ys, cache_kv, cache_win, win_sample_t, page_table, norm_w[i], *prm)
            for name, val in (("kv_s", kvs), ("win_p", wp)):
                outs[name].append(val)
        else:
            prm = (m_w_in[l], m_conv_w[l], m_conv_b[l], m_w_qkv[l], m_w_gate[l], m_b_gate[l], m_norm_w[l], m_skip[l], m_w_out[l])
            final = i == DEPTH - 1
            yp, cp, np_, mp, cvp = mlstm_prompt_layer(yp, norm_w[i], *prm, final_norm_w, final=final)
            ys, c_sample, ns, ms, cvs = mlstm_sample_layer(l, ys, state_conv[l], state_C, c_sample, state_n[l], state_m[l],
                                                           norm_w[i], *prm, final_norm_w, final=final)
            for name, val in (("C_p", cp), ("n_p", np_), ("n_s", ns), ("m_p", mp), ("m_s", ms), ("cv_p", cvp), ("cv_s", cvs)):
                outs[name].append(val)
    st = {name: jnp.stack(vals) for name, vals in outs.items()}
    b, t = x_prompt.shape[:2]
    kv_prompt = kvt_prompt.reshape(n_nsa, b, 4, N_KV_HEADS, HEAD_DIM, t).transpose(0, 1, 5, 2, 3, 4)
    win_sample = win_sample_t.transpose(0, 1, 5, 2, 3, 4)
    return (yp, ys[:, None], kv_prompt, st["kv_s"], st["win_p"], win_sample, st["C_p"], c_sample, st["n_p"], st["n_s"],
            st["m_p"], st["m_s"], st["cv_p"], st["cv_s"])
```
